```python
import math
import jax, jax.numpy as jnp
from jax import lax
import numpy as np

D_MODEL = 1024
BATCH = 2
SEQ = 8192
DEPTH = 2

CTX_LEN = 256
GRID_W = 64

HEAD_DIM = 64
ATTN_WIDTH = D_MODEL // 2
N_Q_HEADS = ATTN_WIDTH // HEAD_DIM
N_KV_HEADS = N_Q_HEADS // 4
WINDOW = 128
BLOCK = 128
ROPE_BASE = 10000.0
NEG_INF = -1e30

S5_WIDTH = D_MODEL // 4
S5_GROUP = 16
S5_GROUPS = S5_WIDTH // S5_GROUP
S5_STATE = 64

CONV_WIDTH = D_MODEL - ATTN_WIDTH - S5_WIDTH
CONV_K = 31

MIX_WIDTH = ATTN_WIDTH + S5_WIDTH + CONV_WIDTH
Q_END = ATTN_WIDTH
K_END = Q_END + N_KV_HEADS * HEAD_DIM
V_END = K_END + N_KV_HEADS * HEAD_DIM
U_END = V_END + S5_WIDTH
IN_WIDTH = U_END + 2 * CONV_WIDTH

N_EXPERTS = 16
EXPERT_FF = 2 * D_MODEL
CAPACITY_FACTOR = 2

DEEPNORM_ALPHA = (2.0 * DEPTH) ** 0.25
DEEPNORM_BETA = (8.0 * DEPTH) ** -0.25
LN_EPS = 1e-5

kernel_name = 'hybrid_s5_conformer_swa_ec_dit_block'


def layer_norm(x, g=None, b=None):
    x32 = x.astype(jnp.float32)
    mu = jnp.mean(x32, axis=-1, keepdims=True)
    var = jnp.mean(jnp.square(x32 - mu), axis=-1, keepdims=True)
    y = (x32 - mu) * lax.rsqrt(var + LN_EPS)
    if g is not None:
        y = y * g.astype(jnp.float32) + b.astype(jnp.float32)
    return y.astype(x.dtype)


def modulate(h, shift, scale):
    return h * (1 + scale) + shift


def rope_1d(x, pos):
    f = x.shape[-1] // 2
    inv_freq = ROPE_BASE ** (-jnp.arange(f, dtype=jnp.float32) / f)
    ang = pos.astype(jnp.float32)[:, None] * inv_freq
    cos = jnp.cos(ang)[None, :, None, :]
    sin = jnp.sin(ang)[None, :, None, :]
    x1 = x[..., :f].astype(jnp.float32)
    x2 = x[..., f:].astype(jnp.float32)
    return jnp.concatenate([x1 * cos - x2 * sin, x2 * cos + x1 * sin], axis=-1).astype(x.dtype)


def axial_rope(x, pos_row, pos_col):
    half = x.shape[-1] // 2
    return jnp.concatenate([rope_1d(x[..., :half], pos_row), rope_1d(x[..., half:], pos_col)], axis=-1)


def window_attention(q, k, v, k_ctx, v_ctx, sink):
    bsz, seq, n_q, dh = q.shape
    g = n_q // N_KV_HEADS
    nb = seq // BLOCK
    lc = k_ctx.shape[1]
    qb = q.reshape(bsz, nb, BLOCK, N_KV_HEADS, g, dh)

    def band(t):
        tp = jnp.pad(t, ((0, 0), (BLOCK, BLOCK), (0, 0), (0, 0))).reshape(bsz, nb + 2, BLOCK, N_KV_HEADS, dh)
        return jnp.concatenate([tp[:, :-2], tp[:, 1:-1], tp[:, 2:]], axis=2)

    k_win, v_win = band(k), band(v)
    scale = dh ** -0.5
    s_win = jnp.einsum('bnqhgd,bnkhd->bnhgqk', qb, k_win, preferred_element_type=jnp.float32) * scale
    qi = jnp.arange(BLOCK)
    kj = jnp.arange(3 * BLOCK)
    rel = (kj[None, :] - BLOCK) - qi[:, None]
    kpos = jnp.arange(nb)[:, None] * BLOCK - BLOCK + kj[None, :]
    mask = (jnp.abs(rel) <= WINDOW)[None] & ((kpos >= 0) & (kpos < seq))[:, None, :]
    s_win = jnp.where(mask[None, :, None, None], s_win, NEG_INF)
    s_ctx = jnp.einsum('bnqhgd,bkhd->bnhgqk', qb, k_ctx, preferred_element_type=jnp.float32) * scale
    s_sink = jnp.broadcast_to(sink.reshape(N_KV_HEADS, g, 1, 1).astype(jnp.float32), s_ctx.shape[:-1] + (1,))
    p = jax.nn.softmax(jnp.concatenate([s_win, s_ctx, s_sink], axis=-1), axis=-1)
    p_win = p[..., :3 * BLOCK].astype(v.dtype)
    p_ctx = p[..., 3 * BLOCK:3 * BLOCK + lc].astype(v.dtype)
    o = (jnp.einsum('bnhgqk,bnkhd->bnqhgd', p_win, v_win)
         + jnp.einsum('bnhgqk,bkhd->bnqhgd', p_ctx, v_ctx))
    return o.reshape(bsz, seq, n_q * dh)


def context_attention(q, k, v, sink):
    bsz, lc, n_q, dh = q.shape
    g = n_q // N_KV_HEADS
    qg = q.reshape(bsz, lc, N_KV_HEADS, g, dh)
    s = jnp.einsum('bqhgd,bkhd->bhgqk', qg, k, preferred_element_type=jnp.float32) * dh ** -0.5
    s_sink = jnp.broadcast_to(sink.reshape(N_KV_HEADS, g, 1, 1).astype(jnp.float32), s.shape[:-1] + (1,))
    p = jax.nn.softmax(jnp.concatenate([s, s_sink], axis=-1), axis=-1)[..., :lc].astype(v.dtype)
    return jnp.einsum('bhgqk,bkhd->bqhgd', p, v).reshape(bsz, lc, n_q * dh)


def _ssm_combine(e1, e2):
    a1, b1 = e1
    a2, b2 = e2
    return a1 * a2, a2 * b1 + b2


def s5_discretize(lam_re, lam_im, log_dt, b_re, b_im):
    lam = lax.complex(lam_re.astype(jnp.float32), lam_im.astype(jnp.float32))
    dt = jnp.exp(log_dt.astype(jnp.float32))[:, None]
    lbar = jnp.exp(lam * dt)
    bmat = lax.complex(b_re.astype(jnp.float32), b_im.astype(jnp.float32))
    bbar = ((lbar - 1) / lam)[..., None] * bmat
    return lbar, bbar


def s5_scan(u_c, lbar, bbar, h0, reverse):
    bu = jnp.einsum('blgc,gpc->blgp', u_c, bbar)
    if h0 is not None:
        bu = bu.at[:, -1 if reverse else 0].add(lbar * h0)
    a = jnp.broadcast_to(lbar, bu.shape)
    _, h = lax.associative_scan(_ssm_combine, (a, bu), reverse=reverse, axis=1)
    return h


def s5_mixer(u, lam_re, lam_im, log_dt, b_re, b_im, c_re, c_im, d_skip, h0, with_output):
    bsz, seq, _ = u.shape
    u_g = u.astype(jnp.float32).reshape(bsz, seq, S5_GROUPS, S5_GROUP)
    u_c = u_g.astype(jnp.complex64)
    y = d_skip.astype(jnp.float32).reshape(S5_GROUPS, S5_GROUP) * u_g if with_output else None
    finals = []
    for direction, reverse in enumerate((False, True)):
        lbar, bbar = s5_discretize(lam_re[direction], lam_im[direction], log_dt[direction],
                                   b_re[direction], b_im[direction])
        h = s5_scan(u_c, lbar, bbar, None if h0 is None else h0[direction], reverse)
        finals.append(h[:, 0] if reverse else h[:, -1])
        if with_output:
            cmat = lax.complex(c_re[direction].astype(jnp.float32), c_im[direction].astype(jnp.float32))
            y = y + jnp.real(jnp.einsum('blgp,gcp->blgc', h, cmat))
    if with_output:
        y = y.reshape(bsz, seq, S5_WIDTH).astype(u.dtype)
    return y, (finals[0], finals[1])


def s5_glu(y, w_glu, b_glu):
    z = jax.nn.gelu(y)
    return z * jax.nn.sigmoid(z @ w_glu + b_glu)


def conformer_conv(p, w_dw, b_dw, ln_g, ln_b, w_pw, b_pw):
    a, g = jnp.split(p, 2, axis=-1)
    h = a * jax.nn.sigmoid(g)
    h = lax.conv_general_dilated(h, w_dw, window_strides=(1,), padding=[(CONV_K // 2, CONV_K // 2)],
                                 dimension_numbers=('NWC', 'WIO', 'NWC'),
                                 feature_group_count=CONV_WIDTH) + b_dw
    h = jax.nn.silu(layer_norm(h, ln_g, ln_b))
    return h @ w_pw + b_pw


def expert_choice(h, w_router, w_gate, w_up, w_down):
    bsz, n, d = h.shape
    cap = CAPACITY_FACTOR * n // N_EXPERTS
    aff = jax.nn.softmax(jnp.einsum('bnd,de->ben', h, w_router, preferred_element_type=jnp.float32), axis=1)
    gates, idx = lax.top_k(aff, cap)
    xs = jax.vmap(lambda hb, ib: hb[ib])(h, idx)
    hid = (jax.nn.silu(jnp.einsum('becd,edf->becf', xs, w_gate))
           * jnp.einsum('becd,edf->becf', xs, w_up))
    y = jnp.einsum('becf,efd->becd', hid, w_down) * gates[..., None].astype(h.dtype)
    return jax.vmap(lambda ib, yb: jnp.zeros((n, d), yb.dtype).at[ib.reshape(-1)].add(yb.reshape(-1, d)))(idx, y)


def setup_inputs(seed: int = 0) -> dict:
    key = jax.random.key(seed)
    ks = iter(jax.random.split(key, 48))
    f32 = jnp.float32
    D = D_MODEL

    def nrm(shape, scale):
        return scale * jax.random.normal(next(ks), shape, f32)

    x = nrm((BATCH, SEQ, D), 1.0)
    c = nrm((BATCH, D), 1.0)
    ctx = nrm((BATCH, CTX_LEN, D), 1.0)
    c_ctx = nrm((D,), 1.0)
    w_mod = nrm((DEPTH, D, 6 * D), 0.5 * D ** -0.5)
    b_mod = nrm((DEPTH, 6 * D), 0.02)
    w_in = nrm((DEPTH, D, IN_WIDTH), D ** -0.5)
    w_in = w_in.at[:, :, K_END:V_END].multiply(DEEPNORM_BETA)
    b_in = nrm((DEPTH, IN_WIDTH), 0.02)
    attn_sink = nrm((DEPTH, N_Q_HEADS), 0.5)
    n_idx = jnp.arange(S5_STATE, dtype=f32)
    s5_lam_re = -0.5 + nrm((DEPTH, 2, S5_GROUPS, S5_STATE), 0.01)
    s5_lam_im = math.pi * n_idx + nrm((DEPTH, 2, S5_GROUPS, S5_STATE), 0.01)
    s5_log_dt = jax.random.uniform(next(ks), (DEPTH, 2, S5_GROUPS), f32, math.log(1e-3), math.log(1e-1))
    s5_b_re = nrm((DEPTH, 2, S5_GROUPS, S5_STATE, S5_GROUP), (2 * S5_GROUP) ** -0.5)
    s5_b_im = nrm((DEPTH, 2, S5_GROUPS, S5_STATE, S5_GROUP), (2 * S5_GROUP) ** -0.5)
    s5_c_re = nrm((DEPTH, 2, S5_GROUPS, S5_GROUP, S5_STATE), S5_STATE ** -0.5)
    s5_c_im = nrm((DEPTH, 2, S5_GROUPS, S5_GROUP, S5_STATE), S5_STATE ** -0.5)
    s5_d = nrm((DEPTH, S5_WIDTH), 1.0)
    s5_w_glu = nrm((DEPTH, S5_WIDTH, S5_WIDTH), S5_WIDTH ** -0.5)
    s5_b_glu = nrm((DEPTH, S5_WIDTH), 0.02)
    conv_w_dw = nrm((DEPTH, CONV_K, 1, CONV_WIDTH), CONV_K ** -0.5)
    conv_b_dw = nrm((DEPTH, CONV_WIDTH), 0.02)
    conv_ln_g = 1.0 + nrm((DEPTH, CONV_WIDTH), 0.05)
    conv_ln_b = nrm((DEPTH, CONV_WIDTH), 0.02)
    conv_w_pw = nrm((DEPTH, CONV_WIDTH, CONV_WIDTH), CONV_WIDTH ** -0.5)
    conv_b_pw = nrm((DEPTH, CONV_WIDTH), 0.02)
    w_out = nrm((DEPTH, MIX_WIDTH, D), DEEPNORM_BETA * MIX_WIDTH ** -0.5)
    b_out = nrm((DEPTH, D), 0.02)
    ln1_g = 1.0 + nrm((DEPTH, D), 0.05)
    ln1_b = nrm((DEPTH, D), 0.02)
    w_router = nrm((DEPTH, D, N_EXPERTS), D ** -0.5)
    exp_w_gate = nrm((DEPTH, N_EXPERTS, D, EXPERT_FF), D ** -0.5)
    exp_w_up = nrm((DEPTH, N_EXPERTS, D, EXPERT_FF), D ** -0.5)
    exp_w_down = nrm((DEPTH, N_EXPERTS, EXPERT_FF, D), DEEPNORM_BETA * EXPERT_FF ** -0.5)
    ln2_g = 1.0 + nrm((DEPTH, D), 0.05)
    ln2_b = nrm((DEPTH, D), 0.02)
    return {'x': x, 'c': c, 'ctx': ctx, 'c_ctx': c_ctx, 'w_mod': w_mod, 'b_mod': b_mod,
            'w_in': w_in, 'b_in': b_in, 'attn_sink': attn_sink,
            's5_lam_re': s5_lam_re, 's5_lam_im': s5_lam_im, 's5_log_dt': s5_log_dt,
            's5_b_re': s5_b_re, 's5_b_im': s5_b_im, 's5_c_re': s5_c_re, 's5_c_im': s5_c_im,
            's5_d': s5_d, 's5_w_glu': s5_w_glu, 's5_b_glu': s5_b_glu,
            'conv_w_dw': conv_w_dw, 'conv_b_dw': conv_b_dw, 'conv_ln_g': conv_ln_g, 'conv_ln_b': conv_ln_b,
            'conv_w_pw': conv_w_pw, 'conv_b_pw': conv_b_pw, 'w_out': w_out, 'b_out': b_out,
            'ln1_g': ln1_g, 'ln1_b': ln1_b, 'w_router': w_router, 'exp_w_gate': exp_w_gate,
            'exp_w_up': exp_w_up, 'exp_w_down': exp_w_down, 'ln2_g': ln2_g, 'ln2_b': ln2_b}


def reference(x, c, ctx, c_ctx, w_mod, b_mod, w_in, b_in, attn_sink,
              s5_lam_re, s5_lam_im, s5_log_dt, s5_b_re, s5_b_im, s5_c_re, s5_c_im,
              s5_d, s5_w_glu, s5_b_glu, conv_w_dw, conv_b_dw, conv_ln_g, conv_ln_b,
              conv_w_pw, conv_b_pw, w_out, b_out, ln1_g, ln1_b, w_router, exp_w_gate,
              exp_w_up, exp_w_down, ln2_g, ln2_b):
    bsz, seq, _ = x.shape
    lc = ctx.shape[1]
    ROWS = seq // GRID_W
    pos_row = jnp.repeat(jnp.arange(ROWS, dtype=jnp.int32), GRID_W)
    pos_col = jnp.tile(jnp.arange(GRID_W, dtype=jnp.int32), ROWS)
    alpha = DEEPNORM_ALPHA
    silu_c = jax.nn.silu(c)
    silu_cc = jax.nn.silu(c_ctx)
    xc = ctx
    for l in range(DEPTH):
        last = l == DEPTH - 1
        mod_x = jnp.split((silu_c @ w_mod[l] + b_mod[l])[:, None, :], 6, axis=-1)
        mod_c = jnp.split((silu_cc @ w_mod[l] + b_mod[l])[None, None, :], 6, axis=-1)

        hx = modulate(layer_norm(x), mod_x[0], mod_x[1])
        hc = modulate(layer_norm(xc), mod_c[0], mod_c[1])
        px = hx @ w_in[l] + b_in[l]
        col0, col1 = (Q_END, U_END) if last else (0, IN_WIDTH)
        pc = hc @ w_in[l][:, col0:col1] + b_in[l][col0:col1]

        k_c = pc[..., Q_END - col0:K_END - col0].reshape(bsz, lc, N_KV_HEADS, HEAD_DIM)
        v_c = pc[..., K_END - col0:V_END - col0].reshape(bsz, lc, N_KV_HEADS, HEAD_DIM)
        q = axial_rope(px[..., :Q_END].reshape(bsz, seq, N_Q_HEADS, HEAD_DIM), pos_row, pos_col)
        k = axial_rope(px[..., Q_END:K_END].reshape(bsz, seq, N_KV_HEADS, HEAD_DIM), pos_row, pos_col)
        v = px[..., K_END:V_END].reshape(bsz, seq, N_KV_HEADS, HEAD_DIM)
        attn_x = window_attention(q, k, v, k_c, v_c, attn_sink[l])

        s5_params = (s5_lam_re[l], s5_lam_im[l], s5_log_dt[l], s5_b_re[l], s5_b_im[l],
                     s5_c_re[l], s5_c_im[l], s5_d[l])
        y_c_s5, h_ctx = s5_mixer(pc[..., V_END - col0:U_END - col0], *s5_params, None, not last)
        y_x_s5, _ = s5_mixer(px[..., V_END:U_END], *s5_params, h_ctx, True)
        s5_x = s5_glu(y_x_s5, s5_w_glu[l], s5_b_glu[l])

        conv_params = (conv_w_dw[l], conv_b_dw[l], conv_ln_g[l], conv_ln_b[l], conv_w_pw[l], conv_b_pw[l])
        conv_x = conformer_conv(px[..., U_END:], *conv_params)

        y_mix = jnp.concatenate([attn_x, s5_x, conv_x], axis=-1) @ w_out[l] + b_out[l]
        x_mid = layer_norm(alpha * x + mod_x[2] * y_mix, ln1_g[l], ln1_b[l])

        if not last:
            q_c = pc[..., :Q_END].reshape(bsz, lc, N_Q_HEADS, HEAD_DIM)
            attn_c = context_attention(q_c, k_c, v_c, attn_sink[l])
            s5_c = s5_glu(y_c_s5, s5_w_glu[l], s5_b_glu[l])
            conv_c = conformer_conv(pc[..., U_END:], *conv_params)
            yc_mix = jnp.concatenate([attn_c, s5_c, conv_c], axis=-1) @ w_out[l] + b_out[l]
            xc = layer_norm(alpha * xc + mod_c[2] * yc_mix, ln1_g[l], ln1_b[l])

        h2 = modulate(layer_norm(x_mid), mod_x[3], mod_x[4])
        moe_x = expert_choice(h2, w_router[l], exp_w_gate[l], exp_w_up[l], exp_w_down[l])
        x = layer_norm(alpha * x_mid + mod_x[5] * moe_x, ln2_g[l], ln2_b[l])
        if not last:
            hc2 = modulate(layer_norm(xc), mod_c[3], mod_c[4])
            moe_c = expert_choice(hc2, w_router[l], exp_w_gate[l], exp_w_up[l], exp_w_down[l])
            xc = layer_norm(alpha * xc + mod_c[5] * moe_c, ln2_g[l], ln2_b[l])
    return x
```

```python
import functools
import math

import jax
import jax.numpy as jnp
import numpy as np
from jax import lax
from jax.experimental import pallas as pl
from jax.experimental.pallas import tpu as pltpu

D_MODEL = 1024
DEPTH = 2
GRID_W = 64
HEAD_DIM = 64
ATTN_WIDTH = D_MODEL // 2
N_Q_HEADS = ATTN_WIDTH // HEAD_DIM
N_KV_HEADS = N_Q_HEADS // 4
Q_PER_KV = N_Q_HEADS // N_KV_HEADS
KV_WIDTH = N_KV_HEADS * HEAD_DIM
BLOCK = 128
ROPE_BASE = 10000.0
NEG_INF = -1e30
S5_WIDTH = D_MODEL // 4
S5_GROUP = 16
S5_GROUPS = S5_WIDTH // S5_GROUP
S5_STATE = 64
CONV_WIDTH = D_MODEL - ATTN_WIDTH - S5_WIDTH
CONV_K = 31
CONV_HALO = 16
Q_END = ATTN_WIDTH
K_END = Q_END + KV_WIDTH
V_END = K_END + KV_WIDTH
U_END = V_END + S5_WIDTH
IN_WIDTH = U_END + 2 * CONV_WIDTH
N_EXPERTS = 16
EXPERT_FF = 2 * D_MODEL
CAPACITY_FACTOR = 2
DEEPNORM_ALPHA = (2.0 * DEPTH) ** 0.25
LN_EPS = 1e-5

LANES = 128
SUBLANES = 8
S5_CHUNKS = S5_WIDTH * S5_STATE // S5_GROUP // LANES
CH_PER_CHUNK = S5_WIDTH // S5_CHUNKS
VMEM_LIMIT = 56 * 1024 * 1024

F32 = jnp.float32
BF16 = jnp.bfloat16


def _cparams(*sem):
    return pltpu.CompilerParams(dimension_semantics=sem, vmem_limit_bytes=VMEM_LIMIT)


def _dot(a, b):
    return jnp.dot(a, b, preferred_element_type=F32)


def _dot_nt(a, b):
    return lax.dot_general(a, b, (((1,), (1,)), ((), ())), preferred_element_type=F32)


def _split_bf16(x):
    hi = x.astype(BF16)
    lo = (x - hi.astype(F32)).astype(BF16)
    return hi, lo


def _dot3(a, b_hi, b_lo):
    a_hi, a_lo = _split_bf16(a)
    return _dot(a_hi, b_hi) + (_dot(a_lo, b_hi) + _dot(a_hi, b_lo))


def _sigmoid(x):
    return 1.0 / (1.0 + jnp.exp(-x))


def _silu(x):
    return x * _sigmoid(x)


def _gelu_tanh(x):
    c = math.sqrt(2.0 / math.pi)
    return 0.5 * x * (1.0 + jnp.tanh(c * (x + 0.044715 * (x * x * x))))


def _layer_norm(x):
    mu = jnp.mean(x, axis=-1, keepdims=True)
    xc = x - mu
    var = jnp.mean(xc * xc, axis=-1, keepdims=True)
    return xc * lax.rsqrt(var + LN_EPS)


def _mod_kernel(c_ref, w_ref, b_ref, o_ref):
    s = _silu(c_ref[...])
    w = w_ref[0]
    w_hi, w_lo = _split_bf16(w)
    o_ref[0] = _dot3(s, w_hi, w_lo) + b_ref[0]


def _modulation(cond, w_mod, b_mod):
    tn = 1536
    n = w_mod.shape[-1]
    return pl.pallas_call(
        _mod_kernel,
        grid=(DEPTH, n // tn),
        in_specs=[
            pl.BlockSpec((SUBLANES, D_MODEL), lambda l, j: (0, 0)),
            pl.BlockSpec((1, D_MODEL, tn), lambda l, j: (l, 0, j)),
            pl.BlockSpec((1, 1, tn), lambda l, j: (l, 0, j)),
        ],
        out_specs=pl.BlockSpec((1, SUBLANES, tn), lambda l, j: (l, 0, j)),
        out_shape=jax.ShapeDtypeStruct((DEPTH, SUBLANES, n), F32),
        compiler_params=_cparams("arbitrary", "arbitrary"),
        name="modulation",
    )(cond, w_mod, b_mod.reshape(DEPTH, 1, n))


def _rope_chunk(x, cos, sin_signed):
    lane = lax.broadcasted_iota(jnp.int32, x.shape, 1)
    first = (lane % 32) < 16
    partner = jnp.where(first, pltpu.roll(x, LANES - 16, 1), pltpu.roll(x, 16, 1))
    return x * cos + partner * sin_signed


def _inproj_kernel(x_ref, shift_ref, scale_ref, w_ref, b_ref, cos_ref, sin_ref,
                   q_ref, k_ref, v_ref, u_ref, cg_ref, *, rope):
    x = x_ref[0]
    h = _layer_norm(x) * (1.0 + scale_ref[0]) + shift_ref[0]
    p = _dot(h.astype(BF16), w_ref[...]) + b_ref[...]
    scale = HEAD_DIM ** -0.5
    if rope:
        cos = cos_ref[...]
        sin = sin_ref[...]
    for j in range(ATTN_WIDTH // LANES):
        qc = p[:, j * LANES:(j + 1) * LANES]
        if rope:
            qc = _rope_chunk(qc, cos, sin)
        q_ref[0, :, j * LANES:(j + 1) * LANES] = (qc * scale).astype(BF16)
    kc = p[:, Q_END:K_END]
    if rope:
        kc = _rope_chunk(kc, cos, sin)
    k_ref[0] = kc.astype(BF16)
    v_ref[0] = p[:, K_END:V_END].astype(BF16)
    u_ref[0] = p[:, V_END:U_END]
    a = p[:, U_END:U_END + CONV_WIDTH]
    g = p[:, U_END + CONV_WIDTH:]
    cg_ref[0] = a * _sigmoid(g)


def _in_projection(x, shift, scale, w_in_bf, b_in, cos_t, sin_t, *, rope, tm):
    bsz, seq, _ = x.shape
    tok = lambda w: pl.BlockSpec((1, tm, w), lambda b, i: (b, i, 0))
    vec = pl.BlockSpec((1, 1, D_MODEL), lambda b, i: (b, 0, 0))
    tab = pl.BlockSpec((tm, LANES), lambda b, i: (i, 0))
    return pl.pallas_call(
        functools.partial(_inproj_kernel, rope=rope),
        grid=(bsz, seq // tm),
        in_specs=[
            tok(D_MODEL), vec, vec,
            pl.BlockSpec((D_MODEL, IN_WIDTH), lambda b, i: (0, 0)),
            pl.BlockSpec((1, IN_WIDTH), lambda b, i: (0, 0)),
            tab, tab,
        ],
        out_specs=[tok(ATTN_WIDTH), tok(KV_WIDTH), tok(KV_WIDTH), tok(S5_WIDTH), tok(CONV_WIDTH)],
        out_shape=[
            jax.ShapeDtypeStruct((bsz, seq, ATTN_WIDTH), BF16),
            jax.ShapeDtypeStruct((bsz, seq, KV_WIDTH), BF16),
            jax.ShapeDtypeStruct((bsz, seq, KV_WIDTH), BF16),
            jax.ShapeDtypeStruct((bsz, seq, S5_WIDTH), F32),
            jax.ShapeDtypeStruct((bsz, seq, CONV_WIDTH), F32),
        ],
        compiler_params=_cparams("arbitrary", "arbitrary"),
        name="in_projection",
    )(x, shift, scale, w_in_bf, b_in.reshape(1, IN_WIDTH), cos_t, sin_t)


def _rope_tables(seq):
    lane = np.arange(LANES)
    i = lane % HEAD_DIM
    use_row = i < HEAD_DIM // 2
    f = HEAD_DIM // 4
    inv_freq = jnp.asarray(ROPE_BASE, F32) ** (-jnp.asarray(i % f, F32) / f)
    t = jnp.arange(seq, dtype=jnp.int32)
    pos = jnp.where(use_row[None, :], (t // GRID_W)[:, None], (t % GRID_W)[:, None]).astype(F32)
    ang = pos * inv_freq[None, :]
    sign = jnp.where((lane % (2 * f)) < f, -1.0, 1.0).astype(F32)
    return jnp.cos(ang), jnp.sin(ang) * sign[None, :]


def _attn_kernel(*refs, window):
    if window:
        q_ref, kp_ref, kc_ref, kn_ref, vp_ref, vc_ref, vn_ref, kx_ref, vx_ref, sink_ref, o_ref = refs
    else:
        q_ref, kx_ref, vx_ref, sink_ref, o_ref = refs
    n = pl.program_id(1)
    nb = pl.num_programs(1)
    q = q_ref[0]
    tq = q.shape[0]
    if window:
        row = lax.broadcasted_iota(jnp.int32, (tq, BLOCK), 0)
        col = lax.broadcasted_iota(jnp.int32, (tq, BLOCK), 1)
        ok_prev = (col >= row) & (n > 0)
        ok_next = (col <= row) & (n < nb - 1)
    outs = []
    for hq in range(N_Q_HEADS):
        hk = hq // Q_PER_KV
        ks = slice(hk * HEAD_DIM, (hk + 1) * HEAD_DIM)
        qh = q[:, hq * HEAD_DIM:(hq + 1) * HEAD_DIM]
        sink = sink_ref[hq:hq + 1, 0:1]
        s_x = _dot_nt(qh, kx_ref[0][:, ks])
        m = jnp.maximum(jnp.max(s_x, axis=-1, keepdims=True), sink)
        if window:
            s_p = jnp.where(ok_prev, _dot_nt(qh, kp_ref[0][:, ks]), NEG_INF)
            s_c = _dot_nt(qh, kc_ref[0][:, ks])
            s_n = jnp.where(ok_next, _dot_nt(qh, kn_ref[0][:, ks]), NEG_INF)
            m = jnp.maximum(m, jnp.max(s_p, axis=-1, keepdims=True))
            m = jnp.maximum(m, jnp.max(s_c, axis=-1, keepdims=True))
            m = jnp.maximum(m, jnp.max(s_n, axis=-1, keepdims=True))
        p_x = jnp.exp(s_x - m)
        den = jnp.sum(p_x, axis=-1, keepdims=True) + jnp.exp(sink - m)
        acc = _dot(p_x.astype(BF16), vx_ref[0][:, ks])
        if window:
            for s_w, v_ref in ((s_p, vp_ref), (s_c, vc_ref), (s_n, vn_ref)):
                p_w = jnp.exp(s_w - m)
                den = den + jnp.sum(p_w, axis=-1, keepdims=True)
                acc = acc + _dot(p_w.astype(BF16), v_ref[0][:, ks])
        outs.append(acc * (1.0 / den))
    o_ref[0] = jnp.concatenate(outs, axis=-1).astype(BF16)


def _attention(q, k, v, k_ctx, v_ctx, sink_rep, *, window):
    bsz, seq, _ = q.shape
    lc = k_ctx.shape[1]
    ctx_spec = pl.BlockSpec((1, lc, KV_WIDTH), lambda b, i: (b, 0, 0))
    sink_spec = pl.BlockSpec((N_Q_HEADS, LANES), lambda b, i: (0, 0))
    if window:
        tq = BLOCK
        nb = seq // tq
        prev = pl.BlockSpec((1, tq, KV_WIDTH), lambda b, i: (b, jnp.maximum(i - 1, 0), 0))
        cur = pl.BlockSpec((1, tq, KV_WIDTH), lambda b, i: (b, i, 0))
        nxt = pl.BlockSpec((1, tq, KV_WIDTH), lambda b, i: (b, jnp.minimum(i + 1, nb - 1), 0))
        in_specs = [pl.BlockSpec((1, tq, ATTN_WIDTH), lambda b, i: (b, i, 0)),
                    prev, cur, nxt, prev, cur, nxt, ctx_spec, ctx_spec, sink_spec]
        args = (q, k, k, k, v, v, v, k_ctx, v_ctx, sink_rep)
    else:
        tq = seq
        nb = 1
        in_specs = [pl.BlockSpec((1, tq, ATTN_WIDTH), lambda b, i: (b, i, 0)), ctx_spec, ctx_spec, sink_spec]
        args = (q, k_ctx, v_ctx, sink_rep)
    return pl.pallas_call(
        functools.partial(_attn_kernel, window=window),
        grid=(bsz, nb),
        in_specs=in_specs,
        out_specs=pl.BlockSpec((1, tq, ATTN_WIDTH), lambda b, i: (b, i, 0)),
        out_shape=jax.ShapeDtypeStruct((bsz, seq, ATTN_WIDTH), BF16),
        compiler_params=_cparams("arbitrary", "arbitrary"),
        name="window_attention" if window else "context_attention",
    )(*args)


def _s5_kernel(uf_ref, ub_ref, h0_ref, rep_ref, fold_ref, mask_ref, bst_ref, a_ref, cst_ref,
               yf_ref, yb_ref, hfin_ref, bu_scr, h_scr, *, bsz, tc):
    i = pl.program_id(0)

    @pl.when(i == 0)
    def _():
        h_scr[...] = h0_ref[...]

    mask = mask_ref[...]
    rep = rep_ref[...]
    rows = SUBLANES * tc
    for d, u_ref in enumerate((uf_ref, ub_ref)):
        for b in range(bsz):
            u_rep = _dot(rep, u_ref[b].astype(BF16))
            lhs = (u_rep.reshape(tc, SUBLANES, S5_WIDTH) * mask[None]).reshape(rows, S5_WIDTH)
            bu_scr[d * bsz + b] = _dot(lhs.astype(BF16), bst_ref[d])

    n_chain = 2 * bsz
    a_re = [a_ref[d, 0] for d in range(2)]
    a_im = [a_ref[d, 1] for d in range(2)]

    def step(t, carry):
        new = []
        for c in range(n_chain):
            d = c // bsz
            tt = t if d == 0 else tc - 1 - t
            r0 = pl.multiple_of(tt * SUBLANES, SUBLANES)
            h_re, h_im = carry[2 * c], carry[2 * c + 1]
            bu_re = bu_scr[c, pl.ds(r0, SUBLANES), 0:LANES]
            bu_im = bu_scr[c, pl.ds(r0, SUBLANES), LANES:2 * LANES]
            n_re = a_re[d] * h_re - a_im[d] * h_im + bu_re
            n_im = a_re[d] * h_im + a_im[d] * h_re + bu_im
            bu_scr[c, pl.ds(r0, SUBLANES), 0:LANES] = n_re
            bu_scr[c, pl.ds(r0, SUBLANES), LANES:2 * LANES] = n_im
            new += [n_re, n_im]
        return tuple(new)

    init = []
    for c in range(n_chain):
        init += [h_scr[c, :, 0:LANES], h_scr[c, :, LANES:2 * LANES]]
    fin = lax.fori_loop(0, tc, step, tuple(init), unroll=4)
    for c in range(n_chain):
        h_scr[c, :, 0:LANES] = fin[2 * c]
        h_scr[c, :, LANES:2 * LANES] = fin[2 * c + 1]
    hfin_ref[...] = h_scr[...]

    fold = fold_ref[...]
    for d, y_ref in enumerate((yf_ref, yb_ref)):
        for b in range(bsz):
            y_rows = _dot(bu_scr[d * bsz + b].astype(BF16), cst_ref[d])
            x_sel = (y_rows.reshape(tc, SUBLANES, S5_WIDTH) * mask[None]).reshape(rows, S5_WIDTH)
            y_ref[b] = _dot(fold, x_sel.astype(BF16))


def _s5_scan(u, h0, consts, bst, a_tiles, cst, *, tc):
    bsz, seq, _ = u.shape
    nch = seq // tc
    rep, fold, mask = consts
    full = lambda shape: pl.BlockSpec(shape, lambda i: (0,) * len(shape))
    fwd = pl.BlockSpec((bsz, tc, S5_WIDTH), lambda i: (0, i, 0))
    bwd = pl.BlockSpec((bsz, tc, S5_WIDTH), lambda i: (0, nch - 1 - i, 0))
    state = (2 * bsz, SUBLANES, 2 * LANES)
    return pl.pallas_call(
        functools.partial(_s5_kernel, bsz=bsz, tc=tc),
        grid=(nch,),
        in_specs=[fwd, bwd, full(state), full(rep.shape), full(fold.shape), full(mask.shape),
                  full(bst.shape), full(a_tiles.shape), full(cst.shape)],
        out_specs=[fwd, bwd, full(state)],
        out_shape=[jax.ShapeDtypeStruct(u.shape, F32), jax.ShapeDtypeStruct(u.shape, F32),
                   jax.ShapeDtypeStruct(state, F32)],
        scratch_shapes=[pltpu.VMEM((2 * bsz, SUBLANES * tc, 2 * LANES), F32), pltpu.VMEM(state, F32)],
        compiler_params=_cparams("arbitrary"),
        name="s5_scan",
    )(u, u, h0, rep, fold, mask, bst, a_tiles, cst)


def _s5_constants(tc):
    rows = SUBLANES * tc
    r = np.arange(rows)
    rep = (r[:, None] // SUBLANES == np.arange(tc)[None, :]).astype(np.float32)
    mask = (np.arange(S5_WIDTH)[None, :] // CH_PER_CHUNK == np.arange(SUBLANES)[:, None]).astype(np.float32)
    return jnp.asarray(rep, BF16), jnp.asarray(rep.T, BF16), jnp.asarray(mask, F32)


def _s5_params(lam_re, lam_im, log_dt, b_re, b_im, c_re, c_im):
    dt = jnp.exp(log_dt)[..., None]
    mag = jnp.exp(lam_re * dt)
    l_re = mag * jnp.cos(lam_im * dt)
    l_im = mag * jnp.sin(lam_im * dt)
    den = lam_re * lam_re + lam_im * lam_im
    f_re = ((l_re - 1.0) * lam_re + l_im * lam_im) / den
    f_im = (l_im * lam_re - (l_re - 1.0) * lam_im) / den
    bb_re = f_re[..., None] * b_re - f_im[..., None] * b_im
    bb_im = f_re[..., None] * b_im + f_im[..., None] * b_re
    a_tiles = jnp.stack([l_re.reshape(2, SUBLANES, LANES), l_im.reshape(2, SUBLANES, LANES)], axis=1)

    half = S5_GROUPS // SUBLANES
    eye = jnp.eye(half, dtype=F32)

    def in_mat(bb):
        t = bb.reshape(2, SUBLANES, half, S5_STATE, S5_GROUP)
        m = jnp.einsum('dsgpc,gh->dsgchp', t, eye)
        return m.reshape(2, S5_WIDTH, half * S5_STATE)

    def out_mat(cc):
        t = cc.reshape(2, SUBLANES, half, S5_GROUP, S5_STATE)
        m = jnp.einsum('dsgcp,gh->dhpsgc', t, eye)
        return m.reshape(2, half * S5_STATE, S5_WIDTH)

    bst = jnp.concatenate([in_mat(bb_re), in_mat(bb_im)], axis=-1).astype(BF16)
    cst = jnp.concatenate([out_mat(c_re), out_mat(-c_im)], axis=1).astype(BF16)
    return bst, a_tiles, cst


def _conv_kernel(prev_ref, cur_ref, next_ref, wdw_ref, bdw_ref, g_ref, b_ref, wpw_ref, bpw_ref,
                 o_ref, win_ref, *, tm):
    i = pl.program_id(1)
    nt = pl.num_programs(1)
    zero = jnp.zeros((CONV_HALO, CONV_WIDTH), F32)
    win_ref[0:CONV_HALO] = jnp.where(i > 0, prev_ref[0], zero)
    win_ref[CONV_HALO:CONV_HALO + tm] = cur_ref[0]
    win_ref[CONV_HALO + tm:] = jnp.where(i < nt - 1, next_ref[0], zero)
    acc = jnp.zeros((tm, CONV_WIDTH), F32) + bdw_ref[...]
    off = CONV_HALO - CONV_K // 2
    for k in range(CONV_K):
        acc = acc + win_ref[off + k:off + k + tm, :] * wdw_ref[k:k + 1, :]
    h = _silu(_layer_norm(acc) * g_ref[...] + b_ref[...])
    o_ref[0] = (_dot(h.astype(BF16), wpw_ref[...]) + bpw_ref[...]).astype(BF16)


def _conformer_conv(cg, w_dw, b_dw, ln_g, ln_b, w_pw_bf, b_pw, *, tm):
    bsz, seq, _ = cg.shape
    hb = tm // CONV_HALO
    last = seq // CONV_HALO - 1
    row = lambda a: a.reshape(1, CONV_WIDTH)
    vec = pl.BlockSpec((1, CONV_WIDTH), lambda b, i: (0, 0))
    return pl.pallas_call(
        functools.partial(_conv_kernel, tm=tm),
        grid=(bsz, seq // tm),
        in_specs=[
            pl.BlockSpec((1, CONV_HALO, CONV_WIDTH), lambda b, i: (b, jnp.maximum(i * hb - 1, 0), 0)),
            pl.BlockSpec((1, tm, CONV_WIDTH), lambda b, i: (b, i, 0)),
            pl.BlockSpec((1, CONV_HALO, CONV_WIDTH), lambda b, i: (b, jnp.minimum((i + 1) * hb, last), 0)),
            pl.BlockSpec((CONV_K + 1, CONV_WIDTH), lambda b, i: (0, 0)),
            vec, vec, vec,
            pl.BlockSpec((CONV_WIDTH, CONV_WIDTH), lambda b, i: (0, 0)),
            vec,
        ],
        out_specs=pl.BlockSpec((1, tm, CONV_WIDTH), lambda b, i: (b, i, 0)),
        out_shape=jax.ShapeDtypeStruct((bsz, seq, CONV_WIDTH), BF16),
        scratch_shapes=[pltpu.VMEM((tm + 2 * CONV_HALO, CONV_WIDTH), F32)],
        compiler_params=_cparams("arbitrary", "arbitrary"),
        name="conformer_conv",
    )(cg, cg, cg, jnp.pad(w_dw.reshape(CONV_K, CONV_WIDTH), ((0, 1), (0, 0))), row(b_dw), row(ln_g),
      row(ln_b), w_pw_bf, row(b_pw))


def _mixout_kernel(attn_ref, yf_ref, yb_ref, u_ref, conv_ref, x_ref,
                   dskip_ref, wglu_ref, bglu_ref, wout_ref, bout_ref,
                   gate_ref, g1_ref, b1_ref, shift_ref, scale_ref, wr_hi_ref, wr_lo_ref,
                   xmid_ref, h2_ref, logit_ref):
    y = dskip_ref[...] * u_ref[0] + yf_ref[0] + yb_ref[0]
    z = _gelu_tanh(y)
    s5 = z * _sigmoid(_dot(z.astype(BF16), wglu_ref[...]) + bglu_ref[...])
    y_mix = (_dot(attn_ref[0], wout_ref[0:ATTN_WIDTH, :])
             + _dot(s5.astype(BF16), wout_ref[ATTN_WIDTH:ATTN_WIDTH + S5_WIDTH, :])
             + _dot(conv_ref[0], wout_ref[ATTN_WIDTH + S5_WIDTH:, :])
             + bout_ref[...])
    r = DEEPNORM_ALPHA * x_ref[0] + gate_ref[0] * y_mix
    x_mid = _layer_norm(r) * g1_ref[...] + b1_ref[...]
    xmid_ref[0] = x_mid
    h2 = _layer_norm(x_mid) * (1.0 + scale_ref[0]) + shift_ref[0]
    h2_ref[0] = h2.astype(BF16)
    logit_ref[0] = _dot3(h2, wr_hi_ref[...], wr_lo_ref[...])


def _mixer_output(attn, yf, yb, u, conv, x, d_skip, w_glu_bf, b_glu, w_out_bf, b_out,
                  gate1, ln_g, ln_b, shift2, scale2, wr_hi, wr_lo, *, tm):
    bsz, seq, _ = x.shape
    tok = lambda w: pl.BlockSpec((1, tm, w), lambda b, i: (b, i, 0))
    full = lambda r, c: pl.BlockSpec((r, c), lambda b, i: (0, 0))
    bvec = pl.BlockSpec((1, 1, D_MODEL), lambda b, i: (b, 0, 0))
    row = lambda a: a.reshape(1, -1)
    return pl.pallas_call(
        _mixout_kernel,
        grid=(bsz, seq // tm),
        in_specs=[
            tok(ATTN_WIDTH), tok(S5_WIDTH), tok(S5_WIDTH), tok(S5_WIDTH), tok(CONV_WIDTH), tok(D_MODEL),
            full(1, S5_WIDTH), full(S5_WIDTH, S5_WIDTH), full(1, S5_WIDTH),
            full(D_MODEL, D_MODEL), full(1, D_MODEL),
            bvec, full(1, D_MODEL), full(1, D_MODEL), bvec, bvec,
            full(D_MODEL, LANES), full(D_MODEL, LANES),
        ],
        out_specs=[tok(D_MODEL), tok(D_MODEL), tok(LANES)],
        out_shape=[
            jax.ShapeDtypeStruct((bsz, seq, D_MODEL), F32),
            jax.ShapeDtypeStruct((bsz, seq, D_MODEL), BF16),
            jax.ShapeDtypeStruct((bsz, seq, LANES), F32),
        ],
        compiler_params=_cparams("arbitrary", "arbitrary"),
        name="mixer_output",
    )(attn, yf, yb, u, conv, x, row(d_skip), w_glu_bf, row(b_glu), w_out_bf, row(b_out),
      gate1, row(ln_g), row(ln_b), shift2, scale2, wr_hi, wr_lo)


def _ffn_kernel(xs_ref, gates_ref, wg_ref, wu_ref, wd_ref, y_ref):
    j = pl.program_id(1)
    xs = xs_ref[0]
    hid = _silu(_dot(xs, wg_ref[0])) * _dot(xs, wu_ref[0])
    part = _dot(hid.astype(BF16), wd_ref[0])

    @pl.when(j == 0)
    def _():
        y_ref[0] = part

    @pl.when(j > 0)
    def _():
        y_ref[0] += part

    @pl.when(j == pl.num_programs(1) - 1)
    def _():
        g = gates_ref[0]
        y_ref[0] = y_ref[0] * jnp.concatenate([g] * (D_MODEL // LANES), axis=-1)


def _expert_ffn(xs, gates_rep, wg_bf, wu_bf, wd_bf, *, tf):
    n_e, rows, _ = xs.shape
    return pl.pallas_call(
        _ffn_kernel,
        grid=(n_e, EXPERT_FF // tf),
        in_specs=[
            pl.BlockSpec((1, rows, D_MODEL), lambda e, j: (e, 0, 0)),
            pl.BlockSpec((1, rows, LANES), lambda e, j: (e, 0, 0)),
            pl.BlockSpec((1, D_MODEL, tf), lambda e, j: (e, 0, j)),
            pl.BlockSpec((1, D_MODEL, tf), lambda e, j: (e, 0, j)),
            pl.BlockSpec((1, tf, D_MODEL), lambda e, j: (e, j, 0)),
        ],
        out_specs=pl.BlockSpec((1, rows, D_MODEL), lambda e, j: (e, 0, 0)),
        out_shape=jax.ShapeDtypeStruct((n_e, rows, D_MODEL), F32),
        compiler_params=_cparams("arbitrary", "arbitrary"),
        name="expert_ffn",
    )(xs, gates_rep, wg_bf, wu_bf, wd_bf)


def _expert_choice(h2_bf, logits, wg_bf, wu_bf, wd_bf):
    bsz, n, d = h2_bf.shape
    cap = CAPACITY_FACTOR * n // N_EXPERTS
    aff = jax.nn.softmax(logits[..., :N_EXPERTS], axis=-1)
    gates, idx = lax.top_k(jnp.swapaxes(aff, 1, 2), cap)
    xs = jax.vmap(lambda hb, ib: hb[ib])(h2_bf, idx)
    xs = jnp.swapaxes(xs, 0, 1).reshape(N_EXPERTS, bsz * cap, d)
    g = jnp.swapaxes(gates, 0, 1).reshape(N_EXPERTS, bsz * cap, 1)
    y = _expert_ffn(xs, jnp.broadcast_to(g, (N_EXPERTS, bsz * cap, LANES)), wg_bf, wu_bf, wd_bf,
                    tf=256)
    y = jnp.swapaxes(y.reshape(N_EXPERTS, bsz, cap, d), 0, 1)
    return jax.vmap(lambda ib, yb: jnp.zeros((n, d), F32).at[ib.reshape(-1)].add(yb.reshape(-1, d)))(idx, y)


def _post_kernel(xmid_ref, moe_ref, gate_ref, g_ref, b_ref, o_ref):
    r = DEEPNORM_ALPHA * xmid_ref[0] + gate_ref[0] * moe_ref[0]
    o_ref[0] = _layer_norm(r) * g_ref[...] + b_ref[...]


def _post_moe(x_mid, moe, gate2, ln_g, ln_b, *, tm):
    bsz, seq, _ = x_mid.shape
    tok = pl.BlockSpec((1, tm, D_MODEL), lambda b, i: (b, i, 0))
    vec = pl.BlockSpec((1, D_MODEL), lambda b, i: (0, 0))
    return pl.pallas_call(
        _post_kernel,
        grid=(bsz, seq // tm),
        in_specs=[tok, tok, pl.BlockSpec((1, 1, D_MODEL), lambda b, i: (b, 0, 0)), vec, vec],
        out_specs=tok,
        out_shape=jax.ShapeDtypeStruct(x_mid.shape, F32),
        compiler_params=_cparams("arbitrary", "arbitrary"),
        name="post_moe",
    )(x_mid, moe, gate2, ln_g.reshape(1, D_MODEL), ln_b.reshape(1, D_MODEL))


def kernel(x, c, ctx, c_ctx, w_mod, b_mod, w_in, b_in, attn_sink, s5_lam_re, s5_lam_im, s5_log_dt,
           s5_b_re, s5_b_im, s5_c_re, s5_c_im, s5_d, s5_w_glu, s5_b_glu, conv_w_dw, conv_b_dw,
           conv_ln_g, conv_ln_b, conv_w_pw, conv_b_pw, w_out, b_out, ln1_g, ln1_b, w_router,
           exp_w_gate, exp_w_up, exp_w_down, ln2_g, ln2_b):
    bsz, seq, d = x.shape
    lc = ctx.shape[1]
    tm_x, tm_c = 512, lc

    cond = jnp.zeros((SUBLANES, d), F32).at[:bsz].set(c).at[bsz].set(c_ctx)
    mods = _modulation(cond, w_mod, b_mod)
    cos_t, sin_t = _rope_tables(seq)
    s5_consts_x = _s5_constants(256)
    s5_consts_c = _s5_constants(lc)
    zero_state = jnp.zeros((2 * bsz, SUBLANES, 2 * LANES), F32)

    xc = ctx
    for l in range(DEPTH):
        last = l == DEPTH - 1
        mod_x = [mods[l, :bsz, k * d:(k + 1) * d].reshape(bsz, 1, d) for k in range(6)]
        mod_c = [jnp.broadcast_to(mods[l, bsz, k * d:(k + 1) * d].reshape(1, 1, d), (bsz, 1, d))
                 for k in range(6)]
        w_in_bf = w_in[l].astype(BF16)
        w_out_bf = w_out[l].astype(BF16)
        w_glu_bf = s5_w_glu[l].astype(BF16)
        w_pw_bf = conv_w_pw[l].astype(BF16)
        wr = jnp.pad(w_router[l], ((0, 0), (0, LANES - N_EXPERTS)))
        wr_hi, wr_lo = _split_bf16(wr)
        wg_bf = exp_w_gate[l].astype(BF16)
        wu_bf = exp_w_up[l].astype(BF16)
        wd_bf = exp_w_down[l].astype(BF16)
        sink_rep = jnp.broadcast_to(attn_sink[l][:, None], (N_Q_HEADS, LANES))
        bst, a_tiles, cst = _s5_params(s5_lam_re[l], s5_lam_im[l], s5_log_dt[l], s5_b_re[l], s5_b_im[l],
                                       s5_c_re[l], s5_c_im[l])

        q, k, v, u, cg = _in_projection(x, mod_x[0], mod_x[1], w_in_bf, b_in[l], cos_t, sin_t,
                                        rope=True, tm=tm_x)
        q_c, k_c, v_c, u_c, cg_c = _in_projection(xc, mod_c[0], mod_c[1], w_in_bf, b_in[l],
                                                  cos_t[:lc], sin_t[:lc], rope=False, tm=tm_c)

        attn_x = _attention(q, k, v, k_c, v_c, sink_rep, window=True)
        yf_c, yb_c, h_ctx = _s5_scan(u_c, zero_state, s5_consts_c, bst, a_tiles, cst, tc=lc)
        yf, yb, _ = _s5_scan(u, h_ctx, s5_consts_x, bst, a_tiles, cst, tc=256)
        conv_args = (conv_w_dw[l], conv_b_dw[l], conv_ln_g[l], conv_ln_b[l], w_pw_bf, conv_b_pw[l])
        conv_x = _conformer_conv(cg, *conv_args, tm=256)
        mix_args = (s5_d[l], w_glu_bf, s5_b_glu[l], w_out_bf, b_out[l])
        x_mid, h2, logits = _mixer_output(attn_x, yf, yb, u, conv_x, x, *mix_args, mod_x[2], ln1_g[l],
                                          ln1_b[l], mod_x[3], mod_x[4], wr_hi, wr_lo, tm=tm_x)
        moe_x = _expert_choice(h2, logits, wg_bf, wu_bf, wd_bf)
        x = _post_moe(x_mid, moe_x, mod_x[5], ln2_g[l], ln2_b[l], tm=tm_x)

        if not last:
            attn_c = _attention(q_c, None, None, k_c, v_c, sink_rep, window=False)
            conv_c = _conformer_conv(cg_c, *conv_args, tm=lc)
            xc_mid, hc2, logits_c = _mixer_output(attn_c, yf_c, yb_c, u_c, conv_c, xc, *mix_args, mod_c[2],
                                                  ln1_g[l], ln1_b[l], mod_c[3], mod_c[4], wr_hi, wr_lo,
                                                  tm=tm_c)
            moe_c = _expert_choice(hc2, logits_c, wg_bf, wu_bf, wd_bf)
            xc = _post_moe(xc_mid, moe_c, mod_c[5], ln2_g[l], ln2_b[l], tm=tm_c)
    return x
```

```python
import functools
import math

import jax
import jax.numpy as jnp
import numpy as np
from jax import lax
from jax.experimental import pallas as pl
from jax.experimental.pallas import tpu as pltpu

D_MODEL = 1024
DEPTH = 2
GRID_W = 64
HEAD_DIM = 64
ATTN_WIDTH = D_MODEL // 2
N_Q_HEADS = ATTN_WIDTH // HEAD_DIM
N_KV_HEADS = N_Q_HEADS // 4
Q_PER_KV = N_Q_HEADS // N_KV_HEADS
KV_WIDTH = N_KV_HEADS * HEAD_DIM
BLOCK = 128
ROPE_BASE = 10000.0
NEG_INF = -1e30
S5_WIDTH = D_MODEL // 4
S5_GROUP = 16
S5_GROUPS = S5_WIDTH // S5_GROUP
S5_STATE = 64
CONV_WIDTH = D_MODEL - ATTN_WIDTH - S5_WIDTH
CONV_K = 31
CONV_HALO = 16
Q_END = ATTN_WIDTH
K_END = Q_END + KV_WIDTH
V_END = K_END + KV_WIDTH
U_END = V_END + S5_WIDTH
IN_WIDTH = U_END + 2 * CONV_WIDTH
N_EXPERTS = 16
EXPERT_FF = 2 * D_MODEL
CAPACITY_FACTOR = 2
DEEPNORM_ALPHA = (2.0 * DEPTH) ** 0.25
LN_EPS = 1e-5

LANES = 128
SUBLANES = 8
S5_CHUNKS = S5_WIDTH * S5_STATE // S5_GROUP // LANES
CH_PER_CHUNK = S5_WIDTH // S5_CHUNKS
VMEM_LIMIT = 56 * 1024 * 1024

F32 = jnp.float32
BF16 = jnp.bfloat16


def _cparams(*sem):
    return pltpu.CompilerParams(dimension_semantics=sem, vmem_limit_bytes=VMEM_LIMIT)


def _dot(a, b):
    return jnp.dot(a, b, preferred_element_type=F32)


def _dot_nt(a, b):
    return lax.dot_general(a, b, (((1,), (1,)), ((), ())), preferred_element_type=F32)


def _split_bf16(x):
    hi = x.astype(BF16)
    lo = (x - hi.astype(F32)).astype(BF16)
    return hi, lo


def _dot3(a, b_hi, b_lo):
    a_hi, a_lo = _split_bf16(a)
    return _dot(a_hi, b_hi) + (_dot(a_lo, b_hi) + _dot(a_hi, b_lo))


def _sigmoid(x):
    return 1.0 / (1.0 + jnp.exp(-x))


def _silu(x):
    return x * _sigmoid(x)


def _gelu_tanh(x):
    c = math.sqrt(2.0 / math.pi)
    return 0.5 * x * (1.0 + jnp.tanh(c * (x + 0.044715 * (x * x * x))))


def _layer_norm(x):
    mu = jnp.mean(x, axis=-1, keepdims=True)
    xc = x - mu
    var = jnp.mean(xc * xc, axis=-1, keepdims=True)
    return xc * lax.rsqrt(var + LN_EPS)


def _mod_kernel(c_ref, w_ref, b_ref, o_ref):
    s = _silu(c_ref[...])
    w = w_ref[0]
    w_hi, w_lo = _split_bf16(w)
    o_ref[0] = _dot3(s, w_hi, w_lo) + b_ref[0]


def _modulation(cond, w_mod, b_mod):
    tn = 1536
    n = w_mod.shape[-1]
    return pl.pallas_call(
        _mod_kernel,
        grid=(DEPTH, n // tn),
        in_specs=[
            pl.BlockSpec((SUBLANES, D_MODEL), lambda l, j: (0, 0)),
            pl.BlockSpec((1, D_MODEL, tn), lambda l, j: (l, 0, j)),
            pl.BlockSpec((1, 1, tn), lambda l, j: (l, 0, j)),
        ],
        out_specs=pl.BlockSpec((1, SUBLANES, tn), lambda l, j: (l, 0, j)),
        out_shape=jax.ShapeDtypeStruct((DEPTH, SUBLANES, n), F32),
        compiler_params=_cparams("arbitrary", "arbitrary"),
        name="modulation",
    )(cond, w_mod, b_mod.reshape(DEPTH, 1, n))


def _rope_chunk(x, cos, sin_signed):
    lane = lax.broadcasted_iota(jnp.int32, x.shape, 1)
    first = (lane % 32) < 16
    partner = jnp.where(first, pltpu.roll(x, LANES - 16, 1), pltpu.roll(x, 16, 1))
    return x * cos + partner * sin_signed


def _inproj_kernel(x_ref, shift_ref, scale_ref, w_ref, b_ref, cos_ref, sin_ref,
                   q_ref, k_ref, v_ref, u_ref, cg_ref, *, rope):
    x = x_ref[0]
    h = _layer_norm(x) * (1.0 + scale_ref[0]) + shift_ref[0]
    p = _dot(h.astype(BF16), w_ref[...]) + b_ref[...]
    scale = HEAD_DIM ** -0.5
    if rope:
        cos = cos_ref[...]
        sin = sin_ref[...]
    for j in range(ATTN_WIDTH // LANES):
        qc = p[:, j * LANES:(j + 1) * LANES]
        if rope:
            qc = _rope_chunk(qc, cos, sin)
        q_ref[0, :, j * LANES:(j + 1) * LANES] = (qc * scale).astype(BF16)
    kc = p[:, Q_END:K_END]
    if rope:
        kc = _rope_chunk(kc, cos, sin)
    k_ref[0] = kc.astype(BF16)
    v_ref[0] = p[:, K_END:V_END].astype(BF16)
    u_ref[0] = p[:, V_END:U_END]
    a = p[:, U_END:U_END + CONV_WIDTH]
    g = p[:, U_END + CONV_WIDTH:]
    cg_ref[0] = a * _sigmoid(g)


def _in_projection(x, shift, scale, w_in_bf, b_in, cos_t, sin_t, *, rope, tm):
    bsz, seq, _ = x.shape
    tok = lambda w: pl.BlockSpec((1, tm, w), lambda b, i: (b, i, 0))
    vec = pl.BlockSpec((1, 1, D_MODEL), lambda b, i: (b, 0, 0))
    tab = pl.BlockSpec((tm, LANES), lambda b, i: (i, 0))
    return pl.pallas_call(
        functools.partial(_inproj_kernel, rope=rope),
        grid=(bsz, seq // tm),
        in_specs=[
            tok(D_MODEL), vec, vec,
            pl.BlockSpec((D_MODEL, IN_WIDTH), lambda b, i: (0, 0)),
            pl.BlockSpec((1, IN_WIDTH), lambda b, i: (0, 0)),
            tab, tab,
        ],
        out_specs=[tok(ATTN_WIDTH), tok(KV_WIDTH), tok(KV_WIDTH), tok(S5_WIDTH), tok(CONV_WIDTH)],
        out_shape=[
            jax.ShapeDtypeStruct((bsz, seq, ATTN_WIDTH), BF16),
            jax.ShapeDtypeStruct((bsz, seq, KV_WIDTH), BF16),
            jax.ShapeDtypeStruct((bsz, seq, KV_WIDTH), BF16),
            jax.ShapeDtypeStruct((bsz, seq, S5_WIDTH), F32),
            jax.ShapeDtypeStruct((bsz, seq, CONV_WIDTH), F32),
        ],
        compiler_params=_cparams("arbitrary", "arbitrary"),
        name="in_projection",
    )(x, shift, scale, w_in_bf, b_in.reshape(1, IN_WIDTH), cos_t, sin_t)


def _rope_tables(seq):
    lane = np.arange(LANES)
    i = lane % HEAD_DIM
    use_row = i < HEAD_DIM // 2
    f = HEAD_DIM // 4
    inv_freq = jnp.asarray(ROPE_BASE, F32) ** (-jnp.asarray(i % f, F32) / f)
    t = jnp.arange(seq, dtype=jnp.int32)
    pos = jnp.where(use_row[None, :], (t // GRID_W)[:, None], (t % GRID_W)[:, None]).astype(F32)
    ang = pos * inv_freq[None, :]
    sign = jnp.where((lane % (2 * f)) < f, -1.0, 1.0).astype(F32)
    return jnp.cos(ang), jnp.sin(ang) * sign[None, :]


def _attn_kernel(*refs, window):
    if window:
        q_ref, kp_ref, kc_ref, kn_ref, vp_ref, vc_ref, vn_ref, kx_ref, vx_ref, sink_ref, o_ref = refs
    else:
        q_ref, kx_ref, vx_ref, sink_ref, o_ref = refs
    n = pl.program_id(1)
    nb = pl.num_programs(1)
    q = q_ref[0]
    tq = q.shape[0]
    rows = Q_PER_KV * tq
    if window:
        row = lax.broadcasted_iota(jnp.int32, (rows, BLOCK), 0) % tq
        col = lax.broadcasted_iota(jnp.int32, (rows, BLOCK), 1)
        ok_prev = (col >= row) & (n > 0)
        ok_next = (col <= row) & (n < nb - 1)

    def with_ones(v, ks):
        return jnp.concatenate([v[:, ks], jnp.ones((v.shape[0], HEAD_DIM), BF16)], axis=-1)

    outs = []
    for g in range(N_KV_HEADS):
        ks = slice(g * HEAD_DIM, (g + 1) * HEAD_DIM)
        heads = range(g * Q_PER_KV, (g + 1) * Q_PER_KV)
        qs = jnp.concatenate([q[:, h * HEAD_DIM:(h + 1) * HEAD_DIM] for h in heads], axis=0)
        sink = jnp.concatenate([jnp.broadcast_to(sink_ref[h:h + 1, 0:1], (tq, 1)) for h in heads], axis=0)
        lx = kx_ref.shape[1]
        s_x = _dot_nt(qs, kx_ref[0][:, ks])
        m_el = s_x[:, 0:LANES]
        for c in range(1, lx // LANES):
            m_el = jnp.maximum(m_el, s_x[:, c * LANES:(c + 1) * LANES])
        if window:
            s_p = jnp.where(ok_prev, _dot_nt(qs, kp_ref[0][:, ks]), NEG_INF)
            s_c = _dot_nt(qs, kc_ref[0][:, ks])
            s_n = jnp.where(ok_next, _dot_nt(qs, kn_ref[0][:, ks]), NEG_INF)
            m_el = jnp.maximum(jnp.maximum(m_el, s_c), jnp.maximum(s_p, s_n))
        m = jnp.maximum(jnp.max(m_el, axis=-1, keepdims=True), sink)
        acc = _dot(jnp.exp(s_x - m).astype(BF16), with_ones(vx_ref[0], ks))
        if window:
            for s_w, v_ref in ((s_p, vp_ref), (s_c, vc_ref), (s_n, vn_ref)):
                acc = acc + _dot(jnp.exp(s_w - m).astype(BF16), with_ones(v_ref[0], ks))
        o = acc[:, 0:HEAD_DIM] / (acc[:, HEAD_DIM:] + jnp.exp(sink - m))
        outs += [o[i * tq:(i + 1) * tq] for i in range(Q_PER_KV)]
    o_ref[0] = jnp.concatenate(outs, axis=-1).astype(BF16)


def _attention(q, k, v, k_ctx, v_ctx, sink_rep, *, window):
    bsz, seq, _ = q.shape
    lc = k_ctx.shape[1]
    ctx_spec = pl.BlockSpec((1, lc, KV_WIDTH), lambda b, i: (b, 0, 0))
    sink_spec = pl.BlockSpec((N_Q_HEADS, LANES), lambda b, i: (0, 0))
    if window:
        tq = BLOCK
        nb = seq // tq
        prev = pl.BlockSpec((1, tq, KV_WIDTH), lambda b, i: (b, jnp.maximum(i - 1, 0), 0))
        cur = pl.BlockSpec((1, tq, KV_WIDTH), lambda b, i: (b, i, 0))
        nxt = pl.BlockSpec((1, tq, KV_WIDTH), lambda b, i: (b, jnp.minimum(i + 1, nb - 1), 0))
        in_specs = [pl.BlockSpec((1, tq, ATTN_WIDTH), lambda b, i: (b, i, 0)),
                    prev, cur, nxt, prev, cur, nxt, ctx_spec, ctx_spec, sink_spec]
        args = (q, k, k, k, v, v, v, k_ctx, v_ctx, sink_rep)
    else:
        tq = seq
        nb = 1
        in_specs = [pl.BlockSpec((1, tq, ATTN_WIDTH), lambda b, i: (b, i, 0)), ctx_spec, ctx_spec, sink_spec]
        args = (q, k_ctx, v_ctx, sink_rep)
    return pl.pallas_call(
        functools.partial(_attn_kernel, window=window),
        grid=(bsz, nb),
        in_specs=in_specs,
        out_specs=pl.BlockSpec((1, tq, ATTN_WIDTH), lambda b, i: (b, i, 0)),
        out_shape=jax.ShapeDtypeStruct((bsz, seq, ATTN_WIDTH), BF16),
        compiler_params=_cparams("arbitrary", "arbitrary"),
        name="window_attention" if window else "context_attention",
    )(*args)


def _s5_kernel(uf_ref, ub_ref, h0_ref, rep_ref, fold_ref, mask_ref, bst_ref, a_ref, cst_ref,
               yf_ref, yb_ref, hfin_ref, bu_scr, h_scr, *, bsz, tc):
    i = pl.program_id(0)

    @pl.when(i == 0)
    def _():
        h_scr[...] = h0_ref[...]

    mask = mask_ref[...]
    rep = rep_ref[...]
    rows = SUBLANES * tc
    for d, u_ref in enumerate((uf_ref, ub_ref)):
        for b in range(bsz):
            u_rep = _dot(rep, u_ref[b].astype(BF16))
            lhs = (u_rep.reshape(tc, SUBLANES, S5_WIDTH) * mask[None]).reshape(rows, S5_WIDTH)
            bu_scr[d * bsz + b] = _dot(lhs.astype(BF16), bst_ref[d])

    n_chain = 2 * bsz
    a_re = [a_ref[d, 0] for d in range(2)]
    a_im = [a_ref[d, 1] for d in range(2)]

    def step(t, carry):
        new = []
        for c in range(n_chain):
            d = c // bsz
            tt = t if d == 0 else tc - 1 - t
            r0 = pl.multiple_of(tt * SUBLANES, SUBLANES)
            h_re, h_im = carry[2 * c], carry[2 * c + 1]
            bu_re = bu_scr[c, pl.ds(r0, SUBLANES), 0:LANES]
            bu_im = bu_scr[c, pl.ds(r0, SUBLANES), LANES:2 * LANES]
            n_re = a_re[d] * h_re - a_im[d] * h_im + bu_re
            n_im = a_re[d] * h_im + a_im[d] * h_re + bu_im
            bu_scr[c, pl.ds(r0, SUBLANES), 0:LANES] = n_re
            bu_scr[c, pl.ds(r0, SUBLANES), LANES:2 * LANES] = n_im
            new += [n_re, n_im]
        return tuple(new)

    init = []
    for c in range(n_chain):
        init += [h_scr[c, :, 0:LANES], h_scr[c, :, LANES:2 * LANES]]
    fin = lax.fori_loop(0, tc, step, tuple(init), unroll=4)
    for c in range(n_chain):
        h_scr[c, :, 0:LANES] = fin[2 * c]
        h_scr[c, :, LANES:2 * LANES] = fin[2 * c + 1]
    hfin_ref[...] = h_scr[...]

    fold = fold_ref[...]
    for d, y_ref in enumerate((yf_ref, yb_ref)):
        for b in range(bsz):
            y_rows = _dot(bu_scr[d * bsz + b].astype(BF16), cst_ref[d])
            x_sel = (y_rows.reshape(tc, SUBLANES, S5_WIDTH) * mask[None]).reshape(rows, S5_WIDTH)
            y_ref[b] = _dot(fold, x_sel.astype(BF16))


def _s5_scan(u, h0, consts, bst, a_tiles, cst, *, tc):
    bsz, seq, _ = u.shape
    nch = seq // tc
    rep, fold, mask = consts
    full = lambda shape: pl.BlockSpec(shape, lambda i: (0,) * len(shape))
    fwd = pl.BlockSpec((bsz, tc, S5_WIDTH), lambda i: (0, i, 0))
    bwd = pl.BlockSpec((bsz, tc, S5_WIDTH), lambda i: (0, nch - 1 - i, 0))
    state = (2 * bsz, SUBLANES, 2 * LANES)
    return pl.pallas_call(
        functools.partial(_s5_kernel, bsz=bsz, tc=tc),
        grid=(nch,),
        in_specs=[fwd, bwd, full(state), full(rep.shape), full(fold.shape), full(mask.shape),
                  full(bst.shape), full(a_tiles.shape), full(cst.shape)],
        out_specs=[fwd, bwd, full(state)],
        out_shape=[jax.ShapeDtypeStruct(u.shape, F32), jax.ShapeDtypeStruct(u.shape, F32),
                   jax.ShapeDtypeStruct(state, F32)],
        scratch_shapes=[pltpu.VMEM((2 * bsz, SUBLANES * tc, 2 * LANES), F32), pltpu.VMEM(state, F32)],
        compiler_params=_cparams("arbitrary"),
        name="s5_scan",
    )(u, u, h0, rep, fold, mask, bst, a_tiles, cst)


def _s5_constants(tc):
    rows = SUBLANES * tc
    r = np.arange(rows)
    rep = (r[:, None] // SUBLANES == np.arange(tc)[None, :]).astype(np.float32)
    mask = (np.arange(S5_WIDTH)[None, :] // CH_PER_CHUNK == np.arange(SUBLANES)[:, None]).astype(np.float32)
    return jnp.asarray(rep, BF16), jnp.asarray(rep.T, BF16), jnp.asarray(mask, F32)


def _s5_params(lam_re, lam_im, log_dt, b_re, b_im, c_re, c_im):
    dt = jnp.exp(log_dt)[..., None]
    mag = jnp.exp(lam_re * dt)
    l_re = mag * jnp.cos(lam_im * dt)
    l_im = mag * jnp.sin(lam_im * dt)
    den = lam_re * lam_re + lam_im * lam_im
    f_re = ((l_re - 1.0) * lam_re + l_im * lam_im) / den
    f_im = (l_im * lam_re - (l_re - 1.0) * lam_im) / den
    bb_re = f_re[..., None] * b_re - f_im[..., None] * b_im
    bb_im = f_re[..., None] * b_im + f_im[..., None] * b_re
    a_tiles = jnp.stack([l_re.reshape(2, SUBLANES, LANES), l_im.reshape(2, SUBLANES, LANES)], axis=1)

    half = S5_GROUPS // SUBLANES
    eye = jnp.eye(half, dtype=F32)

    def in_mat(bb):
        t = bb.reshape(2, SUBLANES, half, S5_STATE, S5_GROUP)
        m = jnp.einsum('dsgpc,gh->dsgchp', t, eye)
        return m.reshape(2, S5_WIDTH, half * S5_STATE)

    def out_mat(cc):
        t = cc.reshape(2, SUBLANES, half, S5_GROUP, S5_STATE)
        m = jnp.einsum('dsgcp,gh->dhpsgc', t, eye)
        return m.reshape(2, half * S5_STATE, S5_WIDTH)

    bst = jnp.concatenate([in_mat(bb_re), in_mat(bb_im)], axis=-1).astype(BF16)
    cst = jnp.concatenate([out_mat(c_re), out_mat(-c_im)], axis=1).astype(BF16)
    return bst, a_tiles, cst


def _conv_kernel(prev_ref, cur_ref, next_ref, wdw_ref, bdw_ref, g_ref, b_ref, wpw_ref, bpw_ref,
                 o_ref, win_ref, *, tm):
    i = pl.program_id(1)
    nt = pl.num_programs(1)
    zero = jnp.zeros((CONV_HALO, CONV_WIDTH), F32)
    win_ref[0:CONV_HALO] = jnp.where(i > 0, prev_ref[0], zero)
    win_ref[CONV_HALO:CONV_HALO + tm] = cur_ref[0]
    win_ref[CONV_HALO + tm:] = jnp.where(i < nt - 1, next_ref[0], zero)
    acc = jnp.zeros((tm, CONV_WIDTH), F32) + bdw_ref[...]
    off = CONV_HALO - CONV_K // 2
    for k in range(CONV_K):
        acc = acc + win_ref[off + k:off + k + tm, :] * wdw_ref[k:k + 1, :]
    h = _silu(_layer_norm(acc) * g_ref[...] + b_ref[...])
    o_ref[0] = (_dot(h.astype(BF16), wpw_ref[...]) + bpw_ref[...]).astype(BF16)


def _conformer_conv(cg, w_dw, b_dw, ln_g, ln_b, w_pw_bf, b_pw, *, tm):
    bsz, seq, _ = cg.shape
    hb = tm // CONV_HALO
    last = seq // CONV_HALO - 1
    row = lambda a: a.reshape(1, CONV_WIDTH)
    vec = pl.BlockSpec((1, CONV_WIDTH), lambda b, i: (0, 0))
    return pl.pallas_call(
        functools.partial(_conv_kernel, tm=tm),
        grid=(bsz, seq // tm),
        in_specs=[
            pl.BlockSpec((1, CONV_HALO, CONV_WIDTH), lambda b, i: (b, jnp.maximum(i * hb - 1, 0), 0)),
            pl.BlockSpec((1, tm, CONV_WIDTH), lambda b, i: (b, i, 0)),
            pl.BlockSpec((1, CONV_HALO, CONV_WIDTH), lambda b, i: (b, jnp.minimum((i + 1) * hb, last), 0)),
            pl.BlockSpec((CONV_K + 1, CONV_WIDTH), lambda b, i: (0, 0)),
            vec, vec, vec,
            pl.BlockSpec((CONV_WIDTH, CONV_WIDTH), lambda b, i: (0, 0)),
            vec,
        ],
        out_specs=pl.BlockSpec((1, tm, CONV_WIDTH), lambda b, i: (b, i, 0)),
        out_shape=jax.ShapeDtypeStruct((bsz, seq, CONV_WIDTH), BF16),
        scratch_shapes=[pltpu.VMEM((tm + 2 * CONV_HALO, CONV_WIDTH), F32)],
        compiler_params=_cparams("arbitrary", "arbitrary"),
        name="conformer_conv",
    )(cg, cg, cg, jnp.pad(w_dw.reshape(CONV_K, CONV_WIDTH), ((0, 1), (0, 0))), row(b_dw), row(ln_g),
      row(ln_b), w_pw_bf, row(b_pw))


def _mixout_kernel(attn_ref, yf_ref, yb_ref, u_ref, conv_ref, x_ref,
                   dskip_ref, wglu_ref, bglu_ref, wout_ref, bout_ref,
                   gate_ref, g1_ref, b1_ref, shift_ref, scale_ref, wr_hi_ref, wr_lo_ref,
                   xmid_ref, h2_ref, logit_ref):
    y = dskip_ref[...] * u_ref[0] + yf_ref[0] + yb_ref[0]
    z = _gelu_tanh(y)
    s5 = z * _sigmoid(_dot(z.astype(BF16), wglu_ref[...]) + bglu_ref[...])
    y_mix = (_dot(attn_ref[0], wout_ref[0:ATTN_WIDTH, :])
             + _dot(s5.astype(BF16), wout_ref[ATTN_WIDTH:ATTN_WIDTH + S5_WIDTH, :])
             + _dot(conv_ref[0], wout_ref[ATTN_WIDTH + S5_WIDTH:, :])
             + bout_ref[...])
    r = DEEPNORM_ALPHA * x_ref[0] + gate_ref[0] * y_mix
    x_mid = _layer_norm(r) * g1_ref[...] + b1_ref[...]
    xmid_ref[0] = x_mid
    h2 = _layer_norm(x_mid) * (1.0 + scale_ref[0]) + shift_ref[0]
    h2_ref[0] = h2.astype(BF16)
    logit_ref[0] = _dot3(h2, wr_hi_ref[...], wr_lo_ref[...])


def _mixer_output(attn, yf, yb, u, conv, x, d_skip, w_glu_bf, b_glu, w_out_bf, b_out,
                  gate1, ln_g, ln_b, shift2, scale2, wr_hi, wr_lo, *, tm):
    bsz, seq, _ = x.shape
    tok = lambda w: pl.BlockSpec((1, tm, w), lambda b, i: (b, i, 0))
    full = lambda r, c: pl.BlockSpec((r, c), lambda b, i: (0, 0))
    bvec = pl.BlockSpec((1, 1, D_MODEL), lambda b, i: (b, 0, 0))
    row = lambda a: a.reshape(1, -1)
    return pl.pallas_call(
        _mixout_kernel,
        grid=(bsz, seq // tm),
        in_specs=[
            tok(ATTN_WIDTH), tok(S5_WIDTH), tok(S5_WIDTH), tok(S5_WIDTH), tok(CONV_WIDTH), tok(D_MODEL),
            full(1, S5_WIDTH), full(S5_WIDTH, S5_WIDTH), full(1, S5_WIDTH),
            full(D_MODEL, D_MODEL), full(1, D_MODEL),
            bvec, full(1, D_MODEL), full(1, D_MODEL), bvec, bvec,
            full(D_MODEL, LANES), full(D_MODEL, LANES),
        ],
        out_specs=[tok(D_MODEL), tok(D_MODEL), tok(LANES)],
        out_shape=[
            jax.ShapeDtypeStruct((bsz, seq, D_MODEL), F32),
            jax.ShapeDtypeStruct((bsz, seq, D_MODEL), BF16),
            jax.ShapeDtypeStruct((bsz, seq, LANES), F32),
        ],
        compiler_params=_cparams("arbitrary", "arbitrary"),
        name="mixer_output",
    )(attn, yf, yb, u, conv, x, row(d_skip), w_glu_bf, row(b_glu), w_out_bf, row(b_out),
      gate1, row(ln_g), row(ln_b), shift2, scale2, wr_hi, wr_lo)


def _token_cumsum(m, tri, ones):
    nblk = m.shape[0]
    m2 = m.reshape(nblk * N_EXPERTS, LANES).astype(BF16)
    within = _dot(m2, tri).reshape(nblk, N_EXPERTS, LANES)
    tot = _dot(m2, ones).reshape(nblk, N_EXPERTS, LANES)
    offs = []
    run = jnp.zeros((N_EXPERTS, LANES), F32)
    for j in range(nblk):
        offs.append(run)
        run = run + tot[j]
    off = jnp.stack(offs, axis=0)
    return within + off, off


def _router_kernel(logit_ref, tri_ref, ones_ref, aff_ref, sel_ref, pos_ref, off_ref, *, cap, slot0):
    b = pl.program_id(0)
    nblk = aff_ref.shape[1]

    def soft(j, carry):
        r0 = pl.multiple_of(j * LANES, LANES)
        t = logit_ref[0, pl.ds(r0, LANES), :].T[0:N_EXPERTS]
        ex = jnp.exp(t - jnp.max(t, axis=0, keepdims=True))
        aff_ref[0, j] = ex / jnp.sum(ex, axis=0, keepdims=True)
        return carry

    lax.fori_loop(0, nblk, soft, 0)
    aff = aff_ref[0]

    def enough(cand):
        cnt = jnp.sum(jnp.where(aff >= cand[None], 1.0, 0.0), axis=0)
        return jnp.sum(cnt, axis=-1, keepdims=True) >= cap

    p = jnp.full((N_EXPERTS, LANES), 2.0, F32)
    for k in range(6, -1, -1):
        cand = p * (2.0 ** -(2 ** k))
        p = jnp.where(enough(cand), p, cand)
    thr = 0.5 * p
    thr = jnp.where(enough(thr), thr, 0.0)

    def refine(_, carry):
        lo, step = carry
        cand = lo + step
        return jnp.where(enough(cand), cand, lo), 0.5 * step

    thr, _ = lax.fori_loop(0, 23, refine, (thr, 0.5 * thr))
    gt = aff > thr[None]
    eq = aff == thr[None]
    n_gt = jnp.sum(jnp.sum(jnp.where(gt, 1.0, 0.0), axis=0), axis=-1, keepdims=True)
    need = cap - n_gt
    tri = tri_ref[...]
    ones = ones_ref[...]
    cum_eq, _ = _token_cumsum(jnp.where(eq, 1.0, 0.0), tri, ones)
    sel = jnp.where(gt | (eq & (cum_eq <= need[None])), 1.0, 0.0)
    cum_sel, off = _token_cumsum(sel, tri, ones)
    base = (slot0 + b * cap).astype(F32)
    sel_ref[0] = sel
    pos_ref[0] = cum_sel - sel + base
    off_ref[0] = off + base


def _router(logits, *, slot0):
    bsz, n, _ = logits.shape
    nblk = n // LANES
    cap = CAPACITY_FACTOR * n // N_EXPERTS
    idx = np.arange(LANES)
    tri = jnp.asarray((idx[:, None] <= idx[None, :]).astype(np.float32), BF16)
    ones = jnp.ones((LANES, LANES), BF16)
    shape = (bsz, nblk, N_EXPERTS, LANES)
    out = pl.BlockSpec((1, nblk, N_EXPERTS, LANES), lambda b: (b, 0, 0, 0))
    sq = pl.BlockSpec((LANES, LANES), lambda b: (0, 0))
    return pl.pallas_call(
        functools.partial(_router_kernel, cap=cap, slot0=slot0),
        grid=(bsz,),
        in_specs=[pl.BlockSpec((1, n, LANES), lambda b: (b, 0, 0)), sq, sq],
        out_specs=[out] * 4,
        out_shape=[jax.ShapeDtypeStruct(shape, F32)] * 4,
        compiler_params=_cparams("arbitrary"),
        name="router",
    )(logits, tri, ones)


DISPATCH_KW = LANES + SUBLANES
COMBINE_KW = LANES + 16
X_STEP_TOKENS = 2048
FF_TILE = 256


def _dispatch_block(tb_ref, tile, e, h2_blk, pos_row, sel_row, aff_row, xs_scr, g_scr):
    base = tb_ref[tile * N_EXPERTS + e]
    base8 = pl.multiple_of(lax.shift_left(lax.shift_right_logical(base, 3), 3), SUBLANES)
    k = lax.broadcasted_iota(jnp.int32, (DISPATCH_KW, LANES), 0).astype(F32)
    hit = (k == (pos_row - base8.astype(F32))) & (sel_row > 0.5)
    sel_t = jnp.where(hit, 1.0, 0.0).astype(BF16)
    xs_scr[pl.ds(base8, DISPATCH_KW), :] += _dot(sel_t, h2_blk)
    g = jnp.sum(jnp.where(hit, aff_row, 0.0), axis=-1, keepdims=True)
    g_scr[pl.ds(base8, DISPATCH_KW), :] += jnp.broadcast_to(g, (DISPATCH_KW, LANES))


def _moe_kernel(*refs, n_x_steps, n_ctx, rows):
    tb_ref = refs[0]
    if n_ctx:
        (posx_ref, selx_ref, affx_ref, h2x_ref, posc_ref, selc_ref, affc_ref, h2c_ref,
         wg_ref, wu_ref, wd_ref, y_ref, acc_scr, g_scr, xs_scr) = refs[1:]
    else:
        (posx_ref, selx_ref, affx_ref, h2x_ref, wg_ref, wu_ref, wd_ref, y_ref,
         acc_scr, g_scr, xs_scr) = refs[1:]
    e = pl.program_id(0)
    s = pl.program_id(1)
    n_disp = n_x_steps + (1 if n_ctx else 0)
    blk_per_step = posx_ref.shape[2]

    @pl.when(s == 0)
    def _():
        acc_scr[...] = jnp.zeros_like(acc_scr)
        g_scr[...] = jnp.zeros_like(g_scr)

    @pl.when(s < n_x_steps)
    def _():
        for i in range(blk_per_step):
            _dispatch_block(tb_ref, s * blk_per_step + i, e, h2x_ref[0, i * LANES:(i + 1) * LANES, :],
                            posx_ref[0, 0, i:i + 1, :], selx_ref[0, 0, i:i + 1, :],
                            affx_ref[0, 0, i:i + 1, :], acc_scr, g_scr)

    if n_ctx:
        @pl.when(s == n_x_steps)
        def _():
            bsz_c, blk_c = posc_ref.shape[0], posc_ref.shape[2]
            for b in range(bsz_c):
                for i in range(blk_c):
                    _dispatch_block(tb_ref, n_x_steps * blk_per_step + b * blk_c + i, e,
                                    h2c_ref[b, i * LANES:(i + 1) * LANES, :],
                                    posc_ref[b, 0, i:i + 1, :], selc_ref[b, 0, i:i + 1, :],
                                    affc_ref[b, 0, i:i + 1, :], acc_scr, g_scr)

    @pl.when(s == n_disp)
    def _():
        xs_scr[...] = acc_scr[0:rows, :].astype(BF16)

    @pl.when(s >= n_disp)
    def _():
        wg = wg_ref[0].astype(BF16)
        wu = wu_ref[0].astype(BF16)
        wd = wd_ref[0].astype(BF16)
        first = jnp.where(s == n_disp, 0.0, 1.0)
        half = rows // 2
        for r0 in (0, half):
            xs = xs_scr[r0:r0 + half, :]
            hid = _silu(_dot(xs, wg)) * _dot(xs, wu)
            acc_scr[r0:r0 + half, :] = acc_scr[r0:r0 + half, :] * first + _dot(hid.astype(BF16), wd)

    @pl.when(s == pl.num_programs(1) - 1)
    def _():
        g = g_scr[0:rows, :]
        y_ref[0, 0:rows, :] = (acc_scr[0:rows, :] * jnp.concatenate([g] * (D_MODEL // LANES), axis=-1)).astype(BF16)
        y_ref[0, rows:, :] = jnp.zeros((y_ref.shape[1] - rows, D_MODEL), BF16)


def _moe_experts(tb, route_x, h2x, route_c, h2c, w_gate, w_up, w_down):
    bsz, n, d = h2x.shape
    x_step = min(X_STEP_TOKENS, n)
    n_x_steps = bsz * n // x_step
    steps_per_b = n // x_step
    blk_per_step = x_step // LANES
    n_ctx = 0 if h2c is None else h2c.shape[0] * h2c.shape[1]
    rows = bsz * CAPACITY_FACTOR * n // N_EXPERTS + CAPACITY_FACTOR * n_ctx // N_EXPERTS
    rows_pad = rows + COMBINE_KW
    n_disp = n_x_steps + (1 if n_ctx else 0)
    n_ff = EXPERT_FF // FF_TILE

    def xs_idx(s):
        sc = jnp.minimum(s, n_x_steps - 1)
        return sc // steps_per_b, sc % steps_per_b

    def route_x_spec():
        return pl.BlockSpec((1, 1, blk_per_step, LANES),
                            lambda e, s, tb: (xs_idx(s)[0], e, xs_idx(s)[1], 0))

    in_specs = [route_x_spec(), route_x_spec(), route_x_spec(),
                pl.BlockSpec((1, x_step, d), lambda e, s, tb: (xs_idx(s)[0], xs_idx(s)[1], 0))]
    args = list(route_x) + [h2x]
    if n_ctx:
        bc, nc, _ = h2c.shape
        rc = pl.BlockSpec((bc, 1, nc // LANES, LANES), lambda e, s, tb: (0, e, 0, 0))
        in_specs += [rc, rc, rc, pl.BlockSpec((bc, nc, d), lambda e, s, tb: (0, 0, 0))]
        args += list(route_c) + [h2c]
    ff = lambda s: jnp.maximum(s - n_disp, 0)
    in_specs += [pl.BlockSpec((1, d, FF_TILE), lambda e, s, tb: (e, 0, ff(s))),
                 pl.BlockSpec((1, d, FF_TILE), lambda e, s, tb: (e, 0, ff(s))),
                 pl.BlockSpec((1, FF_TILE, d), lambda e, s, tb: (e, ff(s), 0))]
    args += [w_gate, w_up, w_down]
    scr_rows = rows + DISPATCH_KW
    return pl.pallas_call(
        functools.partial(_moe_kernel, n_x_steps=n_x_steps, n_ctx=n_ctx, rows=rows),
        grid_spec=pltpu.PrefetchScalarGridSpec(
            num_scalar_prefetch=1,
            grid=(N_EXPERTS, n_disp + n_ff),
            in_specs=in_specs,
            out_specs=pl.BlockSpec((1, rows_pad, d), lambda e, s, tb: (e, 0, 0)),
            scratch_shapes=[pltpu.VMEM((scr_rows, d), F32), pltpu.VMEM((scr_rows, LANES), F32),
                            pltpu.VMEM((rows, d), BF16)],
        ),
        out_shape=jax.ShapeDtypeStruct((N_EXPERTS, rows_pad, d), BF16),
        compiler_params=_cparams("arbitrary", "arbitrary"),
        name="moe_experts",
    )(tb, *args)


def _combine_kernel(tb_ref, pos_ref, sel_ref, xmid_ref, gate_ref, g_ref, b_ref, y_hbm, o_ref,
                    y_scr, sem, *, cap, slot0, blk_per_step, nblk):
    b = pl.program_id(0)
    j = pl.program_id(1)
    win = y_scr.shape[1]

    @pl.when(j == 0)
    def _():
        row0 = pl.multiple_of(slot0 + b * cap, 16)
        cp = pltpu.make_async_copy(y_hbm.at[:, pl.ds(row0, win), :], y_scr, sem)
        cp.start()
        cp.wait()

    set_base = (slot0 + b * cap).astype(F32)
    k = lax.broadcasted_iota(jnp.int32, (COMBINE_KW, LANES), 0).astype(F32)
    for i in range(blk_per_step):
        blk = j * blk_per_step + i
        acc = jnp.zeros((LANES, D_MODEL), F32)
        for e in range(N_EXPERTS):
            base = tb_ref[(b * nblk + blk) * N_EXPERTS + e] - (slot0 + b * cap)
            base16 = pl.multiple_of(lax.shift_left(lax.shift_right_logical(base, 4), 4), 16)
            rel = pos_ref[0, i, e:e + 1, :] - (set_base + base16.astype(F32))
            hit = (k == rel) & (sel_ref[0, i, e:e + 1, :] > 0.5)
            sel_t = jnp.where(hit, 1.0, 0.0).astype(BF16)
            slab = y_scr[e, pl.ds(base16, COMBINE_KW), :]
            acc = acc + lax.dot_general(sel_t, slab, (((0,), (0,)), ((), ())), preferred_element_type=F32)
        rows = slice(i * LANES, (i + 1) * LANES)
        r = DEEPNORM_ALPHA * xmid_ref[0, rows, :] + gate_ref[0] * acc
        o_ref[0, rows, :] = _layer_norm(r) * g_ref[...] + b_ref[...]


def _combine_post(tb, pos, sel, x_mid, gate2, ln_g, ln_b, y, *, slot0, tm):
    bsz, n, d = x_mid.shape
    nblk = n // LANES
    cap = CAPACITY_FACTOR * n // N_EXPERTS
    blk_per_step = tm // LANES
    tok = pl.BlockSpec((1, tm, d), lambda b, j, tb: (b, j, 0))
    route = pl.BlockSpec((1, blk_per_step, N_EXPERTS, LANES), lambda b, j, tb: (b, j, 0, 0))
    vec = pl.BlockSpec((1, d), lambda b, j, tb: (0, 0))
    return pl.pallas_call(
        functools.partial(_combine_kernel, cap=cap, slot0=slot0, blk_per_step=blk_per_step, nblk=nblk),
        grid_spec=pltpu.PrefetchScalarGridSpec(
            num_scalar_prefetch=1,
            grid=(bsz, n // tm),
            in_specs=[route, route, tok, pl.BlockSpec((1, 1, d), lambda b, j, tb: (b, 0, 0)), vec, vec,
                      pl.BlockSpec(memory_space=pl.ANY)],
            out_specs=tok,
            scratch_shapes=[pltpu.VMEM((N_EXPERTS, cap + COMBINE_KW, d), BF16), pltpu.SemaphoreType.DMA(())],
        ),
        out_shape=jax.ShapeDtypeStruct(x_mid.shape, F32),
        compiler_params=_cparams("arbitrary", "arbitrary"),
        name="combine_post",
    )(tb, pos, sel, x_mid, gate2, ln_g.reshape(1, d), ln_b.reshape(1, d), y)


def _tile_table(off):
    return off[..., 0].astype(jnp.int32).reshape(-1)


def _expert_major(a):
    return jnp.swapaxes(a, 1, 2)


def kernel(x, c, ctx, c_ctx, w_mod, b_mod, w_in, b_in, attn_sink, s5_lam_re, s5_lam_im, s5_log_dt,
           s5_b_re, s5_b_im, s5_c_re, s5_c_im, s5_d, s5_w_glu, s5_b_glu, conv_w_dw, conv_b_dw,
           conv_ln_g, conv_ln_b, conv_w_pw, conv_b_pw, w_out, b_out, ln1_g, ln1_b, w_router,
           exp_w_gate, exp_w_up, exp_w_down, ln2_g, ln2_b):
    bsz, seq, d = x.shape
    lc = ctx.shape[1]
    tm_x, tm_c = 512, lc

    cond = jnp.zeros((SUBLANES, d), F32).at[:bsz].set(c).at[bsz].set(c_ctx)
    mods = _modulation(cond, w_mod, b_mod)
    cos_t, sin_t = _rope_tables(seq)
    s5_consts_x = _s5_constants(256)
    s5_consts_c = _s5_constants(lc)
    zero_state = jnp.zeros((2 * bsz, SUBLANES, 2 * LANES), F32)

    xc = ctx
    for l in range(DEPTH):
        last = l == DEPTH - 1
        mod_x = [mods[l, :bsz, k * d:(k + 1) * d].reshape(bsz, 1, d) for k in range(6)]
        mod_c = [jnp.broadcast_to(mods[l, bsz, k * d:(k + 1) * d].reshape(1, 1, d), (bsz, 1, d))
                 for k in range(6)]
        w_in_bf = w_in[l].astype(BF16)
        w_out_bf = w_out[l].astype(BF16)
        w_glu_bf = s5_w_glu[l].astype(BF16)
        w_pw_bf = conv_w_pw[l].astype(BF16)
        wr = jnp.pad(w_router[l], ((0, 0), (0, LANES - N_EXPERTS)))
        wr_hi, wr_lo = _split_bf16(wr)
        sink_rep = jnp.broadcast_to(attn_sink[l][:, None], (N_Q_HEADS, LANES))
        bst, a_tiles, cst = _s5_params(s5_lam_re[l], s5_lam_im[l], s5_log_dt[l], s5_b_re[l], s5_b_im[l],
                                       s5_c_re[l], s5_c_im[l])

        q, k, v, u, cg = _in_projection(x, mod_x[0], mod_x[1], w_in_bf, b_in[l], cos_t, sin_t,
                                        rope=True, tm=tm_x)
        q_c, k_c, v_c, u_c, cg_c = _in_projection(xc, mod_c[0], mod_c[1], w_in_bf, b_in[l],
                                                  cos_t[:lc], sin_t[:lc], rope=False, tm=tm_c)

        attn_x = _attention(q, k, v, k_c, v_c, sink_rep, window=True)
        yf_c, yb_c, h_ctx = _s5_scan(u_c, zero_state, s5_consts_c, bst, a_tiles, cst, tc=lc)
        yf, yb, _ = _s5_scan(u, h_ctx, s5_consts_x, bst, a_tiles, cst, tc=256)
        conv_args = (conv_w_dw[l], conv_b_dw[l], conv_ln_g[l], conv_ln_b[l], w_pw_bf, conv_b_pw[l])
        conv_x = _conformer_conv(cg, *conv_args, tm=256)
        mix_args = (s5_d[l], w_glu_bf, s5_b_glu[l], w_out_bf, b_out[l])
        x_mid, h2, logits = _mixer_output(attn_x, yf, yb, u, conv_x, x, *mix_args, mod_x[2], ln1_g[l],
                                          ln1_b[l], mod_x[3], mod_x[4], wr_hi, wr_lo, tm=tm_x)
        aff_x, sel_x, pos_x, off_x = _router(logits, slot0=0)
        route_x = tuple(_expert_major(a) for a in (pos_x, sel_x, aff_x))
        tb_x = _tile_table(off_x)
        if not last:
            attn_c = _attention(q_c, None, None, k_c, v_c, sink_rep, window=False)
            conv_c = _conformer_conv(cg_c, *conv_args, tm=lc)
            xc_mid, hc2, logits_c = _mixer_output(attn_c, yf_c, yb_c, u_c, conv_c, xc, *mix_args, mod_c[2],
                                                  ln1_g[l], ln1_b[l], mod_c[3], mod_c[4], wr_hi, wr_lo,
                                                  tm=tm_c)
            slot0_c = bsz * CAPACITY_FACTOR * seq // N_EXPERTS
            aff_c, sel_c, pos_c, off_c = _router(logits_c, slot0=slot0_c)
            route_c = tuple(_expert_major(a) for a in (pos_c, sel_c, aff_c))
            tb_c = _tile_table(off_c)
            y = _moe_experts(jnp.concatenate([tb_x, tb_c]), route_x, h2, route_c, hc2,
                             exp_w_gate[l], exp_w_up[l], exp_w_down[l])
            xc = _combine_post(tb_c, pos_c, sel_c, xc_mid, mod_c[5], ln2_g[l], ln2_b[l], y,
                               slot0=slot0_c, tm=lc)
        else:
            y = _moe_experts(tb_x, route_x, h2, None, None, exp_w_gate[l], exp_w_up[l], exp_w_down[l])
        x = _combine_post(tb_x, pos_x, sel_x, x_mid, mod_x[5], ln2_g[l], ln2_b[l], y, slot0=0, tm=256)
    return x
```

```python
import functools
import math

import jax
import jax.numpy as jnp
import numpy as np
from jax import lax
from jax.experimental import pallas as pl
from jax.experimental.pallas import tpu as pltpu

D_MODEL = 1024
DEPTH = 2
GRID_W = 64
HEAD_DIM = 64
ATTN_WIDTH = D_MODEL // 2
N_Q_HEADS = ATTN_WIDTH // HEAD_DIM
N_KV_HEADS = N_Q_HEADS // 4
Q_PER_KV = N_Q_HEADS // N_KV_HEADS
KV_WIDTH = N_KV_HEADS * HEAD_DIM
BLOCK = 128
ROPE_BASE = 10000.0
NEG_INF = -1e30
S5_WIDTH = D_MODEL // 4
S5_GROUP = 16
S5_GROUPS = S5_WIDTH // S5_GROUP
S5_STATE = 64
CONV_WIDTH = D_MODEL - ATTN_WIDTH - S5_WIDTH
CONV_K = 31
CONV_HALO = 16
Q_END = ATTN_WIDTH
K_END = Q_END + KV_WIDTH
V_END = K_END + KV_WIDTH
U_END = V_END + S5_WIDTH
IN_WIDTH = U_END + 2 * CONV_WIDTH
N_EXPERTS = 16
EXPERT_FF = 2 * D_MODEL
CAPACITY_FACTOR = 2
DEEPNORM_ALPHA = (2.0 * DEPTH) ** 0.25
LN_EPS = 1e-5

LANES = 128
SUBLANES = 8
S5_CHUNKS = S5_WIDTH * S5_STATE // S5_GROUP // LANES
CH_PER_CHUNK = S5_WIDTH // S5_CHUNKS
VMEM_LIMIT = 56 * 1024 * 1024

F32 = jnp.float32
BF16 = jnp.bfloat16


def _cparams(*sem):
    return pltpu.CompilerParams(dimension_semantics=sem, vmem_limit_bytes=VMEM_LIMIT)


def _dot(a, b):
    return jnp.dot(a, b, preferred_element_type=F32)


def _dot_nt(a, b):
    return lax.dot_general(a, b, (((1,), (1,)), ((), ())), preferred_element_type=F32)


def _split_bf16(x):
    hi = x.astype(BF16)
    lo = (x - hi.astype(F32)).astype(BF16)
    return hi, lo


def _dot3(a, b_hi, b_lo):
    a_hi, a_lo = _split_bf16(a)
    return _dot(a_hi, b_hi) + (_dot(a_lo, b_hi) + _dot(a_hi, b_lo))


def _sigmoid(x):
    return 1.0 / (1.0 + jnp.exp(-x))


def _silu(x):
    return x * _sigmoid(x)


def _gelu_tanh(x):
    c = math.sqrt(2.0 / math.pi)
    return 0.5 * x * (1.0 + jnp.tanh(c * (x + 0.044715 * (x * x * x))))


def _layer_norm(x):
    mu = jnp.mean(x, axis=-1, keepdims=True)
    xc = x - mu
    var = jnp.mean(xc * xc, axis=-1, keepdims=True)
    return xc * lax.rsqrt(var + LN_EPS)


def _mod_kernel(c_ref, w_ref, b_ref, o_ref):
    s = _silu(c_ref[...])
    w = w_ref[0]
    w_hi, w_lo = _split_bf16(w)
    o_ref[0] = _dot3(s, w_hi, w_lo) + b_ref[0]


def _modulation(cond, w_mod, b_mod):
    tn = 1536
    n = w_mod.shape[-1]
    return pl.pallas_call(
        _mod_kernel,
        grid=(DEPTH, n // tn),
        in_specs=[
            pl.BlockSpec((SUBLANES, D_MODEL), lambda l, j: (0, 0)),
            pl.BlockSpec((1, D_MODEL, tn), lambda l, j: (l, 0, j)),
            pl.BlockSpec((1, 1, tn), lambda l, j: (l, 0, j)),
        ],
        out_specs=pl.BlockSpec((1, SUBLANES, tn), lambda l, j: (l, 0, j)),
        out_shape=jax.ShapeDtypeStruct((DEPTH, SUBLANES, n), F32),
        compiler_params=_cparams("arbitrary", "arbitrary"),
        name="modulation",
    )(cond, w_mod, b_mod.reshape(DEPTH, 1, n))


def _rope_chunk(x, cos, sin_signed):
    lane = lax.broadcasted_iota(jnp.int32, x.shape, 1)
    first = (lane % 32) < 16
    partner = jnp.where(first, pltpu.roll(x, LANES - 16, 1), pltpu.roll(x, 16, 1))
    return x * cos + partner * sin_signed


def _inproj_kernel(x_ref, shift_ref, scale_ref, w_ref, b_ref, cos_ref, sin_ref,
                   q_ref, k_ref, v_ref, u_ref, cg_ref, *, rope):
    x = x_ref[0]
    h = _layer_norm(x) * (1.0 + scale_ref[0]) + shift_ref[0]
    p = _dot(h.astype(BF16), w_ref[...]) + b_ref[...]
    scale = HEAD_DIM ** -0.5
    if rope:
        cos = cos_ref[...]
        sin = sin_ref[...]
    for j in range(ATTN_WIDTH // LANES):
        qc = p[:, j * LANES:(j + 1) * LANES]
        if rope:
            qc = _rope_chunk(qc, cos, sin)
        q_ref[0, :, j * LANES:(j + 1) * LANES] = (qc * scale).astype(BF16)
    kc = p[:, Q_END:K_END]
    if rope:
        kc = _rope_chunk(kc, cos, sin)
    k_ref[0] = kc.astype(BF16)
    v_ref[0] = p[:, K_END:V_END].astype(BF16)
    u_ref[0] = p[:, V_END:U_END]
    a = p[:, U_END:U_END + CONV_WIDTH]
    g = p[:, U_END + CONV_WIDTH:]
    cg_ref[0] = a * _sigmoid(g)


def _in_projection(x, shift, scale, w_in_bf, b_in, cos_t, sin_t, *, rope, tm):
    bsz, seq, _ = x.shape
    tok = lambda w: pl.BlockSpec((1, tm, w), lambda b, i: (b, i, 0))
    vec = pl.BlockSpec((1, 1, D_MODEL), lambda b, i: (b, 0, 0))
    tab = pl.BlockSpec((tm, LANES), lambda b, i: (i, 0))
    return pl.pallas_call(
        functools.partial(_inproj_kernel, rope=rope),
        grid=(bsz, seq // tm),
        in_specs=[
            tok(D_MODEL), vec, vec,
            pl.BlockSpec((D_MODEL, IN_WIDTH), lambda b, i: (0, 0)),
            pl.BlockSpec((1, IN_WIDTH), lambda b, i: (0, 0)),
            tab, tab,
        ],
        out_specs=[tok(ATTN_WIDTH), tok(KV_WIDTH), tok(KV_WIDTH), tok(S5_WIDTH), tok(CONV_WIDTH)],
        out_shape=[
            jax.ShapeDtypeStruct((bsz, seq, ATTN_WIDTH), BF16),
            jax.ShapeDtypeStruct((bsz, seq, KV_WIDTH), BF16),
            jax.ShapeDtypeStruct((bsz, seq, KV_WIDTH), BF16),
            jax.ShapeDtypeStruct((bsz, seq, S5_WIDTH), F32),
            jax.ShapeDtypeStruct((bsz, seq, CONV_WIDTH), F32),
        ],
        compiler_params=_cparams("arbitrary", "arbitrary"),
        name="in_projection",
    )(x, shift, scale, w_in_bf, b_in.reshape(1, IN_WIDTH), cos_t, sin_t)


def _rope_tables(seq):
    lane = np.arange(LANES)
    i = lane % HEAD_DIM
    use_row = i < HEAD_DIM // 2
    f = HEAD_DIM // 4
    inv_freq = jnp.asarray(ROPE_BASE, F32) ** (-jnp.asarray(i % f, F32) / f)
    t = jnp.arange(seq, dtype=jnp.int32)
    pos = jnp.where(use_row[None, :], (t // GRID_W)[:, None], (t % GRID_W)[:, None]).astype(F32)
    ang = pos * inv_freq[None, :]
    sign = jnp.where((lane % (2 * f)) < f, -1.0, 1.0).astype(F32)
    return jnp.cos(ang), jnp.sin(ang) * sign[None, :]


def _attn_kernel(*refs, window):
    if window:
        q_ref, kp_ref, kc_ref, kn_ref, vp_ref, vc_ref, vn_ref, kx_ref, vx_ref, sink_ref, o_ref = refs
    else:
        q_ref, kx_ref, vx_ref, sink_ref, o_ref = refs
    n = pl.program_id(1)
    nb = pl.num_programs(1)
    q = q_ref[0]
    tq = q.shape[0]
    rows = Q_PER_KV * tq
    if window:
        row = lax.broadcasted_iota(jnp.int32, (rows, BLOCK), 0) % tq
        col = lax.broadcasted_iota(jnp.int32, (rows, BLOCK), 1)
        ok_prev = (col >= row) & (n > 0)
        ok_next = (col <= row) & (n < nb - 1)

    def with_ones(v, ks):
        return jnp.concatenate([v[:, ks], jnp.ones((v.shape[0], HEAD_DIM), BF16)], axis=-1)

    outs = []
    for g in range(N_KV_HEADS):
        ks = slice(g * HEAD_DIM, (g + 1) * HEAD_DIM)
        heads = range(g * Q_PER_KV, (g + 1) * Q_PER_KV)
        qs = jnp.concatenate([q[:, h * HEAD_DIM:(h + 1) * HEAD_DIM] for h in heads], axis=0)
        sink = jnp.concatenate([jnp.broadcast_to(sink_ref[h:h + 1, 0:1], (tq, 1)) for h in heads], axis=0)
        lx = kx_ref.shape[1]
        s_x = _dot_nt(qs, kx_ref[0][:, ks])
        m_el = s_x[:, 0:LANES]
        for c in range(1, lx // LANES):
            m_el = jnp.maximum(m_el, s_x[:, c * LANES:(c + 1) * LANES])
        if window:
            s_p = jnp.where(ok_prev, _dot_nt(qs, kp_ref[0][:, ks]), NEG_INF)
            s_c = _dot_nt(qs, kc_ref[0][:, ks])
            s_n = jnp.where(ok_next, _dot_nt(qs, kn_ref[0][:, ks]), NEG_INF)
            m_el = jnp.maximum(jnp.maximum(m_el, s_c), jnp.maximum(s_p, s_n))
        m = jnp.maximum(jnp.max(m_el, axis=-1, keepdims=True), sink)
        acc = _dot(jnp.exp(s_x - m).astype(BF16), with_ones(vx_ref[0], ks))
        if window:
            for s_w, v_ref in ((s_p, vp_ref), (s_c, vc_ref), (s_n, vn_ref)):
                acc = acc + _dot(jnp.exp(s_w - m).astype(BF16), with_ones(v_ref[0], ks))
        o = acc[:, 0:HEAD_DIM] / (acc[:, HEAD_DIM:] + jnp.exp(sink - m))
        outs += [o[i * tq:(i + 1) * tq] for i in range(Q_PER_KV)]
    o_ref[0] = jnp.concatenate(outs, axis=-1).astype(BF16)


def _attention(q, k, v, k_ctx, v_ctx, sink_rep, *, window):
    bsz, seq, _ = q.shape
    lc = k_ctx.shape[1]
    ctx_spec = pl.BlockSpec((1, lc, KV_WIDTH), lambda b, i: (b, 0, 0))
    sink_spec = pl.BlockSpec((N_Q_HEADS, LANES), lambda b, i: (0, 0))
    if window:
        tq = BLOCK
        nb = seq // tq
        prev = pl.BlockSpec((1, tq, KV_WIDTH), lambda b, i: (b, jnp.maximum(i - 1, 0), 0))
        cur = pl.BlockSpec((1, tq, KV_WIDTH), lambda b, i: (b, i, 0))
        nxt = pl.BlockSpec((1, tq, KV_WIDTH), lambda b, i: (b, jnp.minimum(i + 1, nb - 1), 0))
        in_specs = [pl.BlockSpec((1, tq, ATTN_WIDTH), lambda b, i: (b, i, 0)),
                    prev, cur, nxt, prev, cur, nxt, ctx_spec, ctx_spec, sink_spec]
        args = (q, k, k, k, v, v, v, k_ctx, v_ctx, sink_rep)
    else:
        tq = seq
        nb = 1
        in_specs = [pl.BlockSpec((1, tq, ATTN_WIDTH), lambda b, i: (b, i, 0)), ctx_spec, ctx_spec, sink_spec]
        args = (q, k_ctx, v_ctx, sink_rep)
    return pl.pallas_call(
        functools.partial(_attn_kernel, window=window),
        grid=(bsz, nb),
        in_specs=in_specs,
        out_specs=pl.BlockSpec((1, tq, ATTN_WIDTH), lambda b, i: (b, i, 0)),
        out_shape=jax.ShapeDtypeStruct((bsz, seq, ATTN_WIDTH), BF16),
        compiler_params=_cparams("arbitrary", "arbitrary"),
        name="window_attention" if window else "context_attention",
    )(*args)


def _s5_kernel(uf_ref, ub_ref, h0_ref, mask_ref, bst_ref, a_ref, cw_ref,
               yf_ref, yb_ref, hfin_ref, lhs_scr, bu_scr, hs_scr, h_scr, *, bsz, tc):
    i = pl.program_id(0)

    @pl.when(i == 0)
    def _():
        h_scr[...] = h0_ref[...]

    mask = mask_ref[...]
    n_chain = 2 * bsz

    for d, u_ref in enumerate((uf_ref, ub_ref)):
        for b in range(bsz):
            for j in range(tc // 2):
                pair = [jnp.broadcast_to(u_ref[b, 2 * j + k:2 * j + k + 1, :], (SUBLANES, S5_WIDTH)) * mask
                        for k in range(2)]
                lhs_scr[d * bsz + b, 2 * SUBLANES * j:2 * SUBLANES * (j + 1), :] = (
                    jnp.concatenate(pair, axis=0).astype(BF16))
    for c in range(n_chain):
        bu_scr[c] = _dot(lhs_scr[c], bst_ref[c // bsz])

    a_re = [a_ref[d, 0] for d in range(2)]
    a_im = [a_ref[d, 1] for d in range(2)]

    def step(t, carry):
        new = []
        for c in range(n_chain):
            d = c // bsz
            tt = t if d == 0 else tc - 1 - t
            r0 = pl.multiple_of(tt * SUBLANES, SUBLANES)
            h_re, h_im = carry[2 * c], carry[2 * c + 1]
            n_re = a_re[d] * h_re - a_im[d] * h_im + bu_scr[c, pl.ds(r0, SUBLANES), 0:LANES]
            n_im = a_re[d] * h_im + a_im[d] * h_re + bu_scr[c, pl.ds(r0, SUBLANES), LANES:2 * LANES]
            hs_scr[c, 0, pl.ds(r0, SUBLANES), :] = n_re
            hs_scr[c, 1, pl.ds(r0, SUBLANES), :] = n_im
            new += [n_re, n_im]
        return tuple(new)

    init = []
    for c in range(n_chain):
        init += [h_scr[c, :, 0:LANES], h_scr[c, :, LANES:2 * LANES]]
    fin = lax.fori_loop(0, tc, step, tuple(init), unroll=8)
    for c in range(n_chain):
        h_scr[c, :, 0:LANES] = fin[2 * c]
        h_scr[c, :, LANES:2 * LANES] = fin[2 * c + 1]
    hfin_ref[...] = h_scr[...]

    for d, y_ref in enumerate((yf_ref, yb_ref)):
        for b in range(bsz):
            c = d * bsz + b
            parts = [hs_scr[c, ri, pl.ds(s, tc, stride=SUBLANES), :].astype(BF16)
                     for s in range(SUBLANES) for ri in range(2)]
            y_ref[b] = _dot(jnp.concatenate(parts, axis=-1), cw_ref[d])


def _s5_scan(u, h0, mask, bst, a_tiles, cwide, *, tc):
    bsz, seq, _ = u.shape
    nch = seq // tc
    full = lambda shape: pl.BlockSpec(shape, lambda i: (0,) * len(shape))
    fwd = pl.BlockSpec((bsz, tc, S5_WIDTH), lambda i: (0, i, 0))
    bwd = pl.BlockSpec((bsz, tc, S5_WIDTH), lambda i: (0, nch - 1 - i, 0))
    state = (2 * bsz, SUBLANES, 2 * LANES)
    rows = SUBLANES * tc
    return pl.pallas_call(
        functools.partial(_s5_kernel, bsz=bsz, tc=tc),
        grid=(nch,),
        in_specs=[fwd, bwd, full(state), full(mask.shape), full(bst.shape), full(a_tiles.shape),
                  full(cwide.shape)],
        out_specs=[fwd, bwd, full(state)],
        out_shape=[jax.ShapeDtypeStruct(u.shape, F32), jax.ShapeDtypeStruct(u.shape, F32),
                   jax.ShapeDtypeStruct(state, F32)],
        scratch_shapes=[pltpu.VMEM((2 * bsz, rows, S5_WIDTH), BF16),
                        pltpu.VMEM((2 * bsz, rows, 2 * LANES), F32),
                        pltpu.VMEM((2 * bsz, 2, rows, LANES), F32),
                        pltpu.VMEM(state, F32)],
        compiler_params=_cparams("arbitrary"),
        name="s5_scan",
    )(u, u, h0, mask, bst, a_tiles, cwide)


def _s5_mask():
    m = np.arange(S5_WIDTH)[None, :] // CH_PER_CHUNK == np.arange(SUBLANES)[:, None]
    return jnp.asarray(m.astype(np.float32), F32)


def _s5_params(lam_re, lam_im, log_dt, b_re, b_im, c_re, c_im):
    dt = jnp.exp(log_dt)[..., None]
    mag = jnp.exp(lam_re * dt)
    l_re = mag * jnp.cos(lam_im * dt)
    l_im = mag * jnp.sin(lam_im * dt)
    den = lam_re * lam_re + lam_im * lam_im
    f_re = ((l_re - 1.0) * lam_re + l_im * lam_im) / den
    f_im = (l_im * lam_re - (l_re - 1.0) * lam_im) / den
    bb_re = f_re[..., None] * b_re - f_im[..., None] * b_im
    bb_im = f_re[..., None] * b_im + f_im[..., None] * b_re
    a_tiles = jnp.stack([l_re.reshape(2, SUBLANES, LANES), l_im.reshape(2, SUBLANES, LANES)], axis=1)

    half = S5_GROUPS // SUBLANES
    eye = jnp.eye(half, dtype=F32)

    def in_mat(bb):
        t = bb.reshape(2, SUBLANES, half, S5_STATE, S5_GROUP)
        m = jnp.einsum('dsgpc,gh->dsgchp', t, eye)
        return m.reshape(2, S5_WIDTH, half * S5_STATE)

    def out_mat(cc):
        t = cc.reshape(2, SUBLANES, half, S5_GROUP, S5_STATE)
        m = jnp.einsum('dsgcp,st,gh->dsgpthc', t, jnp.eye(SUBLANES, dtype=F32), eye)
        return m.reshape(2, SUBLANES, half * S5_STATE, S5_WIDTH)

    bst = jnp.concatenate([in_mat(bb_re), in_mat(bb_im)], axis=-1).astype(BF16)
    cwide = jnp.stack([out_mat(c_re), out_mat(-c_im)], axis=2)
    cwide = cwide.reshape(2, 2 * SUBLANES * LANES, S5_WIDTH).astype(BF16)
    return bst, a_tiles, cwide


def _conv_kernel(prev_ref, cur_ref, next_ref, wdw_ref, bdw_ref, g_ref, b_ref, wpw_ref, bpw_ref,
                 o_ref, win_ref, *, tm):
    i = pl.program_id(1)
    nt = pl.num_programs(1)
    zero = jnp.zeros((CONV_HALO, CONV_WIDTH), F32)
    win_ref[0:CONV_HALO] = jnp.where(i > 0, prev_ref[0], zero)
    win_ref[CONV_HALO:CONV_HALO + tm] = cur_ref[0]
    win_ref[CONV_HALO + tm:] = jnp.where(i < nt - 1, next_ref[0], zero)
    acc = jnp.zeros((tm, CONV_WIDTH), F32) + bdw_ref[...]
    off = CONV_HALO - CONV_K // 2
    for k in range(CONV_K):
        acc = acc + win_ref[off + k:off + k + tm, :] * wdw_ref[k:k + 1, :]
    h = _silu(_layer_norm(acc) * g_ref[...] + b_ref[...])
    o_ref[0] = (_dot(h.astype(BF16), wpw_ref[...]) + bpw_ref[...]).astype(BF16)


def _conformer_conv(cg, w_dw, b_dw, ln_g, ln_b, w_pw_bf, b_pw, *, tm):
    bsz, seq, _ = cg.shape
    hb = tm // CONV_HALO
    last = seq // CONV_HALO - 1
    row = lambda a: a.reshape(1, CONV_WIDTH)
    vec = pl.BlockSpec((1, CONV_WIDTH), lambda b, i: (0, 0))
    return pl.pallas_call(
        functools.partial(_conv_kernel, tm=tm),
        grid=(bsz, seq // tm),
        in_specs=[
            pl.BlockSpec((1, CONV_HALO, CONV_WIDTH), lambda b, i: (b, jnp.maximum(i * hb - 1, 0), 0)),
            pl.BlockSpec((1, tm, CONV_WIDTH), lambda b, i: (b, i, 0)),
            pl.BlockSpec((1, CONV_HALO, CONV_WIDTH), lambda b, i: (b, jnp.minimum((i + 1) * hb, last), 0)),
            pl.BlockSpec((CONV_K + 1, CONV_WIDTH), lambda b, i: (0, 0)),
            vec, vec, vec,
            pl.BlockSpec((CONV_WIDTH, CONV_WIDTH), lambda b, i: (0, 0)),
            vec,
        ],
        out_specs=pl.BlockSpec((1, tm, CONV_WIDTH), lambda b, i: (b, i, 0)),
        out_shape=jax.ShapeDtypeStruct((bsz, seq, CONV_WIDTH), BF16),
        scratch_shapes=[pltpu.VMEM((tm + 2 * CONV_HALO, CONV_WIDTH), F32)],
        compiler_params=_cparams("arbitrary", "arbitrary"),
        name="conformer_conv",
    )(cg, cg, cg, jnp.pad(w_dw.reshape(CONV_K, CONV_WIDTH), ((0, 1), (0, 0))), row(b_dw), row(ln_g),
      row(ln_b), w_pw_bf, row(b_pw))


def _mixout_kernel(attn_ref, yf_ref, yb_ref, u_ref, conv_ref, x_ref,
                   dskip_ref, wglu_ref, bglu_ref, wout_ref, bout_ref,
                   gate_ref, g1_ref, b1_ref, shift_ref, scale_ref, wr_hi_ref, wr_lo_ref,
                   xmid_ref, h2_ref, logit_ref):
    y = dskip_ref[...] * u_ref[0] + yf_ref[0] + yb_ref[0]
    z = _gelu_tanh(y)
    s5 = z * _sigmoid(_dot(z.astype(BF16), wglu_ref[...]) + bglu_ref[...])
    y_mix = (_dot(attn_ref[0], wout_ref[0:ATTN_WIDTH, :])
             + _dot(s5.astype(BF16), wout_ref[ATTN_WIDTH:ATTN_WIDTH + S5_WIDTH, :])
             + _dot(conv_ref[0], wout_ref[ATTN_WIDTH + S5_WIDTH:, :])
             + bout_ref[...])
    r = DEEPNORM_ALPHA * x_ref[0] + gate_ref[0] * y_mix
    x_mid = _layer_norm(r) * g1_ref[...] + b1_ref[...]
    xmid_ref[0] = x_mid
    h2 = _layer_norm(x_mid) * (1.0 + scale_ref[0]) + shift_ref[0]
    h2_ref[0] = h2.astype(BF16)
    logit_ref[0] = _dot3(h2, wr_hi_ref[...], wr_lo_ref[...])


def _mixer_output(attn, yf, yb, u, conv, x, d_skip, w_glu_bf, b_glu, w_out_bf, b_out,
                  gate1, ln_g, ln_b, shift2, scale2, wr_hi, wr_lo, *, tm):
    bsz, seq, _ = x.shape
    tok = lambda w: pl.BlockSpec((1, tm, w), lambda b, i: (b, i, 0))
    full = lambda r, c: pl.BlockSpec((r, c), lambda b, i: (0, 0))
    bvec = pl.BlockSpec((1, 1, D_MODEL), lambda b, i: (b, 0, 0))
    row = lambda a: a.reshape(1, -1)
    return pl.pallas_call(
        _mixout_kernel,
        grid=(bsz, seq // tm),
        in_specs=[
            tok(ATTN_WIDTH), tok(S5_WIDTH), tok(S5_WIDTH), tok(S5_WIDTH), tok(CONV_WIDTH), tok(D_MODEL),
            full(1, S5_WIDTH), full(S5_WIDTH, S5_WIDTH), full(1, S5_WIDTH),
            full(D_MODEL, D_MODEL), full(1, D_MODEL),
            bvec, full(1, D_MODEL), full(1, D_MODEL), bvec, bvec,
            full(D_MODEL, LANES), full(D_MODEL, LANES),
        ],
        out_specs=[tok(D_MODEL), tok(D_MODEL), tok(LANES)],
        out_shape=[
            jax.ShapeDtypeStruct((bsz, seq, D_MODEL), F32),
            jax.ShapeDtypeStruct((bsz, seq, D_MODEL), BF16),
            jax.ShapeDtypeStruct((bsz, seq, LANES), F32),
        ],
        compiler_params=_cparams("arbitrary", "arbitrary"),
        name="mixer_output",
    )(attn, yf, yb, u, conv, x, row(d_skip), w_glu_bf, row(b_glu), w_out_bf, row(b_out),
      gate1, row(ln_g), row(ln_b), shift2, scale2, wr_hi, wr_lo)


def _token_cumsum(m, tri, ones):
    nblk = m.shape[0]
    m2 = m.reshape(nblk * N_EXPERTS, LANES).astype(BF16)
    within = _dot(m2, tri).reshape(nblk, N_EXPERTS, LANES)
    tot = _dot(m2, ones).reshape(nblk, N_EXPERTS, LANES)
    offs = []
    run = jnp.zeros((N_EXPERTS, LANES), F32)
    for j in range(nblk):
        offs.append(run)
        run = run + tot[j]
    off = jnp.stack(offs, axis=0)
    return within + off, off, tot


def _router_kernel(logit_ref, tri_ref, ones_ref, aff_ref, sel_ref, pos_ref, off_ref, cnt_ref, *, cap, slot0):
    b = pl.program_id(0)
    nblk = aff_ref.shape[1]

    def soft(j, carry):
        r0 = pl.multiple_of(j * LANES, LANES)
        t = logit_ref[0, pl.ds(r0, LANES), :].T[0:N_EXPERTS]
        ex = jnp.exp(t - jnp.max(t, axis=0, keepdims=True))
        aff_ref[0, j] = ex / jnp.sum(ex, axis=0, keepdims=True)
        return carry

    lax.fori_loop(0, nblk, soft, 0)
    aff = aff_ref[0]

    def enough(cand):
        cnt = jnp.sum(jnp.where(aff >= cand[None], 1.0, 0.0), axis=0)
        return jnp.sum(cnt, axis=-1, keepdims=True) >= cap

    p = jnp.full((N_EXPERTS, LANES), 2.0, F32)
    for k in range(6, -1, -1):
        cand = p * (2.0 ** -(2 ** k))
        p = jnp.where(enough(cand), p, cand)
    thr = 0.5 * p
    thr = jnp.where(enough(thr), thr, 0.0)

    def refine(_, carry):
        lo, step = carry
        cand = lo + step
        return jnp.where(enough(cand), cand, lo), 0.5 * step

    thr, _ = lax.fori_loop(0, 23, refine, (thr, 0.5 * thr))
    gt = aff > thr[None]
    eq = aff == thr[None]
    n_gt = jnp.sum(jnp.sum(jnp.where(gt, 1.0, 0.0), axis=0), axis=-1, keepdims=True)
    need = cap - n_gt
    tri = tri_ref[...]
    ones = ones_ref[...]
    cum_eq, _, _ = _token_cumsum(jnp.where(eq, 1.0, 0.0), tri, ones)
    sel = jnp.where(gt | (eq & (cum_eq <= need[None])), 1.0, 0.0)
    cum_sel, off, cnt = _token_cumsum(sel, tri, ones)
    base = (slot0 + b * cap).astype(F32)
    sel_ref[0] = sel
    pos_ref[0] = cum_sel - sel + base
    off_ref[0] = off + base
    cnt_ref[0] = cnt


def _router(logits, *, slot0):
    bsz, n, _ = logits.shape
    nblk = n // LANES
    cap = CAPACITY_FACTOR * n // N_EXPERTS
    idx = np.arange(LANES)
    tri = jnp.asarray((idx[:, None] <= idx[None, :]).astype(np.float32), BF16)
    ones = jnp.ones((LANES, LANES), BF16)
    shape = (bsz, nblk, N_EXPERTS, LANES)
    out = pl.BlockSpec((1, nblk, N_EXPERTS, LANES), lambda b: (b, 0, 0, 0))
    sq = pl.BlockSpec((LANES, LANES), lambda b: (0, 0))
    return pl.pallas_call(
        functools.partial(_router_kernel, cap=cap, slot0=slot0),
        grid=(bsz,),
        in_specs=[pl.BlockSpec((1, n, LANES), lambda b: (b, 0, 0)), sq, sq],
        out_specs=[out] * 5,
        out_shape=[jax.ShapeDtypeStruct(shape, F32)] * 5,
        compiler_params=_cparams("arbitrary"),
        name="router",
    )(logits, tri, ones)


DISPATCH_KW = LANES + SUBLANES
COMBINE_KW = LANES + 16
SMALL_COUNT = 32
COMBINE_SMALL_KW = SMALL_COUNT + 16
X_STEP_TOKENS = 2048
FF_TILE = 256


def _dispatch_blocks(tb_ref, n_entries, e, blocks, xs_scr, g_scr):
    most = tb_ref[n_entries + blocks[0][0] * N_EXPERTS + e]
    for tile, _ in blocks[1:]:
        most = jnp.maximum(most, tb_ref[n_entries + tile * N_EXPERTS + e])

    def run(kw):
        k = lax.broadcasted_iota(jnp.int32, (kw, LANES), 0).astype(F32)
        for tile, load in blocks:
            h2_blk, pos_row, sel_row, aff_row = load()
            base = tb_ref[tile * N_EXPERTS + e]
            base8 = pl.multiple_of(lax.shift_left(lax.shift_right_logical(base, 3), 3), SUBLANES)
            hit = (k == (pos_row - base8.astype(F32))) & (sel_row > 0.5)
            sel_t = jnp.where(hit, 1.0, 0.0).astype(BF16)
            xs_scr[pl.ds(base8, kw), :] += _dot(sel_t, h2_blk)
            g = jnp.sum(jnp.where(hit, aff_row, 0.0), axis=-1, keepdims=True)
            g_scr[pl.ds(base8, kw), :] += jnp.broadcast_to(g, (kw, LANES))

    @pl.when(most <= SMALL_COUNT)
    def _():
        run(SMALL_COUNT + SUBLANES)

    @pl.when(most > SMALL_COUNT)
    def _():
        run(DISPATCH_KW)


def _moe_kernel(*refs, n_x_steps, n_ctx, rows, n_entries):
    tb_ref = refs[0]
    if n_ctx:
        (posx_ref, selx_ref, affx_ref, h2x_ref, posc_ref, selc_ref, affc_ref, h2c_ref,
         wg_ref, wu_ref, wd_ref, y_ref, acc_scr, g_scr, xs_scr) = refs[1:]
    else:
        (posx_ref, selx_ref, affx_ref, h2x_ref, wg_ref, wu_ref, wd_ref, y_ref,
         acc_scr, g_scr, xs_scr) = refs[1:]
    e = pl.program_id(0)
    s = pl.program_id(1)
    n_disp = n_x_steps + (1 if n_ctx else 0)
    blk_per_step = posx_ref.shape[2]

    @pl.when(s == 0)
    def _():
        acc_scr[...] = jnp.zeros_like(acc_scr)
        g_scr[...] = jnp.zeros_like(g_scr)

    @pl.when(s < n_x_steps)
    def _():
        def load_x(i):
            return lambda: (h2x_ref[0, i * LANES:(i + 1) * LANES, :], posx_ref[0, 0, i:i + 1, :],
                            selx_ref[0, 0, i:i + 1, :], affx_ref[0, 0, i:i + 1, :])

        blocks = [(s * blk_per_step + i, load_x(i)) for i in range(blk_per_step)]
        _dispatch_blocks(tb_ref, n_entries, e, blocks, acc_scr, g_scr)

    if n_ctx:
        @pl.when(s == n_x_steps)
        def _():
            bsz_c, blk_c = posc_ref.shape[0], posc_ref.shape[2]
            def load_c(b, i):
                return lambda: (h2c_ref[b, i * LANES:(i + 1) * LANES, :], posc_ref[b, 0, i:i + 1, :],
                                selc_ref[b, 0, i:i + 1, :], affc_ref[b, 0, i:i + 1, :])

            blocks = [(n_x_steps * blk_per_step + b * blk_c + i, load_c(b, i))
                      for b in range(bsz_c) for i in range(blk_c)]
            _dispatch_blocks(tb_ref, n_entries, e, blocks, acc_scr, g_scr)

    @pl.when(s == n_disp)
    def _():
        xs_scr[...] = acc_scr[0:rows, :].astype(BF16)

    @pl.when(s >= n_disp)
    def _():
        wg = wg_ref[0, 0].astype(BF16)
        wu = wu_ref[0, 0].astype(BF16)
        wd = wd_ref[0, 0].astype(BF16)
        first = jnp.where(s == n_disp, 0.0, 1.0)
        half = rows // 2
        for r0 in (0, half):
            xs = xs_scr[r0:r0 + half, :]
            hid = _silu(_dot(xs, wg)) * _dot(xs, wu)
            acc_scr[r0:r0 + half, :] = acc_scr[r0:r0 + half, :] * first + _dot(hid.astype(BF16), wd)

    @pl.when(s == pl.num_programs(1) - 1)
    def _():
        g = g_scr[0:rows, :]
        y_ref[0, 0:rows, :] = (acc_scr[0:rows, :] * jnp.concatenate([g] * (D_MODEL // LANES), axis=-1)).astype(BF16)
        y_ref[0, rows:, :] = jnp.zeros((y_ref.shape[1] - rows, D_MODEL), BF16)


def _moe_experts(tb, route_x, h2x, route_c, h2c, w_gate, w_up, w_down, *, layer):
    bsz, n, d = h2x.shape
    x_step = min(X_STEP_TOKENS, n)
    n_x_steps = bsz * n // x_step
    steps_per_b = n // x_step
    blk_per_step = x_step // LANES
    n_ctx = 0 if h2c is None else h2c.shape[0] * h2c.shape[1]
    rows = bsz * CAPACITY_FACTOR * n // N_EXPERTS + CAPACITY_FACTOR * n_ctx // N_EXPERTS
    rows_pad = rows + COMBINE_KW
    n_disp = n_x_steps + (1 if n_ctx else 0)
    n_ff = EXPERT_FF // FF_TILE

    def xs_idx(s):
        sc = jnp.minimum(s, n_x_steps - 1)
        return sc // steps_per_b, sc % steps_per_b

    def route_x_spec():
        return pl.BlockSpec((1, 1, blk_per_step, LANES),
                            lambda e, s, tb: (xs_idx(s)[0], e, xs_idx(s)[1], 0))

    in_specs = [route_x_spec(), route_x_spec(), route_x_spec(),
                pl.BlockSpec((1, x_step, d), lambda e, s, tb: (xs_idx(s)[0], xs_idx(s)[1], 0))]
    args = list(route_x) + [h2x]
    if n_ctx:
        bc, nc, _ = h2c.shape
        rc = pl.BlockSpec((bc, 1, nc // LANES, LANES), lambda e, s, tb: (0, e, 0, 0))
        in_specs += [rc, rc, rc, pl.BlockSpec((bc, nc, d), lambda e, s, tb: (0, 0, 0))]
        args += list(route_c) + [h2c]
    ff = lambda s: jnp.maximum(s - n_disp, 0)
    in_specs += [pl.BlockSpec((1, 1, d, FF_TILE), lambda e, s, tb: (layer, e, 0, ff(s))),
                 pl.BlockSpec((1, 1, d, FF_TILE), lambda e, s, tb: (layer, e, 0, ff(s))),
                 pl.BlockSpec((1, 1, FF_TILE, d), lambda e, s, tb: (layer, e, ff(s), 0))]
    args += [w_gate, w_up, w_down]
    scr_rows = rows + DISPATCH_KW
    return pl.pallas_call(
        functools.partial(_moe_kernel, n_x_steps=n_x_steps, n_ctx=n_ctx, rows=rows,
                          n_entries=tb.shape[0] // 2),
        grid_spec=pltpu.PrefetchScalarGridSpec(
            num_scalar_prefetch=1,
            grid=(N_EXPERTS, n_disp + n_ff),
            in_specs=in_specs,
            out_specs=pl.BlockSpec((1, rows_pad, d), lambda e, s, tb: (e, 0, 0)),
            scratch_shapes=[pltpu.VMEM((scr_rows, d), F32), pltpu.VMEM((scr_rows, LANES), F32),
                            pltpu.VMEM((rows, d), BF16)],
        ),
        out_shape=jax.ShapeDtypeStruct((N_EXPERTS, rows_pad, d), BF16),
        compiler_params=_cparams("arbitrary", "arbitrary"),
        name="moe_experts",
    )(tb, *args)


def _combine_kernel(tb_ref, pos_ref, sel_ref, xmid_ref, gate_ref, g_ref, b_ref, y_hbm, o_ref,
                    y_scr, stack_scr, sem, *, cap, slot0, blk_per_step, nblk, n_entries):
    b = pl.program_id(0)
    j = pl.program_id(1)
    win = y_scr.shape[1]

    @pl.when(j == 0)
    def _():
        row0 = pl.multiple_of(slot0 + b * cap, 16)
        cp = pltpu.make_async_copy(y_hbm.at[:, pl.ds(row0, win), :], y_scr, sem)
        cp.start()
        cp.wait()

    set_base = (slot0 + b * cap).astype(F32)
    tn_dims = (((0,), (0,)), ((), ()))
    for i in range(blk_per_step):
        entry = (b * nblk + j * blk_per_step + i) * N_EXPERTS
        rows = slice(i * LANES, (i + 1) * LANES)

        def hits(e, kw, entry=entry, i=i):
            base = tb_ref[entry + e] - (slot0 + b * cap)
            base16 = pl.multiple_of(lax.shift_left(lax.shift_right_logical(base, 4), 4), 16)
            k = lax.broadcasted_iota(jnp.int32, (kw, LANES), 0).astype(F32)
            rel = pos_ref[0, i, e:e + 1, :] - (set_base + base16.astype(F32))
            hit = (k == rel) & (sel_ref[0, i, e:e + 1, :] > 0.5)
            return base16, jnp.where(hit, 1.0, 0.0).astype(BF16)

        def finish(moe, rows=rows):
            r = DEEPNORM_ALPHA * xmid_ref[0, rows, :] + gate_ref[0] * moe
            o_ref[0, rows, :] = _layer_norm(r) * g_ref[...] + b_ref[...]

        most = tb_ref[n_entries + entry]
        for e in range(1, N_EXPERTS):
            most = jnp.maximum(most, tb_ref[n_entries + entry + e])

        @pl.when(most <= SMALL_COUNT)
        def _():
            sel_all = []
            for e in range(N_EXPERTS):
                base16, sel_t = hits(e, COMBINE_SMALL_KW)
                sel_all.append(sel_t)
                stack_scr[e * COMBINE_SMALL_KW:(e + 1) * COMBINE_SMALL_KW, :] = (
                    y_scr[e, pl.ds(base16, COMBINE_SMALL_KW), :])
            finish(lax.dot_general(jnp.concatenate(sel_all, axis=0), stack_scr[...], tn_dims,
                                   preferred_element_type=F32))

        @pl.when(most > SMALL_COUNT)
        def _():
            acc = jnp.zeros((LANES, D_MODEL), F32)
            for e in range(N_EXPERTS):
                base16, sel_t = hits(e, COMBINE_KW)
                acc = acc + lax.dot_general(sel_t, y_scr[e, pl.ds(base16, COMBINE_KW), :], tn_dims,
                                            preferred_element_type=F32)
            finish(acc)


def _combine_post(tb, pos, sel, x_mid, gate2, ln_g, ln_b, y, *, slot0, tm):
    bsz, n, d = x_mid.shape
    nblk = n // LANES
    cap = CAPACITY_FACTOR * n // N_EXPERTS
    blk_per_step = tm // LANES
    tok = pl.BlockSpec((1, tm, d), lambda b, j, tb: (b, j, 0))
    route = pl.BlockSpec((1, blk_per_step, N_EXPERTS, LANES), lambda b, j, tb: (b, j, 0, 0))
    vec = pl.BlockSpec((1, d), lambda b, j, tb: (0, 0))
    return pl.pallas_call(
        functools.partial(_combine_kernel, cap=cap, slot0=slot0, blk_per_step=blk_per_step, nblk=nblk,
                          n_entries=bsz * nblk * N_EXPERTS),
        grid_spec=pltpu.PrefetchScalarGridSpec(
            num_scalar_prefetch=1,
            grid=(bsz, n // tm),
            in_specs=[route, route, tok, pl.BlockSpec((1, 1, d), lambda b, j, tb: (b, 0, 0)), vec, vec,
                      pl.BlockSpec(memory_space=pl.ANY)],
            out_specs=tok,
            scratch_shapes=[pltpu.VMEM((N_EXPERTS, cap + COMBINE_KW, d), BF16),
                            pltpu.VMEM((N_EXPERTS * COMBINE_SMALL_KW, d), BF16),
                            pltpu.SemaphoreType.DMA(())],
        ),
        out_shape=jax.ShapeDtypeStruct(x_mid.shape, F32),
        compiler_params=_cparams("arbitrary", "arbitrary"),
        name="combine_post",
    )(tb, pos, sel, x_mid, gate2, ln_g.reshape(1, d), ln_b.reshape(1, d), y)


def _tile_table(*lane_replicated):
    return jnp.concatenate([a[..., 0].astype(jnp.int32).reshape(-1) for a in lane_replicated])


def _expert_major(a):
    return jnp.swapaxes(a, 1, 2)


def kernel(x, c, ctx, c_ctx, w_mod, b_mod, w_in, b_in, attn_sink, s5_lam_re, s5_lam_im, s5_log_dt,
           s5_b_re, s5_b_im, s5_c_re, s5_c_im, s5_d, s5_w_glu, s5_b_glu, conv_w_dw, conv_b_dw,
           conv_ln_g, conv_ln_b, conv_w_pw, conv_b_pw, w_out, b_out, ln1_g, ln1_b, w_router,
           exp_w_gate, exp_w_up, exp_w_down, ln2_g, ln2_b):
    bsz, seq, d = x.shape
    lc = ctx.shape[1]
    tm_x, tm_c = 512, lc

    cond = jnp.zeros((SUBLANES, d), F32).at[:bsz].set(c).at[bsz].set(c_ctx)
    mods = _modulation(cond, w_mod, b_mod)
    cos_t, sin_t = _rope_tables(seq)
    s5_mask = _s5_mask()
    zero_state = jnp.zeros((2 * bsz, SUBLANES, 2 * LANES), F32)

    xc = ctx
    for l in range(DEPTH):
        last = l == DEPTH - 1
        mod_x = [mods[l, :bsz, k * d:(k + 1) * d].reshape(bsz, 1, d) for k in range(6)]
        mod_c = [jnp.broadcast_to(mods[l, bsz, k * d:(k + 1) * d].reshape(1, 1, d), (bsz, 1, d))
                 for k in range(6)]
        w_in_bf = w_in[l].astype(BF16)
        w_out_bf = w_out[l].astype(BF16)
        w_glu_bf = s5_w_glu[l].astype(BF16)
        w_pw_bf = conv_w_pw[l].astype(BF16)
        wr = jnp.pad(w_router[l], ((0, 0), (0, LANES - N_EXPERTS)))
        wr_hi, wr_lo = _split_bf16(wr)
        sink_rep = jnp.broadcast_to(attn_sink[l][:, None], (N_Q_HEADS, LANES))
        bst, a_tiles, cwide = _s5_params(s5_lam_re[l], s5_lam_im[l], s5_log_dt[l], s5_b_re[l], s5_b_im[l],
                                       s5_c_re[l], s5_c_im[l])

        q, k, v, u, cg = _in_projection(x, mod_x[0], mod_x[1], w_in_bf, b_in[l], cos_t, sin_t,
                                        rope=True, tm=tm_x)
        q_c, k_c, v_c, u_c, cg_c = _in_projection(xc, mod_c[0], mod_c[1], w_in_bf, b_in[l],
                                                  cos_t[:lc], sin_t[:lc], rope=False, tm=tm_c)

        attn_x = _attention(q, k, v, k_c, v_c, sink_rep, window=True)
        yf_c, yb_c, h_ctx = _s5_scan(u_c, zero_state, s5_mask, bst, a_tiles, cwide, tc=lc)
        yf, yb, _ = _s5_scan(u, h_ctx, s5_mask, bst, a_tiles, cwide, tc=256)
        conv_args = (conv_w_dw[l], conv_b_dw[l], conv_ln_g[l], conv_ln_b[l], w_pw_bf, conv_b_pw[l])
        conv_x = _conformer_conv(cg, *conv_args, tm=256)
        mix_args = (s5_d[l], w_glu_bf, s5_b_glu[l], w_out_bf, b_out[l])
        x_mid, h2, logits = _mixer_output(attn_x, yf, yb, u, conv_x, x, *mix_args, mod_x[2], ln1_g[l],
                                          ln1_b[l], mod_x[3], mod_x[4], wr_hi, wr_lo, tm=tm_x)
        aff_x, sel_x, pos_x, off_x, cnt_x = _router(logits, slot0=0)
        route_x = tuple(_expert_major(a) for a in (pos_x, sel_x, aff_x))
        tb_x = _tile_table(off_x, cnt_x)
        if not last:
            attn_c = _attention(q_c, None, None, k_c, v_c, sink_rep, window=False)
            conv_c = _conformer_conv(cg_c, *conv_args, tm=lc)
            xc_mid, hc2, logits_c = _mixer_output(attn_c, yf_c, yb_c, u_c, conv_c, xc, *mix_args, mod_c[2],
                                                  ln1_g[l], ln1_b[l], mod_c[3], mod_c[4], wr_hi, wr_lo,
                                                  tm=tm_c)
            slot0_c = bsz * CAPACITY_FACTOR * seq // N_EXPERTS
            aff_c, sel_c, pos_c, off_c, cnt_c = _router(logits_c, slot0=slot0_c)
            route_c = tuple(_expert_major(a) for a in (pos_c, sel_c, aff_c))
            tb_c = _tile_table(off_c, cnt_c)
            y = _moe_experts(_tile_table(off_x, off_c, cnt_x, cnt_c), route_x, h2, route_c, hc2,
                             exp_w_gate, exp_w_up, exp_w_down, layer=l)
            xc = _combine_post(tb_c, pos_c, sel_c, xc_mid, mod_c[5], ln2_g[l], ln2_b[l], y,
                               slot0=slot0_c, tm=lc)
        else:
            y = _moe_experts(tb_x, route_x, h2, None, None, exp_w_gate, exp_w_up, exp_w_down, layer=l)
        x = _combine_post(tb_x, pos_x, sel_x, x_mid, mod_x[5], ln2_g[l], ln2_b[l], y, slot0=0, tm=256)
    return x
```

```python
import functools
import math

import jax
import jax.numpy as jnp
import numpy as np
from jax import lax
from jax.experimental import pallas as pl
from jax.experimental.pallas import tpu as pltpu

D_MODEL = 1024
DEPTH = 2
GRID_W = 64
HEAD_DIM = 64
ATTN_WIDTH = D_MODEL // 2
N_Q_HEADS = ATTN_WIDTH // HEAD_DIM
N_KV_HEADS = N_Q_HEADS // 4
Q_PER_KV = N_Q_HEADS // N_KV_HEADS
KV_WIDTH = N_KV_HEADS * HEAD_DIM
BLOCK = 128
ROPE_BASE = 10000.0
NEG_INF = -1e30
S5_WIDTH = D_MODEL // 4
S5_GROUP = 16
S5_GROUPS = S5_WIDTH // S5_GROUP
S5_STATE = 64
CONV_WIDTH = D_MODEL - ATTN_WIDTH - S5_WIDTH
CONV_K = 31
CONV_HALO = 16
Q_END = ATTN_WIDTH
K_END = Q_END + KV_WIDTH
V_END = K_END + KV_WIDTH
U_END = V_END + S5_WIDTH
IN_WIDTH = U_END + 2 * CONV_WIDTH
N_EXPERTS = 16
EXPERT_FF = 2 * D_MODEL
CAPACITY_FACTOR = 2
DEEPNORM_ALPHA = (2.0 * DEPTH) ** 0.25
LN_EPS = 1e-5

LANES = 128
SUBLANES = 8
S5_CHUNKS = S5_WIDTH * S5_STATE // S5_GROUP // LANES
CH_PER_CHUNK = S5_WIDTH // S5_CHUNKS
VMEM_LIMIT = 56 * 1024 * 1024

F32 = jnp.float32
BF16 = jnp.bfloat16


def _cparams(*sem):
    return pltpu.CompilerParams(dimension_semantics=sem, vmem_limit_bytes=VMEM_LIMIT)


def _dot(a, b):
    return jnp.dot(a, b, preferred_element_type=F32)


def _dot_nt(a, b):
    return lax.dot_general(a, b, (((1,), (1,)), ((), ())), preferred_element_type=F32)


def _split_bf16(x):
    hi = x.astype(BF16)
    lo = (x - hi.astype(F32)).astype(BF16)
    return hi, lo


def _dot3(a, b_hi, b_lo):
    a_hi, a_lo = _split_bf16(a)
    return _dot(a_hi, b_hi) + (_dot(a_lo, b_hi) + _dot(a_hi, b_lo))


def _sigmoid(x):
    return 1.0 / (1.0 + jnp.exp(-x))


def _silu(x):
    return x * _sigmoid(x)


def _gelu_tanh(x):
    c = math.sqrt(2.0 / math.pi)
    return 0.5 * x * (1.0 + jnp.tanh(c * (x + 0.044715 * (x * x * x))))


def _layer_norm(x):
    mu = jnp.mean(x, axis=-1, keepdims=True)
    xc = x - mu
    var = jnp.mean(xc * xc, axis=-1, keepdims=True)
    return xc * lax.rsqrt(var + LN_EPS)


def _mod_kernel(c_ref, w_ref, b_ref, o_ref):
    s = _silu(c_ref[...])
    w = w_ref[0]
    w_hi, w_lo = _split_bf16(w)
    o_ref[0] = _dot3(s, w_hi, w_lo) + b_ref[0]


def _modulation(cond, w_mod, b_mod):
    tn = 1536
    n = w_mod.shape[-1]
    return pl.pallas_call(
        _mod_kernel,
        grid=(DEPTH, n // tn),
        in_specs=[
            pl.BlockSpec((SUBLANES, D_MODEL), lambda l, j: (0, 0)),
            pl.BlockSpec((1, D_MODEL, tn), lambda l, j: (l, 0, j)),
            pl.BlockSpec((1, 1, tn), lambda l, j: (l, 0, j)),
        ],
        out_specs=pl.BlockSpec((1, SUBLANES, tn), lambda l, j: (l, 0, j)),
        out_shape=jax.ShapeDtypeStruct((DEPTH, SUBLANES, n), F32),
        compiler_params=_cparams("arbitrary", "arbitrary"),
        name="modulation",
    )(cond, w_mod, b_mod.reshape(DEPTH, 1, n))


def _rope_chunk(x, cos, sin_signed):
    lane = lax.broadcasted_iota(jnp.int32, x.shape, 1)
    first = (lane % 32) < 16
    partner = jnp.where(first, pltpu.roll(x, LANES - 16, 1), pltpu.roll(x, 16, 1))
    return x * cos + partner * sin_signed


def _inproj_kernel(x_ref, shift_ref, scale_ref, w_ref, b_ref, cos_ref, sin_ref,
                   q_ref, k_ref, v_ref, u_ref, cg_ref, *, rope):
    x = x_ref[0]
    h = _layer_norm(x) * (1.0 + scale_ref[0]) + shift_ref[0]
    p = _dot(h.astype(BF16), w_ref[...]) + b_ref[...]
    scale = HEAD_DIM ** -0.5
    if rope:
        cos = cos_ref[...]
        sin = sin_ref[...]
    for j in range(ATTN_WIDTH // LANES):
        qc = p[:, j * LANES:(j + 1) * LANES]
        if rope:
            qc = _rope_chunk(qc, cos, sin)
        q_ref[0, :, j * LANES:(j + 1) * LANES] = (qc * scale).astype(BF16)
    kc = p[:, Q_END:K_END]
    if rope:
        kc = _rope_chunk(kc, cos, sin)
    k_ref[0] = kc.astype(BF16)
    v_ref[0] = p[:, K_END:V_END].astype(BF16)
    u_ref[0] = p[:, V_END:U_END]
    a = p[:, U_END:U_END + CONV_WIDTH]
    g = p[:, U_END + CONV_WIDTH:]
    cg_ref[0] = a * _sigmoid(g)


def _in_projection(x, shift, scale, w_in_bf, b_in, cos_t, sin_t, *, rope, tm):
    bsz, seq, _ = x.shape
    tok = lambda w: pl.BlockSpec((1, tm, w), lambda b, i: (b, i, 0))
    vec = pl.BlockSpec((1, 1, D_MODEL), lambda b, i: (b, 0, 0))
    tab = pl.BlockSpec((tm, LANES), lambda b, i: (i, 0))
    return pl.pallas_call(
        functools.partial(_inproj_kernel, rope=rope),
        grid=(bsz, seq // tm),
        in_specs=[
            tok(D_MODEL), vec, vec,
            pl.BlockSpec((D_MODEL, IN_WIDTH), lambda b, i: (0, 0)),
            pl.BlockSpec((1, IN_WIDTH), lambda b, i: (0, 0)),
            tab, tab,
        ],
        out_specs=[tok(ATTN_WIDTH), tok(KV_WIDTH), tok(KV_WIDTH), tok(S5_WIDTH), tok(CONV_WIDTH)],
        out_shape=[
            jax.ShapeDtypeStruct((bsz, seq, ATTN_WIDTH), BF16),
            jax.ShapeDtypeStruct((bsz, seq, KV_WIDTH), BF16),
            jax.ShapeDtypeStruct((bsz, seq, KV_WIDTH), BF16),
            jax.ShapeDtypeStruct((bsz, seq, S5_WIDTH), F32),
            jax.ShapeDtypeStruct((bsz, seq, CONV_WIDTH), F32),
        ],
        compiler_params=_cparams("arbitrary", "arbitrary"),
        name="in_projection",
    )(x, shift, scale, w_in_bf, b_in.reshape(1, IN_WIDTH), cos_t, sin_t)


def _rope_tables(seq):
    lane = np.arange(LANES)
    i = lane % HEAD_DIM
    use_row = i < HEAD_DIM // 2
    f = HEAD_DIM // 4
    inv_freq = jnp.asarray(ROPE_BASE, F32) ** (-jnp.asarray(i % f, F32) / f)
    t = jnp.arange(seq, dtype=jnp.int32)
    pos = jnp.where(use_row[None, :], (t // GRID_W)[:, None], (t % GRID_W)[:, None]).astype(F32)
    ang = pos * inv_freq[None, :]
    sign = jnp.where((lane % (2 * f)) < f, -1.0, 1.0).astype(F32)
    return jnp.cos(ang), jnp.sin(ang) * sign[None, :]


def _attn_kernel(*refs, window):
    if window:
        q_ref, kp_ref, kc_ref, kn_ref, vp_ref, vc_ref, vn_ref, kx_ref, vx_ref, sink_ref, o_ref = refs
    else:
        q_ref, kx_ref, vx_ref, sink_ref, o_ref = refs
    n = pl.program_id(1)
    nb = pl.num_programs(1)
    q = q_ref[0]
    tq = q.shape[0]
    rows = Q_PER_KV * tq
    if window:
        row = lax.broadcasted_iota(jnp.int32, (rows, BLOCK), 0) % tq
        col = lax.broadcasted_iota(jnp.int32, (rows, BLOCK), 1)
        ok_prev = (col >= row) & (n > 0)
        ok_next = (col <= row) & (n < nb - 1)

    def with_ones(v, ks):
        return jnp.concatenate([v[:, ks], jnp.ones((v.shape[0], HEAD_DIM), BF16)], axis=-1)

    outs = []
    for g in range(N_KV_HEADS):
        ks = slice(g * HEAD_DIM, (g + 1) * HEAD_DIM)
        heads = range(g * Q_PER_KV, (g + 1) * Q_PER_KV)
        qs = jnp.concatenate([q[:, h * HEAD_DIM:(h + 1) * HEAD_DIM] for h in heads], axis=0)
        sink = jnp.concatenate([jnp.broadcast_to(sink_ref[h:h + 1, 0:1], (tq, 1)) for h in heads], axis=0)
        lx = kx_ref.shape[1]
        s_x = _dot_nt(qs, kx_ref[0][:, ks])
        m_el = s_x[:, 0:LANES]
        for c in range(1, lx // LANES):
            m_el = jnp.maximum(m_el, s_x[:, c * LANES:(c + 1) * LANES])
        if window:
            s_p = jnp.where(ok_prev, _dot_nt(qs, kp_ref[0][:, ks]), NEG_INF)
            s_c = _dot_nt(qs, kc_ref[0][:, ks])
            s_n = jnp.where(ok_next, _dot_nt(qs, kn_ref[0][:, ks]), NEG_INF)
            m_el = jnp.maximum(jnp.maximum(m_el, s_c), jnp.maximum(s_p, s_n))
        m = jnp.maximum(jnp.max(m_el, axis=-1, keepdims=True), sink)
        acc = _dot(jnp.exp(s_x - m).astype(BF16), with_ones(vx_ref[0], ks))
        if window:
            for s_w, v_ref in ((s_p, vp_ref), (s_c, vc_ref), (s_n, vn_ref)):
                acc = acc + _dot(jnp.exp(s_w - m).astype(BF16), with_ones(v_ref[0], ks))
        o = acc[:, 0:HEAD_DIM] / (acc[:, HEAD_DIM:] + jnp.exp(sink - m))
        outs += [o[i * tq:(i + 1) * tq] for i in range(Q_PER_KV)]
    o_ref[0] = jnp.concatenate(outs, axis=-1).astype(BF16)


def _attention(q, k, v, k_ctx, v_ctx, sink_rep, *, window):
    bsz, seq, _ = q.shape
    lc = k_ctx.shape[1]
    ctx_spec = pl.BlockSpec((1, lc, KV_WIDTH), lambda b, i: (b, 0, 0))
    sink_spec = pl.BlockSpec((N_Q_HEADS, LANES), lambda b, i: (0, 0))
    if window:
        tq = BLOCK
        nb = seq // tq
        prev = pl.BlockSpec((1, tq, KV_WIDTH), lambda b, i: (b, jnp.maximum(i - 1, 0), 0))
        cur = pl.BlockSpec((1, tq, KV_WIDTH), lambda b, i: (b, i, 0))
        nxt = pl.BlockSpec((1, tq, KV_WIDTH), lambda b, i: (b, jnp.minimum(i + 1, nb - 1), 0))
        in_specs = [pl.BlockSpec((1, tq, ATTN_WIDTH), lambda b, i: (b, i, 0)),
                    prev, cur, nxt, prev, cur, nxt, ctx_spec, ctx_spec, sink_spec]
        args = (q, k, k, k, v, v, v, k_ctx, v_ctx, sink_rep)
    else:
        tq = seq
        nb = 1
        in_specs = [pl.BlockSpec((1, tq, ATTN_WIDTH), lambda b, i: (b, i, 0)), ctx_spec, ctx_spec, sink_spec]
        args = (q, k_ctx, v_ctx, sink_rep)
    return pl.pallas_call(
        functools.partial(_attn_kernel, window=window),
        grid=(bsz, nb),
        in_specs=in_specs,
        out_specs=pl.BlockSpec((1, tq, ATTN_WIDTH), lambda b, i: (b, i, 0)),
        out_shape=jax.ShapeDtypeStruct((bsz, seq, ATTN_WIDTH), BF16),
        compiler_params=_cparams("arbitrary", "arbitrary"),
        name="window_attention" if window else "context_attention",
    )(*args)


def _s5_kernel(uf_ref, ub_ref, h0_ref, mask_ref, bst_ref, a_ref, cw_ref,
               yf_ref, yb_ref, hfin_ref, lhs_scr, bu_scr, hs_scr, h_scr, *, bsz, tc):
    i = pl.program_id(0)

    @pl.when(i == 0)
    def _():
        h_scr[...] = h0_ref[...]

    mask = mask_ref[...]
    n_chain = 2 * bsz

    for d, u_ref in enumerate((uf_ref, ub_ref)):
        for b in range(bsz):
            for j in range(tc // 2):
                pair = [jnp.broadcast_to(u_ref[b, 2 * j + k:2 * j + k + 1, :], (SUBLANES, S5_WIDTH)) * mask
                        for k in range(2)]
                lhs_scr[d * bsz + b, 2 * SUBLANES * j:2 * SUBLANES * (j + 1), :] = (
                    jnp.concatenate(pair, axis=0).astype(BF16))
    for c in range(n_chain):
        bu_scr[c] = _dot(lhs_scr[c], bst_ref[c // bsz])

    a_re = [a_ref[d, 0] for d in range(2)]
    a_im = [a_ref[d, 1] for d in range(2)]

    def step(t, carry):
        new = []
        for c in range(n_chain):
            d = c // bsz
            tt = t if d == 0 else tc - 1 - t
            r0 = pl.multiple_of(tt * SUBLANES, SUBLANES)
            h_re, h_im = carry[2 * c], carry[2 * c + 1]
            n_re = a_re[d] * h_re - a_im[d] * h_im + bu_scr[c, pl.ds(r0, SUBLANES), 0:LANES]
            n_im = a_re[d] * h_im + a_im[d] * h_re + bu_scr[c, pl.ds(r0, SUBLANES), LANES:2 * LANES]
            hs_scr[c, 0, pl.ds(r0, SUBLANES), :] = n_re
            hs_scr[c, 1, pl.ds(r0, SUBLANES), :] = n_im
            new += [n_re, n_im]
        return tuple(new)

    init = []
    for c in range(n_chain):
        init += [h_scr[c, :, 0:LANES], h_scr[c, :, LANES:2 * LANES]]
    fin = lax.fori_loop(0, tc, step, tuple(init), unroll=8)
    for c in range(n_chain):
        h_scr[c, :, 0:LANES] = fin[2 * c]
        h_scr[c, :, LANES:2 * LANES] = fin[2 * c + 1]
    hfin_ref[...] = h_scr[...]

    for d, y_ref in enumerate((yf_ref, yb_ref)):
        for b in range(bsz):
            c = d * bsz + b
            parts = [hs_scr[c, ri, pl.ds(s, tc, stride=SUBLANES), :].astype(BF16)
                     for s in range(SUBLANES) for ri in range(2)]
            y_ref[b] = _dot(jnp.concatenate(parts, axis=-1), cw_ref[d])


def _s5_scan(u, h0, mask, bst, a_tiles, cwide, *, tc):
    bsz, seq, _ = u.shape
    nch = seq // tc
    full = lambda shape: pl.BlockSpec(shape, lambda i: (0,) * len(shape))
    fwd = pl.BlockSpec((bsz, tc, S5_WIDTH), lambda i: (0, i, 0))
    bwd = pl.BlockSpec((bsz, tc, S5_WIDTH), lambda i: (0, nch - 1 - i, 0))
    state = (2 * bsz, SUBLANES, 2 * LANES)
    rows = SUBLANES * tc
    return pl.pallas_call(
        functools.partial(_s5_kernel, bsz=bsz, tc=tc),
        grid=(nch,),
        in_specs=[fwd, bwd, full(state), full(mask.shape), full(bst.shape), full(a_tiles.shape),
                  full(cwide.shape)],
        out_specs=[fwd, bwd, full(state)],
        out_shape=[jax.ShapeDtypeStruct(u.shape, F32), jax.ShapeDtypeStruct(u.shape, F32),
                   jax.ShapeDtypeStruct(state, F32)],
        scratch_shapes=[pltpu.VMEM((2 * bsz, rows, S5_WIDTH), BF16),
                        pltpu.VMEM((2 * bsz, rows, 2 * LANES), F32),
                        pltpu.VMEM((2 * bsz, 2, rows, LANES), F32),
                        pltpu.VMEM(state, F32)],
        compiler_params=_cparams("arbitrary"),
        name="s5_scan",
    )(u, u, h0, mask, bst, a_tiles, cwide)


def _s5_mask():
    m = np.arange(S5_WIDTH)[None, :] // CH_PER_CHUNK == np.arange(SUBLANES)[:, None]
    return jnp.asarray(m.astype(np.float32), F32)


def _s5_params(lam_re, lam_im, log_dt, b_re, b_im, c_re, c_im):
    dt = jnp.exp(log_dt)[..., None]
    mag = jnp.exp(lam_re * dt)
    l_re = mag * jnp.cos(lam_im * dt)
    l_im = mag * jnp.sin(lam_im * dt)
    den = lam_re * lam_re + lam_im * lam_im
    f_re = ((l_re - 1.0) * lam_re + l_im * lam_im) / den
    f_im = (l_im * lam_re - (l_re - 1.0) * lam_im) / den
    bb_re = f_re[..., None] * b_re - f_im[..., None] * b_im
    bb_im = f_re[..., None] * b_im + f_im[..., None] * b_re
    a_tiles = jnp.stack([l_re.reshape(2, SUBLANES, LANES), l_im.reshape(2, SUBLANES, LANES)], axis=1)

    half = S5_GROUPS // SUBLANES
    eye = jnp.eye(half, dtype=F32)

    def in_mat(bb):
        t = bb.reshape(2, SUBLANES, half, S5_STATE, S5_GROUP)
        m = jnp.einsum('dsgpc,gh->dsgchp', t, eye)
        return m.reshape(2, S5_WIDTH, half * S5_STATE)

    def out_mat(cc):
        t = cc.reshape(2, SUBLANES, half, S5_GROUP, S5_STATE)
        m = jnp.einsum('dsgcp,st,gh->dsgpthc', t, jnp.eye(SUBLANES, dtype=F32), eye)
        return m.reshape(2, SUBLANES, half * S5_STATE, S5_WIDTH)

    bst = jnp.concatenate([in_mat(bb_re), in_mat(bb_im)], axis=-1).astype(BF16)
    cwide = jnp.stack([out_mat(c_re), out_mat(-c_im)], axis=2)
    cwide = cwide.reshape(2, 2 * SUBLANES * LANES, S5_WIDTH).astype(BF16)
    return bst, a_tiles, cwide


def _conv_kernel(prev_ref, cur_ref, next_ref, wdw_ref, bdw_ref, g_ref, b_ref, wpw_ref, bpw_ref,
                 o_ref, win_ref, *, tm):
    i = pl.program_id(1)
    nt = pl.num_programs(1)
    zero = jnp.zeros((CONV_HALO, CONV_WIDTH), F32)
    win_ref[0:CONV_HALO] = jnp.where(i > 0, prev_ref[0], zero)
    win_ref[CONV_HALO:CONV_HALO + tm] = cur_ref[0]
    win_ref[CONV_HALO + tm:] = jnp.where(i < nt - 1, next_ref[0], zero)
    acc = jnp.zeros((tm, CONV_WIDTH), F32) + bdw_ref[...]
    for r in range(SUBLANES):
        z = win_ref[0:tm + SUBLANES, :] * wdw_ref[r:r + 1, :]
        for a in range(1, (CONV_K + 1) // SUBLANES):
            j = a * SUBLANES
            z = z + win_ref[j:j + tm + SUBLANES, :] * wdw_ref[j + r:j + r + 1, :]
        acc = acc + z[r:r + tm]
    h = _silu(_layer_norm(acc) * g_ref[...] + b_ref[...])
    o_ref[0] = (_dot(h.astype(BF16), wpw_ref[...]) + bpw_ref[...]).astype(BF16)


def _conformer_conv(cg, w_dw, b_dw, ln_g, ln_b, w_pw_bf, b_pw, *, tm):
    bsz, seq, _ = cg.shape
    hb = tm // CONV_HALO
    last = seq // CONV_HALO - 1
    row = lambda a: a.reshape(1, CONV_WIDTH)
    vec = pl.BlockSpec((1, CONV_WIDTH), lambda b, i: (0, 0))
    return pl.pallas_call(
        functools.partial(_conv_kernel, tm=tm),
        grid=(bsz, seq // tm),
        in_specs=[
            pl.BlockSpec((1, CONV_HALO, CONV_WIDTH), lambda b, i: (b, jnp.maximum(i * hb - 1, 0), 0)),
            pl.BlockSpec((1, tm, CONV_WIDTH), lambda b, i: (b, i, 0)),
            pl.BlockSpec((1, CONV_HALO, CONV_WIDTH), lambda b, i: (b, jnp.minimum((i + 1) * hb, last), 0)),
            pl.BlockSpec((CONV_K + 1, CONV_WIDTH), lambda b, i: (0, 0)),
            vec, vec, vec,
            pl.BlockSpec((CONV_WIDTH, CONV_WIDTH), lambda b, i: (0, 0)),
            vec,
        ],
        out_specs=pl.BlockSpec((1, tm, CONV_WIDTH), lambda b, i: (b, i, 0)),
        out_shape=jax.ShapeDtypeStruct((bsz, seq, CONV_WIDTH), BF16),
        scratch_shapes=[pltpu.VMEM((tm + 2 * CONV_HALO, CONV_WIDTH), F32)],
        compiler_params=_cparams("arbitrary", "arbitrary"),
        name="conformer_conv",
    )(cg, cg, cg, jnp.pad(w_dw.reshape(CONV_K, CONV_WIDTH), ((1, 0), (0, 0))), row(b_dw), row(ln_g),
      row(ln_b), w_pw_bf, row(b_pw))


MIX_CHUNK = 256


def _mixout_kernel(attn_ref, yf_ref, yb_ref, u_ref, conv_ref, x_ref,
                   dskip_ref, wglu_ref, bglu_ref, wout_ref, bout_ref,
                   gate_ref, g1_ref, b1_ref, shift_ref, scale_ref, wr_hi_ref, wr_lo_ref,
                   xmid_ref, h2_ref, logit_ref):
    tm = x_ref.shape[1]
    chunk = min(tm, MIX_CHUNK)
    for r0 in range(0, tm, chunk):
        rows = slice(r0, r0 + chunk)
        y = dskip_ref[...] * u_ref[0, rows, :] + yf_ref[0, rows, :] + yb_ref[0, rows, :]
        z = _gelu_tanh(y)
        s5 = z * _sigmoid(_dot(z.astype(BF16), wglu_ref[...]) + bglu_ref[...])
        y_mix = (_dot(attn_ref[0, rows, :], wout_ref[0:ATTN_WIDTH, :])
                 + _dot(s5.astype(BF16), wout_ref[ATTN_WIDTH:ATTN_WIDTH + S5_WIDTH, :])
                 + _dot(conv_ref[0, rows, :], wout_ref[ATTN_WIDTH + S5_WIDTH:, :])
                 + bout_ref[...])
        r = DEEPNORM_ALPHA * x_ref[0, rows, :] + gate_ref[0] * y_mix
        x_mid = _layer_norm(r) * g1_ref[...] + b1_ref[...]
        xmid_ref[0, rows, :] = x_mid
        h2 = _layer_norm(x_mid) * (1.0 + scale_ref[0]) + shift_ref[0]
        h2_ref[0, rows, :] = h2.astype(BF16)
        logit_ref[0, rows, :] = _dot3(h2, wr_hi_ref[...], wr_lo_ref[...])


def _mixer_output(attn, yf, yb, u, conv, x, d_skip, w_glu_bf, b_glu, w_out_bf, b_out,
                  gate1, ln_g, ln_b, shift2, scale2, wr_hi, wr_lo, *, tm):
    bsz, seq, _ = x.shape
    tok = lambda w: pl.BlockSpec((1, tm, w), lambda b, i: (b, i, 0))
    full = lambda r, c: pl.BlockSpec((r, c), lambda b, i: (0, 0))
    bvec = pl.BlockSpec((1, 1, D_MODEL), lambda b, i: (b, 0, 0))
    row = lambda a: a.reshape(1, -1)
    return pl.pallas_call(
        _mixout_kernel,
        grid=(bsz, seq // tm),
        in_specs=[
            tok(ATTN_WIDTH), tok(S5_WIDTH), tok(S5_WIDTH), tok(S5_WIDTH), tok(CONV_WIDTH), tok(D_MODEL),
            full(1, S5_WIDTH), full(S5_WIDTH, S5_WIDTH), full(1, S5_WIDTH),
            full(D_MODEL, D_MODEL), full(1, D_MODEL),
            bvec, full(1, D_MODEL), full(1, D_MODEL), bvec, bvec,
            full(D_MODEL, LANES), full(D_MODEL, LANES),
        ],
        out_specs=[tok(D_MODEL), tok(D_MODEL), tok(LANES)],
        out_shape=[
            jax.ShapeDtypeStruct((bsz, seq, D_MODEL), F32),
            jax.ShapeDtypeStruct((bsz, seq, D_MODEL), BF16),
            jax.ShapeDtypeStruct((bsz, seq, LANES), F32),
        ],
        compiler_params=_cparams("arbitrary", "arbitrary"),
        name="mixer_output",
    )(attn, yf, yb, u, conv, x, row(d_skip), w_glu_bf, row(b_glu), w_out_bf, row(b_out),
      gate1, row(ln_g), row(ln_b), shift2, scale2, wr_hi, wr_lo)


def _token_cumsum(m, tri, ones):
    nblk = m.shape[0]
    m2 = m.reshape(nblk * N_EXPERTS, LANES).astype(BF16)
    within = _dot(m2, tri).reshape(nblk, N_EXPERTS, LANES)
    tot = _dot(m2, ones).reshape(nblk, N_EXPERTS, LANES)
    offs = []
    run = jnp.zeros((N_EXPERTS, LANES), F32)
    for j in range(nblk):
        offs.append(run)
        run = run + tot[j]
    off = jnp.stack(offs, axis=0)
    return within + off, off, tot


def _router_kernel(logit_ref, tri_ref, ones_ref, aff_ref, sel_ref, pos_ref, off_ref, cnt_ref, *, cap, slot0):
    b = pl.program_id(0)
    nblk = aff_ref.shape[1]

    def soft(j, carry):
        r0 = pl.multiple_of(j * LANES, LANES)
        t = logit_ref[0, pl.ds(r0, LANES), :].T[0:N_EXPERTS]
        ex = jnp.exp(t - jnp.max(t, axis=0, keepdims=True))
        aff_ref[0, j] = ex / jnp.sum(ex, axis=0, keepdims=True)
        return carry

    lax.fori_loop(0, nblk, soft, 0)
    aff = aff_ref[0]

    def enough(cand):
        cnt = jnp.sum(jnp.where(aff >= cand[None], 1.0, 0.0), axis=0)
        return jnp.sum(cnt, axis=-1, keepdims=True) >= cap

    p = jnp.full((N_EXPERTS, LANES), 2.0, F32)
    for k in range(6, -1, -1):
        cand = p * (2.0 ** -(2 ** k))
        p = jnp.where(enough(cand), p, cand)
    thr = 0.5 * p
    thr = jnp.where(enough(thr), thr, 0.0)

    def refine(_, carry):
        lo, step = carry
        cand = lo + step
        return jnp.where(enough(cand), cand, lo), 0.5 * step

    thr, _ = lax.fori_loop(0, 23, refine, (thr, 0.5 * thr))
    gt = aff > thr[None]
    eq = aff == thr[None]
    n_gt = jnp.sum(jnp.sum(jnp.where(gt, 1.0, 0.0), axis=0), axis=-1, keepdims=True)
    need = cap - n_gt
    tri = tri_ref[...]
    ones = ones_ref[...]
    cum_eq, _, _ = _token_cumsum(jnp.where(eq, 1.0, 0.0), tri, ones)
    sel = jnp.where(gt | (eq & (cum_eq <= need[None])), 1.0, 0.0)
    cum_sel, off, cnt = _token_cumsum(sel, tri, ones)
    base = (slot0 + b * cap).astype(F32)
    sel_ref[0] = sel
    pos_ref[0] = cum_sel - sel + base
    off_ref[0] = off + base
    cnt_ref[0] = cnt


def _router(logits, *, slot0):
    bsz, n, _ = logits.shape
    nblk = n // LANES
    cap = CAPACITY_FACTOR * n // N_EXPERTS
    idx = np.arange(LANES)
    tri = jnp.asarray((idx[:, None] <= idx[None, :]).astype(np.float32), BF16)
    ones = jnp.ones((LANES, LANES), BF16)
    shape = (bsz, nblk, N_EXPERTS, LANES)
    out = pl.BlockSpec((1, nblk, N_EXPERTS, LANES), lambda b: (b, 0, 0, 0))
    sq = pl.BlockSpec((LANES, LANES), lambda b: (0, 0))
    return pl.pallas_call(
        functools.partial(_router_kernel, cap=cap, slot0=slot0),
        grid=(bsz,),
        in_specs=[pl.BlockSpec((1, n, LANES), lambda b: (b, 0, 0)), sq, sq],
        out_specs=[out] * 5,
        out_shape=[jax.ShapeDtypeStruct(shape, F32)] * 5,
        compiler_params=_cparams("arbitrary"),
        name="router",
    )(logits, tri, ones)


COMBINE_KW = LANES + 16
SMALL_COUNT = 32
COMBINE_SMALL_KW = SMALL_COUNT + 16
FF_TILE = 256


def _dispatch_blocks(tb_ref, n_entries, e, blocks, xs_buf, g_buf):
    most = tb_ref[n_entries + blocks[0][0] * N_EXPERTS + e]
    for tile, _ in blocks[1:]:
        most = jnp.maximum(most, tb_ref[n_entries + tile * N_EXPERTS + e])

    def run(kw):
        k = lax.broadcasted_iota(jnp.int32, (kw, LANES), 0).astype(F32)
        for tile, load in blocks:
            h2_blk, pos_row, sel_row, aff_row = load()
            base = tb_ref[tile * N_EXPERTS + e]
            base16 = pl.multiple_of(lax.shift_left(lax.shift_right_logical(base, 4), 4), 16)
            hit = (k == (pos_row - base16.astype(F32))) & (sel_row > 0.5)
            sel_t = jnp.where(hit, 1.0, 0.0).astype(BF16)
            xs_buf[pl.ds(base16, kw), :] += _dot(sel_t, h2_blk).astype(BF16)
            g = jnp.sum(jnp.where(hit, aff_row, 0.0), axis=-1, keepdims=True)
            g_buf[pl.ds(base16, kw), :] += jnp.broadcast_to(g, (kw, LANES))

    @pl.when(most <= SMALL_COUNT)
    def _():
        run(COMBINE_SMALL_KW)

    @pl.when(most > SMALL_COUNT)
    def _():
        run(COMBINE_KW)


def _moe_kernel(*refs, n_ctx, rows, n_entries):
    tb_ref = refs[0]
    if n_ctx:
        (posx_ref, selx_ref, affx_ref, h2x_ref, posc_ref, selc_ref, affc_ref, h2c_ref,
         wg_ref, wu_ref, wd_ref, y_ref, acc_scr, g_scr, xs_scr) = refs[1:]
    else:
        (posx_ref, selx_ref, affx_ref, h2x_ref, wg_ref, wu_ref, wd_ref, y_ref,
         acc_scr, g_scr, xs_scr) = refs[1:]
    p = pl.program_id(0)
    s = pl.program_id(1)
    last = pl.num_programs(1) - 1
    fill = lax.rem(p, 2)
    use = 1 - fill
    blk_per_step = posx_ref.shape[2]

    @pl.when(p < N_EXPERTS)
    def _():
        xs_buf = xs_scr.at[fill]
        g_buf = g_scr.at[fill]

        @pl.when(s == 0)
        def _():
            xs_buf[...] = jnp.zeros(xs_buf.shape, BF16)
            g_buf[...] = jnp.zeros(g_buf.shape, F32)

        if n_ctx:
            @pl.when(s == 0)
            def _():
                bsz_c, blk_c = posc_ref.shape[0], posc_ref.shape[2]

                def load_c(b, i):
                    return lambda: (h2c_ref[b, i * LANES:(i + 1) * LANES, :], posc_ref[b, 0, i:i + 1, :],
                                    selc_ref[b, 0, i:i + 1, :], affc_ref[b, 0, i:i + 1, :])

                blocks = [((last + 1) * blk_per_step + b * blk_c + i, load_c(b, i))
                          for b in range(bsz_c) for i in range(blk_c)]
                _dispatch_blocks(tb_ref, n_entries, p, blocks, xs_buf, g_buf)

        def load_x(i):
            return lambda: (h2x_ref[0, i * LANES:(i + 1) * LANES, :], posx_ref[0, 0, i:i + 1, :],
                            selx_ref[0, 0, i:i + 1, :], affx_ref[0, 0, i:i + 1, :])

        blocks = [(s * blk_per_step + i, load_x(i)) for i in range(blk_per_step)]
        _dispatch_blocks(tb_ref, n_entries, p, blocks, xs_buf, g_buf)

    @pl.when(p >= 1)
    def _():
        wg = wg_ref[0, 0].astype(BF16)
        wu = wu_ref[0, 0].astype(BF16)
        wd = wd_ref[0, 0].astype(BF16)
        half = rows // 2
        for r0 in (0, half):
            xs = xs_scr[use, r0:r0 + half, :]
            hid = _silu(_dot(xs, wg)) * _dot(xs, wu)
            part = _dot(hid.astype(BF16), wd)
            acc_scr[r0:r0 + half, :] = jnp.where(s == 0, part, acc_scr[r0:r0 + half, :] + part)

        @pl.when(s == last)
        def _():
            g = g_scr[use, 0:rows, :]
            gated = acc_scr[...] * jnp.concatenate([g] * (D_MODEL // LANES), axis=-1)
            y_ref[0, 0:rows, :] = gated.astype(BF16)
            y_ref[0, rows:, :] = jnp.zeros((y_ref.shape[1] - rows, D_MODEL), BF16)


def _moe_experts(tb, route_x, h2x, route_c, h2c, w_gate, w_up, w_down, *, layer):
    bsz, n, d = h2x.shape
    n_ff = EXPERT_FF // FF_TILE
    x_step = bsz * n // n_ff
    steps_per_b = n // x_step
    blk_per_step = x_step // LANES
    n_ctx = 0 if h2c is None else h2c.shape[0] * h2c.shape[1]
    rows = bsz * CAPACITY_FACTOR * n // N_EXPERTS + CAPACITY_FACTOR * n_ctx // N_EXPERTS
    rows_pad = rows + COMBINE_KW
    last_e = N_EXPERTS - 1

    def xs_idx(p, s):
        sc = jnp.where(p > last_e, n_ff - 1, s)
        return sc // steps_per_b, sc % steps_per_b

    def route_x_spec():
        return pl.BlockSpec((1, 1, blk_per_step, LANES),
                            lambda p, s, tb: (xs_idx(p, s)[0], jnp.minimum(p, last_e), xs_idx(p, s)[1], 0))

    in_specs = [route_x_spec(), route_x_spec(), route_x_spec(),
                pl.BlockSpec((1, x_step, d), lambda p, s, tb: (xs_idx(p, s)[0], xs_idx(p, s)[1], 0))]
    args = list(route_x) + [h2x]
    if n_ctx:
        bc, nc, _ = h2c.shape
        rc = pl.BlockSpec((bc, 1, nc // LANES, LANES), lambda p, s, tb: (0, jnp.minimum(p, last_e), 0, 0))
        in_specs += [rc, rc, rc, pl.BlockSpec((bc, nc, d), lambda p, s, tb: (0, 0, 0))]
        args += list(route_c) + [h2c]
    ffn_e = lambda p: jnp.maximum(p - 1, 0)
    in_specs += [pl.BlockSpec((1, 1, d, FF_TILE), lambda p, s, tb: (layer, ffn_e(p), 0, s)),
                 pl.BlockSpec((1, 1, d, FF_TILE), lambda p, s, tb: (layer, ffn_e(p), 0, s)),
                 pl.BlockSpec((1, 1, FF_TILE, d), lambda p, s, tb: (layer, ffn_e(p), s, 0))]
    args += [w_gate, w_up, w_down]
    return pl.pallas_call(
        functools.partial(_moe_kernel, n_ctx=n_ctx, rows=rows, n_entries=tb.shape[0] // 2),
        grid_spec=pltpu.PrefetchScalarGridSpec(
            num_scalar_prefetch=1,
            grid=(N_EXPERTS + 1, n_ff),
            in_specs=in_specs,
            out_specs=pl.BlockSpec((1, rows_pad, d), lambda p, s, tb: (ffn_e(p), 0, 0)),
            scratch_shapes=[pltpu.VMEM((rows, d), F32), pltpu.VMEM((2, rows_pad, LANES), F32),
                            pltpu.VMEM((2, rows_pad, d), BF16)],
        ),
        out_shape=jax.ShapeDtypeStruct((N_EXPERTS, rows_pad, d), BF16),
        compiler_params=_cparams("arbitrary", "arbitrary"),
        name="moe_experts",
    )(tb, *args)


def _combine_kernel(tb_ref, pos_ref, sel_ref, xmid_ref, gate_ref, g_ref, b_ref, y_hbm, o_ref,
                    y_scr, stack_scr, sem, *, cap, slot0, blk_per_step, nblk, n_entries):
    b = pl.program_id(0)
    j = pl.program_id(1)
    win = y_scr.shape[1]

    @pl.when(j == 0)
    def _():
        row0 = pl.multiple_of(slot0 + b * cap, 16)
        cp = pltpu.make_async_copy(y_hbm.at[:, pl.ds(row0, win), :], y_scr, sem)
        cp.start()
        cp.wait()

    set_base = (slot0 + b * cap).astype(F32)
    tn_dims = (((0,), (0,)), ((), ()))
    for i in range(blk_per_step):
        entry = (b * nblk + j * blk_per_step + i) * N_EXPERTS
        rows = slice(i * LANES, (i + 1) * LANES)

        def hits(e, kw, entry=entry, i=i):
            base = tb_ref[entry + e] - (slot0 + b * cap)
            base16 = pl.multiple_of(lax.shift_left(lax.shift_right_logical(base, 4), 4), 16)
            k = lax.broadcasted_iota(jnp.int32, (kw, LANES), 0).astype(F32)
            rel = pos_ref[0, i, e:e + 1, :] - (set_base + base16.astype(F32))
            hit = (k == rel) & (sel_ref[0, i, e:e + 1, :] > 0.5)
            return base16, jnp.where(hit, 1.0, 0.0).astype(BF16)

        def finish(moe, rows=rows):
            r = DEEPNORM_ALPHA * xmid_ref[0, rows, :] + gate_ref[0] * moe
            o_ref[0, rows, :] = _layer_norm(r) * g_ref[...] + b_ref[...]

        most = tb_ref[n_entries + entry]
        for e in range(1, N_EXPERTS):
            most = jnp.maximum(most, tb_ref[n_entries + entry + e])

        @pl.when(most <= SMALL_COUNT)
        def _():
            sel_all = []
            for e in range(N_EXPERTS):
                base16, sel_t = hits(e, COMBINE_SMALL_KW)
                sel_all.append(sel_t)
                stack_scr[e * COMBINE_SMALL_KW:(e + 1) * COMBINE_SMALL_KW, :] = (
                    y_scr[e, pl.ds(base16, COMBINE_SMALL_KW), :])
            finish(lax.dot_general(jnp.concatenate(sel_all, axis=0), stack_scr[...], tn_dims,
                                   preferred_element_type=F32))

        @pl.when(most > SMALL_COUNT)
        def _():
            acc = jnp.zeros((LANES, D_MODEL), F32)
            for e in range(N_EXPERTS):
                base16, sel_t = hits(e, COMBINE_KW)
                acc = acc + lax.dot_general(sel_t, y_scr[e, pl.ds(base16, COMBINE_KW), :], tn_dims,
                                            preferred_element_type=F32)
            finish(acc)


def _combine_post(tb, pos, sel, x_mid, gate2, ln_g, ln_b, y, *, slot0, tm):
    bsz, n, d = x_mid.shape
    nblk = n // LANES
    cap = CAPACITY_FACTOR * n // N_EXPERTS
    blk_per_step = tm // LANES
    tok = pl.BlockSpec((1, tm, d), lambda b, j, tb: (b, j, 0))
    route = pl.BlockSpec((1, blk_per_step, N_EXPERTS, LANES), lambda b, j, tb: (b, j, 0, 0))
    vec = pl.BlockSpec((1, d), lambda b, j, tb: (0, 0))
    return pl.pallas_call(
        functools.partial(_combine_kernel, cap=cap, slot0=slot0, blk_per_step=blk_per_step, nblk=nblk,
                          n_entries=bsz * nblk * N_EXPERTS),
        grid_spec=pltpu.PrefetchScalarGridSpec(
            num_scalar_prefetch=1,
            grid=(bsz, n // tm),
            in_specs=[route, route, tok, pl.BlockSpec((1, 1, d), lambda b, j, tb: (b, 0, 0)), vec, vec,
                      pl.BlockSpec(memory_space=pl.ANY)],
            out_specs=tok,
            scratch_shapes=[pltpu.VMEM((N_EXPERTS, cap + COMBINE_KW, d), BF16),
                            pltpu.VMEM((N_EXPERTS * COMBINE_SMALL_KW, d), BF16),
                            pltpu.SemaphoreType.DMA(())],
        ),
        out_shape=jax.ShapeDtypeStruct(x_mid.shape, F32),
        compiler_params=_cparams("arbitrary", "arbitrary"),
        name="combine_post",
    )(tb, pos, sel, x_mid, gate2, ln_g.reshape(1, d), ln_b.reshape(1, d), y)


def _tile_table(*lane_replicated):
    return jnp.concatenate([a[..., 0].astype(jnp.int32).reshape(-1) for a in lane_replicated])


def _expert_major(a):
    return jnp.swapaxes(a, 1, 2)


def kernel(x, c, ctx, c_ctx, w_mod, b_mod, w_in, b_in, attn_sink, s5_lam_re, s5_lam_im, s5_log_dt,
           s5_b_re, s5_b_im, s5_c_re, s5_c_im, s5_d, s5_w_glu, s5_b_glu, conv_w_dw, conv_b_dw,
           conv_ln_g, conv_ln_b, conv_w_pw, conv_b_pw, w_out, b_out, ln1_g, ln1_b, w_router,
           exp_w_gate, exp_w_up, exp_w_down, ln2_g, ln2_b):
    bsz, seq, d = x.shape
    lc = ctx.shape[1]
    tm_x, tm_c = 512, lc

    cond = jnp.zeros((SUBLANES, d), F32).at[:bsz].set(c).at[bsz].set(c_ctx)
    mods = _modulation(cond, w_mod, b_mod)
    cos_t, sin_t = _rope_tables(seq)
    s5_mask = _s5_mask()
    zero_state = jnp.zeros((2 * bsz, SUBLANES, 2 * LANES), F32)

    xc = ctx
    for l in range(DEPTH):
        last = l == DEPTH - 1
        mod_x = [mods[l, :bsz, k * d:(k + 1) * d].reshape(bsz, 1, d) for k in range(6)]
        mod_c = [jnp.broadcast_to(mods[l, bsz, k * d:(k + 1) * d].reshape(1, 1, d), (bsz, 1, d))
                 for k in range(6)]
        w_in_bf = w_in[l].astype(BF16)
        w_out_bf = w_out[l].astype(BF16)
        w_glu_bf = s5_w_glu[l].astype(BF16)
        w_pw_bf = conv_w_pw[l].astype(BF16)
        wr = jnp.pad(w_router[l], ((0, 0), (0, LANES - N_EXPERTS)))
        wr_hi, wr_lo = _split_bf16(wr)
        sink_rep = jnp.broadcast_to(attn_sink[l][:, None], (N_Q_HEADS, LANES))
        bst, a_tiles, cwide = _s5_params(s5_lam_re[l], s5_lam_im[l], s5_log_dt[l], s5_b_re[l], s5_b_im[l],
                                       s5_c_re[l], s5_c_im[l])

        q, k, v, u, cg = _in_projection(x, mod_x[0], mod_x[1], w_in_bf, b_in[l], cos_t, sin_t,
                                        rope=True, tm=tm_x)
        q_c, k_c, v_c, u_c, cg_c = _in_projection(xc, mod_c[0], mod_c[1], w_in_bf, b_in[l],
                                                  cos_t[:lc], sin_t[:lc], rope=False, tm=tm_c)

        attn_x = _attention(q, k, v, k_c, v_c, sink_rep, window=True)
        yf_c, yb_c, h_ctx = _s5_scan(u_c, zero_state, s5_mask, bst, a_tiles, cwide, tc=lc)
        yf, yb, _ = _s5_scan(u, h_ctx, s5_mask, bst, a_tiles, cwide, tc=256)
        conv_args = (conv_w_dw[l], conv_b_dw[l], conv_ln_g[l], conv_ln_b[l], w_pw_bf, conv_b_pw[l])
        conv_x = _conformer_conv(cg, *conv_args, tm=256)
        mix_args = (s5_d[l], w_glu_bf, s5_b_glu[l], w_out_bf, b_out[l])
        x_mid, h2, logits = _mixer_output(attn_x, yf, yb, u, conv_x, x, *mix_args, mod_x[2], ln1_g[l],
                                          ln1_b[l], mod_x[3], mod_x[4], wr_hi, wr_lo, tm=tm_x)
        aff_x, sel_x, pos_x, off_x, cnt_x = _router(logits, slot0=0)
        route_x = tuple(_expert_major(a) for a in (pos_x, sel_x, aff_x))
        tb_x = _tile_table(off_x, cnt_x)
        if not last:
            attn_c = _attention(q_c, None, None, k_c, v_c, sink_rep, window=False)
            conv_c = _conformer_conv(cg_c, *conv_args, tm=lc)
            xc_mid, hc2, logits_c = _mixer_output(attn_c, yf_c, yb_c, u_c, conv_c, xc, *mix_args, mod_c[2],
                                                  ln1_g[l], ln1_b[l], mod_c[3], mod_c[4], wr_hi, wr_lo,
                                                  tm=tm_c)
            slot0_c = bsz * CAPACITY_FACTOR * seq // N_EXPERTS
            aff_c, sel_c, pos_c, off_c, cnt_c = _router(logits_c, slot0=slot0_c)
            route_c = tuple(_expert_major(a) for a in (pos_c, sel_c, aff_c))
            tb_c = _tile_table(off_c, cnt_c)
            y = _moe_experts(_tile_table(off_x, off_c, cnt_x, cnt_c), route_x, h2, route_c, hc2,
                             exp_w_gate, exp_w_up, exp_w_down, layer=l)
            xc = _combine_post(tb_c, pos_c, sel_c, xc_mid, mod_c[5], ln2_g[l], ln2_b[l], y,
                               slot0=slot0_c, tm=lc)
        else:
            y = _moe_experts(tb_x, route_x, h2, None, None, exp_w_gate, exp_w_up, exp_w_down, layer=l)
        x = _combine_post(tb_x, pos_x, sel_x, x_mid, mod_x[5], ln2_g[l], ln2_b[l], y, slot0=0, tm=256)
    return x
```

```python
import functools
import math

import jax
import jax.numpy as jnp
import numpy as np
from jax import lax
from jax.experimental import pallas as pl
from jax.experimental.pallas import tpu as pltpu

D_MODEL = 1024
DEPTH = 2
GRID_W = 64
HEAD_DIM = 64
ATTN_WIDTH = D_MODEL // 2
N_Q_HEADS = ATTN_WIDTH // HEAD_DIM
N_KV_HEADS = N_Q_HEADS // 4
Q_PER_KV = N_Q_HEADS // N_KV_HEADS
KV_WIDTH = N_KV_HEADS * HEAD_DIM
BLOCK = 128
ROPE_BASE = 10000.0
NEG_INF = -1e30
S5_WIDTH = D_MODEL // 4
S5_GROUP = 16
S5_GROUPS = S5_WIDTH // S5_GROUP
S5_STATE = 64
CONV_WIDTH = D_MODEL - ATTN_WIDTH - S5_WIDTH
CONV_K = 31
CONV_HALO = 16
Q_END = ATTN_WIDTH
K_END = Q_END + KV_WIDTH
V_END = K_END + KV_WIDTH
U_END = V_END + S5_WIDTH
IN_WIDTH = U_END + 2 * CONV_WIDTH
N_EXPERTS = 16
EXPERT_FF = 2 * D_MODEL
CAPACITY_FACTOR = 2
DEEPNORM_ALPHA = (2.0 * DEPTH) ** 0.25
LN_EPS = 1e-5

LANES = 128
SUBLANES = 8
S5_CHUNKS = S5_WIDTH * S5_STATE // S5_GROUP // LANES
CH_PER_CHUNK = S5_WIDTH // S5_CHUNKS
VMEM_LIMIT = 56 * 1024 * 1024
MIX_CHUNK = 256

F32 = jnp.float32
BF16 = jnp.bfloat16


def _cparams(*sem):
    return pltpu.CompilerParams(dimension_semantics=sem, vmem_limit_bytes=VMEM_LIMIT)


def _dot(a, b):
    return jnp.dot(a, b, preferred_element_type=F32)


def _dot_nt(a, b):
    return lax.dot_general(a, b, (((1,), (1,)), ((), ())), preferred_element_type=F32)


def _split_bf16(x):
    hi = x.astype(BF16)
    lo = (x - hi.astype(F32)).astype(BF16)
    return hi, lo


def _dot3(a, b_hi, b_lo):
    a_hi, a_lo = _split_bf16(a)
    return _dot(a_hi, b_hi) + (_dot(a_lo, b_hi) + _dot(a_hi, b_lo))


def _sigmoid(x):
    return 1.0 / (1.0 + jnp.exp(-x))


def _silu(x):
    return x * _sigmoid(x)


def _gelu_tanh(x):
    c = math.sqrt(2.0 / math.pi)
    return 0.5 * x * (1.0 + jnp.tanh(c * (x + 0.044715 * (x * x * x))))


def _layer_norm(x):
    mu = jnp.mean(x, axis=-1, keepdims=True)
    xc = x - mu
    var = jnp.mean(xc * xc, axis=-1, keepdims=True)
    return xc * lax.rsqrt(var + LN_EPS)


def _mod_kernel(c_ref, w_ref, b_ref, o_ref):
    s = _silu(c_ref[...])
    w = w_ref[0]
    w_hi, w_lo = _split_bf16(w)
    o_ref[0] = _dot3(s, w_hi, w_lo) + b_ref[0]


def _modulation(cond, w_mod, b_mod):
    tn = 1536
    n = w_mod.shape[-1]
    return pl.pallas_call(
        _mod_kernel,
        grid=(DEPTH, n // tn),
        in_specs=[
            pl.BlockSpec((SUBLANES, D_MODEL), lambda l, j: (0, 0)),
            pl.BlockSpec((1, D_MODEL, tn), lambda l, j: (l, 0, j)),
            pl.BlockSpec((1, 1, tn), lambda l, j: (l, 0, j)),
        ],
        out_specs=pl.BlockSpec((1, SUBLANES, tn), lambda l, j: (l, 0, j)),
        out_shape=jax.ShapeDtypeStruct((DEPTH, SUBLANES, n), F32),
        compiler_params=_cparams("arbitrary", "arbitrary"),
        name="modulation",
    )(cond, w_mod, b_mod.reshape(DEPTH, 1, n))


def _rope_chunk(x, cos, sin_signed):
    lane = lax.broadcasted_iota(jnp.int32, x.shape, 1)
    first = (lane % 32) < 16
    partner = jnp.where(first, pltpu.roll(x, LANES - 16, 1), pltpu.roll(x, 16, 1))
    return x * cos + partner * sin_signed


def _inproj_kernel(x_ref, shift_ref, scale_ref, w_ref, b_ref, cos_ref, sin_ref,
                   q_ref, k_ref, v_ref, u_ref, cg_ref, *, rope):
    tm = x_ref.shape[1]
    chunk = min(tm, MIX_CHUNK)
    scale = HEAD_DIM ** -0.5
    for r0 in range(0, tm, chunk):
        rows = slice(r0, r0 + chunk)
        h = _layer_norm(x_ref[0, rows, :]) * (1.0 + scale_ref[0]) + shift_ref[0]
        p = _dot(h.astype(BF16), w_ref[...]) + b_ref[...]
        if rope:
            cos = cos_ref[rows, :]
            sin = sin_ref[rows, :]
        for j in range(ATTN_WIDTH // LANES):
            qc = p[:, j * LANES:(j + 1) * LANES]
            if rope:
                qc = _rope_chunk(qc, cos, sin)
            q_ref[0, rows, j * LANES:(j + 1) * LANES] = (qc * scale).astype(BF16)
        kc = p[:, Q_END:K_END]
        if rope:
            kc = _rope_chunk(kc, cos, sin)
        k_ref[0, rows, :] = kc.astype(BF16)
        v_ref[0, rows, :] = p[:, K_END:V_END].astype(BF16)
        u_ref[0, rows, :] = p[:, V_END:U_END]
        a = p[:, U_END:U_END + CONV_WIDTH]
        g = p[:, U_END + CONV_WIDTH:]
        cg_ref[0, rows, :] = a * _sigmoid(g)


def _in_projection(x, shift, scale, w_in_bf, b_in, cos_t, sin_t, *, rope, tm):
    bsz, seq, _ = x.shape
    tok = lambda w: pl.BlockSpec((1, tm, w), lambda b, i: (b, i, 0))
    vec = pl.BlockSpec((1, 1, D_MODEL), lambda b, i: (b, 0, 0))
    tab = pl.BlockSpec((tm, LANES), lambda b, i: (i, 0))
    return pl.pallas_call(
        functools.partial(_inproj_kernel, rope=rope),
        grid=(bsz, seq // tm),
        in_specs=[
            tok(D_MODEL), vec, vec,
            pl.BlockSpec((D_MODEL, IN_WIDTH), lambda b, i: (0, 0)),
            pl.BlockSpec((1, IN_WIDTH), lambda b, i: (0, 0)),
            tab, tab,
        ],
        out_specs=[tok(ATTN_WIDTH), tok(KV_WIDTH), tok(KV_WIDTH), tok(S5_WIDTH), tok(CONV_WIDTH)],
        out_shape=[
            jax.ShapeDtypeStruct((bsz, seq, ATTN_WIDTH), BF16),
            jax.ShapeDtypeStruct((bsz, seq, KV_WIDTH), BF16),
            jax.ShapeDtypeStruct((bsz, seq, KV_WIDTH), BF16),
            jax.ShapeDtypeStruct((bsz, seq, S5_WIDTH), F32),
            jax.ShapeDtypeStruct((bsz, seq, CONV_WIDTH), F32),
        ],
        compiler_params=_cparams("arbitrary", "arbitrary"),
        name="in_projection",
    )(x, shift, scale, w_in_bf, b_in.reshape(1, IN_WIDTH), cos_t, sin_t)


def _rope_tables(seq):
    lane = np.arange(LANES)
    i = lane % HEAD_DIM
    use_row = i < HEAD_DIM // 2
    f = HEAD_DIM // 4
    inv_freq = jnp.asarray(ROPE_BASE, F32) ** (-jnp.asarray(i % f, F32) / f)
    t = jnp.arange(seq, dtype=jnp.int32)
    pos = jnp.where(use_row[None, :], (t // GRID_W)[:, None], (t % GRID_W)[:, None]).astype(F32)
    ang = pos * inv_freq[None, :]
    sign = jnp.where((lane % (2 * f)) < f, -1.0, 1.0).astype(F32)
    return jnp.cos(ang), jnp.sin(ang) * sign[None, :]


ATTN_SUBBLOCKS = 2


def _attn_kernel(*refs, window):
    if window:
        n_kv = ATTN_SUBBLOCKS + 2
        q_ref = refs[0]
        k_refs = refs[1:1 + n_kv]
        v_refs = refs[1 + n_kv:1 + 2 * n_kv]
        kx_ref, vx_ref, sink_ref, o_ref = refs[1 + 2 * n_kv:]
        subs = ATTN_SUBBLOCKS
    else:
        q_ref, kx_ref, vx_ref, sink_ref, o_ref = refs
        subs = 1
    step = pl.program_id(1)
    n_steps = pl.num_programs(1)
    tq = q_ref.shape[1] // subs
    rows = Q_PER_KV * tq
    if window:
        row = lax.broadcasted_iota(jnp.int32, (rows, BLOCK), 0) % tq
        col = lax.broadcasted_iota(jnp.int32, (rows, BLOCK), 1)

    def with_ones(v, ks):
        return jnp.concatenate([v[:, ks], jnp.ones((v.shape[0], HEAD_DIM), BF16)], axis=-1)

    for sub in range(subs):
        q = q_ref[0, sub * tq:(sub + 1) * tq, :]
        if window:
            kp_ref, kc_ref, kn_ref = k_refs[sub:sub + 3]
            vp_ref, vc_ref, vn_ref = v_refs[sub:sub + 3]
            ok_prev = (col >= row) if sub > 0 else (col >= row) & (step > 0)
            ok_next = (col <= row) if sub < subs - 1 else (col <= row) & (step < n_steps - 1)
        outs = []
        for g in range(N_KV_HEADS):
            ks = slice(g * HEAD_DIM, (g + 1) * HEAD_DIM)
            heads = range(g * Q_PER_KV, (g + 1) * Q_PER_KV)
            qs = jnp.concatenate([q[:, h * HEAD_DIM:(h + 1) * HEAD_DIM] for h in heads], axis=0)
            sink = jnp.concatenate([jnp.broadcast_to(sink_ref[h:h + 1, 0:1], (tq, 1)) for h in heads], axis=0)
            lx = kx_ref.shape[1]
            s_x = _dot_nt(qs, kx_ref[0][:, ks])
            m_el = s_x[:, 0:LANES]
            for c in range(1, lx // LANES):
                m_el = jnp.maximum(m_el, s_x[:, c * LANES:(c + 1) * LANES])
            if window:
                s_p = jnp.where(ok_prev, _dot_nt(qs, kp_ref[0][:, ks]), NEG_INF)
                s_c = _dot_nt(qs, kc_ref[0][:, ks])
                s_n = jnp.where(ok_next, _dot_nt(qs, kn_ref[0][:, ks]), NEG_INF)
                m_el = jnp.maximum(jnp.maximum(m_el, s_c), jnp.maximum(s_p, s_n))
            m = jnp.maximum(jnp.max(m_el, axis=-1, keepdims=True), sink)
            acc = _dot(jnp.exp(s_x - m).astype(BF16), with_ones(vx_ref[0], ks))
            if window:
                for s_w, v_ref in ((s_p, vp_ref), (s_c, vc_ref), (s_n, vn_ref)):
                    acc = acc + _dot(jnp.exp(s_w - m).astype(BF16), with_ones(v_ref[0], ks))
            o = acc[:, 0:HEAD_DIM] / (acc[:, HEAD_DIM:] + jnp.exp(sink - m))
            outs += [o[i * tq:(i + 1) * tq] for i in range(Q_PER_KV)]
        o_ref[0, sub * tq:(sub + 1) * tq, :] = jnp.concatenate(outs, axis=-1).astype(BF16)


def _attention(q, k, v, k_ctx, v_ctx, sink_rep, *, window):
    bsz, seq, _ = q.shape
    lc = k_ctx.shape[1]
    ctx_spec = pl.BlockSpec((1, lc, KV_WIDTH), lambda b, i: (b, 0, 0))
    sink_spec = pl.BlockSpec((N_Q_HEADS, LANES), lambda b, i: (0, 0))
    if window:
        tq = BLOCK * ATTN_SUBBLOCKS
        nb = seq // BLOCK

        def kv_spec(j):
            return pl.BlockSpec((1, BLOCK, KV_WIDTH),
                                lambda b, i: (b, jnp.clip(ATTN_SUBBLOCKS * i - 1 + j, 0, nb - 1), 0))

        kv_specs = [kv_spec(j) for j in range(ATTN_SUBBLOCKS + 2)]
        in_specs = ([pl.BlockSpec((1, tq, ATTN_WIDTH), lambda b, i: (b, i, 0))] + kv_specs + kv_specs
                    + [ctx_spec, ctx_spec, sink_spec])
        args = (q,) + (k,) * len(kv_specs) + (v,) * len(kv_specs) + (k_ctx, v_ctx, sink_rep)
    else:
        tq = seq
        in_specs = [pl.BlockSpec((1, tq, ATTN_WIDTH), lambda b, i: (b, i, 0)), ctx_spec, ctx_spec, sink_spec]
        args = (q, k_ctx, v_ctx, sink_rep)
    return pl.pallas_call(
        functools.partial(_attn_kernel, window=window),
        grid=(bsz, seq // tq),
        in_specs=in_specs,
        out_specs=pl.BlockSpec((1, tq, ATTN_WIDTH), lambda b, i: (b, i, 0)),
        out_shape=jax.ShapeDtypeStruct((bsz, seq, ATTN_WIDTH), BF16),
        compiler_params=_cparams("arbitrary", "arbitrary"),
        name="window_attention" if window else "context_attention",
    )(*args)


def _s5_kernel(uf_ref, ub_ref, h0_ref, mask_ref, bst_ref, a_ref, cw_ref,
               yf_ref, yb_ref, hfin_ref, lhs_scr, bu_scr, hs_scr, h_scr, *, bsz, tc):
    i = pl.program_id(0)

    @pl.when(i == 0)
    def _():
        h_scr[...] = h0_ref[...]

    mask = mask_ref[...]
    n_chain = 2 * bsz

    for d, u_ref in enumerate((uf_ref, ub_ref)):
        for b in range(bsz):
            for j in range(tc // 2):
                pair = [jnp.broadcast_to(u_ref[b, 2 * j + k:2 * j + k + 1, :], (SUBLANES, S5_WIDTH)) * mask
                        for k in range(2)]
                lhs_scr[d * bsz + b, 2 * SUBLANES * j:2 * SUBLANES * (j + 1), :] = (
                    jnp.concatenate(pair, axis=0).astype(BF16))
    for c in range(n_chain):
        bu_scr[c] = _dot(lhs_scr[c], bst_ref[c // bsz])

    a_re = [a_ref[d, 0] for d in range(2)]
    a_im = [a_ref[d, 1] for d in range(2)]

    def step(t, carry):
        new = []
        for c in range(n_chain):
            d = c // bsz
            tt = t if d == 0 else tc - 1 - t
            r0 = pl.multiple_of(tt * SUBLANES, SUBLANES)
            h_re, h_im = carry[2 * c], carry[2 * c + 1]
            n_re = a_re[d] * h_re - a_im[d] * h_im + bu_scr[c, pl.ds(r0, SUBLANES), 0:LANES]
            n_im = a_re[d] * h_im + a_im[d] * h_re + bu_scr[c, pl.ds(r0, SUBLANES), LANES:2 * LANES]
            hs_scr[c, 0, pl.ds(r0, SUBLANES), :] = n_re
            hs_scr[c, 1, pl.ds(r0, SUBLANES), :] = n_im
            new += [n_re, n_im]
        return tuple(new)

    init = []
    for c in range(n_chain):
        init += [h_scr[c, :, 0:LANES], h_scr[c, :, LANES:2 * LANES]]
    fin = lax.fori_loop(0, tc, step, tuple(init), unroll=8)
    for c in range(n_chain):
        h_scr[c, :, 0:LANES] = fin[2 * c]
        h_scr[c, :, LANES:2 * LANES] = fin[2 * c + 1]
    hfin_ref[...] = h_scr[...]

    for d, y_ref in enumerate((yf_ref, yb_ref)):
        for b in range(bsz):
            c = d * bsz + b
            parts = [hs_scr[c, ri, pl.ds(s, tc, stride=SUBLANES), :].astype(BF16)
                     for s in range(SUBLANES) for ri in range(2)]
            y_ref[b] = _dot(jnp.concatenate(parts, axis=-1), cw_ref[d])


def _s5_scan(u, h0, mask, bst, a_tiles, cwide, *, tc):
    bsz, seq, _ = u.shape
    nch = seq // tc
    full = lambda shape: pl.BlockSpec(shape, lambda i: (0,) * len(shape))
    fwd = pl.BlockSpec((bsz, tc, S5_WIDTH), lambda i: (0, i, 0))
    bwd = pl.BlockSpec((bsz, tc, S5_WIDTH), lambda i: (0, nch - 1 - i, 0))
    state = (2 * bsz, SUBLANES, 2 * LANES)
    rows = SUBLANES * tc
    return pl.pallas_call(
        functools.partial(_s5_kernel, bsz=bsz, tc=tc),
        grid=(nch,),
        in_specs=[fwd, bwd, full(state), full(mask.shape), full(bst.shape), full(a_tiles.shape),
                  full(cwide.shape)],
        out_specs=[fwd, bwd, full(state)],
        out_shape=[jax.ShapeDtypeStruct(u.shape, F32), jax.ShapeDtypeStruct(u.shape, F32),
                   jax.ShapeDtypeStruct(state, F32)],
        scratch_shapes=[pltpu.VMEM((2 * bsz, rows, S5_WIDTH), BF16),
                        pltpu.VMEM((2 * bsz, rows, 2 * LANES), F32),
                        pltpu.VMEM((2 * bsz, 2, rows, LANES), F32),
                        pltpu.VMEM(state, F32)],
        compiler_params=_cparams("arbitrary"),
        name="s5_scan",
    )(u, u, h0, mask, bst, a_tiles, cwide)


def _s5_mask():
    m = np.arange(S5_WIDTH)[None, :] // CH_PER_CHUNK == np.arange(SUBLANES)[:, None]
    return jnp.asarray(m.astype(np.float32), F32)


def _s5_params(lam_re, lam_im, log_dt, b_re, b_im, c_re, c_im):
    dt = jnp.exp(log_dt)[..., None]
    mag = jnp.exp(lam_re * dt)
    l_re = mag * jnp.cos(lam_im * dt)
    l_im = mag * jnp.sin(lam_im * dt)
    den = lam_re * lam_re + lam_im * lam_im
    f_re = ((l_re - 1.0) * lam_re + l_im * lam_im) / den
    f_im = (l_im * lam_re - (l_re - 1.0) * lam_im) / den
    bb_re = f_re[..., None] * b_re - f_im[..., None] * b_im
    bb_im = f_re[..., None] * b_im + f_im[..., None] * b_re
    a_tiles = jnp.stack([l_re.reshape(2, SUBLANES, LANES), l_im.reshape(2, SUBLANES, LANES)], axis=1)

    half = S5_GROUPS // SUBLANES
    eye = jnp.eye(half, dtype=F32)

    def in_mat(bb):
        t = bb.reshape(2, SUBLANES, half, S5_STATE, S5_GROUP)
        m = jnp.einsum('dsgpc,gh->dsgchp', t, eye)
        return m.reshape(2, S5_WIDTH, half * S5_STATE)

    def out_mat(cc):
        t = cc.reshape(2, SUBLANES, half, S5_GROUP, S5_STATE)
        m = jnp.einsum('dsgcp,st,gh->dsgpthc', t, jnp.eye(SUBLANES, dtype=F32), eye)
        return m.reshape(2, SUBLANES, half * S5_STATE, S5_WIDTH)

    bst = jnp.concatenate([in_mat(bb_re), in_mat(bb_im)], axis=-1).astype(BF16)
    cwide = jnp.stack([out_mat(c_re), out_mat(-c_im)], axis=2)
    cwide = cwide.reshape(2, 2 * SUBLANES * LANES, S5_WIDTH).astype(BF16)
    return bst, a_tiles, cwide


def _conv_kernel(prev_ref, cur_ref, next_ref, wdw_ref, bdw_ref, g_ref, b_ref, wpw_ref, bpw_ref,
                 o_ref, win_ref, *, tm):
    i = pl.program_id(1)
    nt = pl.num_programs(1)
    zero = jnp.zeros((CONV_HALO, CONV_WIDTH), F32)
    win_ref[0:CONV_HALO] = jnp.where(i > 0, prev_ref[0], zero)
    win_ref[CONV_HALO:CONV_HALO + tm] = cur_ref[0]
    win_ref[CONV_HALO + tm:] = jnp.where(i < nt - 1, next_ref[0], zero)
    acc = jnp.zeros((tm, CONV_WIDTH), F32) + bdw_ref[...]
    for r in range(SUBLANES):
        z = win_ref[0:tm + SUBLANES, :] * wdw_ref[r:r + 1, :]
        for a in range(1, (CONV_K + 1) // SUBLANES):
            j = a * SUBLANES
            z = z + win_ref[j:j + tm + SUBLANES, :] * wdw_ref[j + r:j + r + 1, :]
        acc = acc + z[r:r + tm]
    h = _silu(_layer_norm(acc) * g_ref[...] + b_ref[...])
    o_ref[0] = (_dot(h.astype(BF16), wpw_ref[...]) + bpw_ref[...]).astype(BF16)


def _conformer_conv(cg, w_dw, b_dw, ln_g, ln_b, w_pw_bf, b_pw, *, tm):
    bsz, seq, _ = cg.shape
    hb = tm // CONV_HALO
    last = seq // CONV_HALO - 1
    row = lambda a: a.reshape(1, CONV_WIDTH)
    vec = pl.BlockSpec((1, CONV_WIDTH), lambda b, i: (0, 0))
    return pl.pallas_call(
        functools.partial(_conv_kernel, tm=tm),
        grid=(bsz, seq // tm),
        in_specs=[
            pl.BlockSpec((1, CONV_HALO, CONV_WIDTH), lambda b, i: (b, jnp.maximum(i * hb - 1, 0), 0)),
            pl.BlockSpec((1, tm, CONV_WIDTH), lambda b, i: (b, i, 0)),
            pl.BlockSpec((1, CONV_HALO, CONV_WIDTH), lambda b, i: (b, jnp.minimum((i + 1) * hb, last), 0)),
            pl.BlockSpec((CONV_K + 1, CONV_WIDTH), lambda b, i: (0, 0)),
            vec, vec, vec,
            pl.BlockSpec((CONV_WIDTH, CONV_WIDTH), lambda b, i: (0, 0)),
            vec,
        ],
        out_specs=pl.BlockSpec((1, tm, CONV_WIDTH), lambda b, i: (b, i, 0)),
        out_shape=jax.ShapeDtypeStruct((bsz, seq, CONV_WIDTH), BF16),
        scratch_shapes=[pltpu.VMEM((tm + 2 * CONV_HALO, CONV_WIDTH), F32)],
        compiler_params=_cparams("arbitrary", "arbitrary"),
        name="conformer_conv",
    )(cg, cg, cg, jnp.pad(w_dw.reshape(CONV_K, CONV_WIDTH), ((1, 0), (0, 0))), row(b_dw), row(ln_g),
      row(ln_b), w_pw_bf, row(b_pw))


def _mixout_kernel(attn_ref, yf_ref, yb_ref, u_ref, conv_ref, x_ref,
                   dskip_ref, wglu_ref, bglu_ref, wout_ref, bout_ref,
                   gate_ref, g1_ref, b1_ref, shift_ref, scale_ref, wr_hi_ref, wr_lo_ref,
                   xmid_ref, h2_ref, logit_ref):
    tm = x_ref.shape[1]
    chunk = min(tm, MIX_CHUNK)
    for r0 in range(0, tm, chunk):
        rows = slice(r0, r0 + chunk)
        y = dskip_ref[...] * u_ref[0, rows, :] + yf_ref[0, rows, :] + yb_ref[0, rows, :]
        z = _gelu_tanh(y)
        s5 = z * _sigmoid(_dot(z.astype(BF16), wglu_ref[...]) + bglu_ref[...])
        y_mix = (_dot(attn_ref[0, rows, :], wout_ref[0:ATTN_WIDTH, :])
                 + _dot(s5.astype(BF16), wout_ref[ATTN_WIDTH:ATTN_WIDTH + S5_WIDTH, :])
                 + _dot(conv_ref[0, rows, :], wout_ref[ATTN_WIDTH + S5_WIDTH:, :])
                 + bout_ref[...])
        r = DEEPNORM_ALPHA * x_ref[0, rows, :] + gate_ref[0] * y_mix
        x_mid = _layer_norm(r) * g1_ref[...] + b1_ref[...]
        xmid_ref[0, rows, :] = x_mid
        h2 = _layer_norm(x_mid) * (1.0 + scale_ref[0]) + shift_ref[0]
        h2_ref[0, rows, :] = h2.astype(BF16)
        logit_ref[0, rows, :] = _dot3(h2, wr_hi_ref[...], wr_lo_ref[...])


def _mixer_output(attn, yf, yb, u, conv, x, d_skip, w_glu_bf, b_glu, w_out_bf, b_out,
                  gate1, ln_g, ln_b, shift2, scale2, wr_hi, wr_lo, *, tm):
    bsz, seq, _ = x.shape
    tok = lambda w: pl.BlockSpec((1, tm, w), lambda b, i: (b, i, 0))
    full = lambda r, c: pl.BlockSpec((r, c), lambda b, i: (0, 0))
    bvec = pl.BlockSpec((1, 1, D_MODEL), lambda b, i: (b, 0, 0))
    row = lambda a: a.reshape(1, -1)
    return pl.pallas_call(
        _mixout_kernel,
        grid=(bsz, seq // tm),
        in_specs=[
            tok(ATTN_WIDTH), tok(S5_WIDTH), tok(S5_WIDTH), tok(S5_WIDTH), tok(CONV_WIDTH), tok(D_MODEL),
            full(1, S5_WIDTH), full(S5_WIDTH, S5_WIDTH), full(1, S5_WIDTH),
            full(D_MODEL, D_MODEL), full(1, D_MODEL),
            bvec, full(1, D_MODEL), full(1, D_MODEL), bvec, bvec,
            full(D_MODEL, LANES), full(D_MODEL, LANES),
        ],
        out_specs=[tok(D_MODEL), tok(D_MODEL), tok(LANES)],
        out_shape=[
            jax.ShapeDtypeStruct((bsz, seq, D_MODEL), F32),
            jax.ShapeDtypeStruct((bsz, seq, D_MODEL), BF16),
            jax.ShapeDtypeStruct((bsz, seq, LANES), F32),
        ],
        compiler_params=_cparams("arbitrary", "arbitrary"),
        name="mixer_output",
    )(attn, yf, yb, u, conv, x, row(d_skip), w_glu_bf, row(b_glu), w_out_bf, row(b_out),
      gate1, row(ln_g), row(ln_b), shift2, scale2, wr_hi, wr_lo)


def _token_cumsum(m, tri, ones):
    nblk = m.shape[0]
    m2 = m.reshape(nblk * N_EXPERTS, LANES).astype(BF16)
    within = _dot(m2, tri).reshape(nblk, N_EXPERTS, LANES)
    tot = _dot(m2, ones).reshape(nblk, N_EXPERTS, LANES)
    offs = []
    run = jnp.zeros((N_EXPERTS, LANES), F32)
    for j in range(nblk):
        offs.append(run)
        run = run + tot[j]
    off = jnp.stack(offs, axis=0)
    return within + off, off, tot


def _router_kernel(logit_ref, tri_ref, ones_ref, aff_ref, sel_ref, pos_ref, off_ref, cnt_ref, *, cap, slot0):
    b = pl.program_id(0)
    nblk = aff_ref.shape[1]

    def soft(j, carry):
        r0 = pl.multiple_of(j * LANES, LANES)
        t = logit_ref[0, pl.ds(r0, LANES), :].T[0:N_EXPERTS]
        ex = jnp.exp(t - jnp.max(t, axis=0, keepdims=True))
        aff_ref[0, j] = ex / jnp.sum(ex, axis=0, keepdims=True)
        return carry

    lax.fori_loop(0, nblk, soft, 0)
    aff = aff_ref[0]

    def enough(cand):
        cnt = jnp.sum(jnp.where(aff >= cand[None], 1.0, 0.0), axis=0)
        return jnp.sum(cnt, axis=-1, keepdims=True) >= cap

    p = jnp.full((N_EXPERTS, LANES), 2.0, F32)
    for k in range(6, -1, -1):
        cand = p * (2.0 ** -(2 ** k))
        p = jnp.where(enough(cand), p, cand)
    thr = 0.5 * p
    thr = jnp.where(enough(thr), thr, 0.0)

    def refine(_, carry):
        lo, step = carry
        cand = lo + step
        return jnp.where(enough(cand), cand, lo), 0.5 * step

    thr, _ = lax.fori_loop(0, 23, refine, (thr, 0.5 * thr))
    gt = aff > thr[None]
    eq = aff == thr[None]
    n_gt = jnp.sum(jnp.sum(jnp.where(gt, 1.0, 0.0), axis=0), axis=-1, keepdims=True)
    need = cap - n_gt
    tri = tri_ref[...]
    ones = ones_ref[...]
    cum_eq, _, _ = _token_cumsum(jnp.where(eq, 1.0, 0.0), tri, ones)
    sel = jnp.where(gt | (eq & (cum_eq <= need[None])), 1.0, 0.0)
    cum_sel, off, cnt = _token_cumsum(sel, tri, ones)
    base = (slot0 + b * cap).astype(F32)
    sel_ref[0] = sel
    pos_ref[0] = cum_sel - sel + base
    off_ref[0] = off + base
    cnt_ref[0] = cnt


def _router(logits, *, slot0):
    bsz, n, _ = logits.shape
    nblk = n // LANES
    cap = CAPACITY_FACTOR * n // N_EXPERTS
    idx = np.arange(LANES)
    tri = jnp.asarray((idx[:, None] <= idx[None, :]).astype(np.float32), BF16)
    ones = jnp.ones((LANES, LANES), BF16)
    shape = (bsz, nblk, N_EXPERTS, LANES)
    out = pl.BlockSpec((1, nblk, N_EXPERTS, LANES), lambda b: (b, 0, 0, 0))
    sq = pl.BlockSpec((LANES, LANES), lambda b: (0, 0))
    return pl.pallas_call(
        functools.partial(_router_kernel, cap=cap, slot0=slot0),
        grid=(bsz,),
        in_specs=[pl.BlockSpec((1, n, LANES), lambda b: (b, 0, 0)), sq, sq],
        out_specs=[out] * 5,
        out_shape=[jax.ShapeDtypeStruct(shape, F32)] * 5,
        compiler_params=_cparams("arbitrary"),
        name="router",
    )(logits, tri, ones)


COMBINE_KW = LANES + 16
SMALL_COUNT = 32
COMBINE_SMALL_KW = SMALL_COUNT + 16
FF_TILE = 256


def _dispatch_blocks(tb_ref, n_entries, e, blocks, xs_buf, g_buf):
    most = tb_ref[n_entries + blocks[0][0] * N_EXPERTS + e]
    for tile, _ in blocks[1:]:
        most = jnp.maximum(most, tb_ref[n_entries + tile * N_EXPERTS + e])

    def run(kw):
        k = lax.broadcasted_iota(jnp.int32, (kw, LANES), 0).astype(F32)
        for tile, load in blocks:
            h2_blk, pos_row, sel_row, aff_row = load()
            base = tb_ref[tile * N_EXPERTS + e]
            base16 = pl.multiple_of(lax.shift_left(lax.shift_right_logical(base, 4), 4), 16)
            hit = (k == (pos_row - base16.astype(F32))) & (sel_row > 0.5)
            sel_t = jnp.where(hit, 1.0, 0.0).astype(BF16)
            xs_buf[pl.ds(base16, kw), :] += _dot(sel_t, h2_blk).astype(BF16)
            g = jnp.sum(jnp.where(hit, aff_row, 0.0), axis=-1, keepdims=True)
            g_buf[pl.ds(base16, kw), :] += jnp.broadcast_to(g, (kw, LANES))

    @pl.when(most <= SMALL_COUNT)
    def _():
        run(COMBINE_SMALL_KW)

    @pl.when(most > SMALL_COUNT)
    def _():
        run(COMBINE_KW)


def _moe_kernel(*refs, n_ctx, rows, n_entries):
    tb_ref = refs[0]
    if n_ctx:
        (posx_ref, selx_ref, affx_ref, h2x_ref, posc_ref, selc_ref, affc_ref, h2c_ref,
         wg_ref, wu_ref, wd_ref, y_ref, acc_scr, g_scr, xs_scr) = refs[1:]
    else:
        (posx_ref, selx_ref, affx_ref, h2x_ref, wg_ref, wu_ref, wd_ref, y_ref,
         acc_scr, g_scr, xs_scr) = refs[1:]
    p = pl.program_id(0)
    s = pl.program_id(1)
    last = pl.num_programs(1) - 1
    fill = lax.rem(p, 2)
    use = 1 - fill
    blk_per_step = posx_ref.shape[2]

    @pl.when(p < N_EXPERTS)
    def _():
        xs_buf = xs_scr.at[fill]
        g_buf = g_scr.at[fill]

        @pl.when(s == 0)
        def _():
            xs_buf[...] = jnp.zeros(xs_buf.shape, BF16)
            g_buf[...] = jnp.zeros(g_buf.shape, F32)

        if n_ctx:
            @pl.when(s == 0)
            def _():
                bsz_c, blk_c = posc_ref.shape[0], posc_ref.shape[2]

                def load_c(b, i):
                    return lambda: (h2c_ref[b, i * LANES:(i + 1) * LANES, :], posc_ref[b, 0, i:i + 1, :],
                                    selc_ref[b, 0, i:i + 1, :], affc_ref[b, 0, i:i + 1, :])

                blocks = [((last + 1) * blk_per_step + b * blk_c + i, load_c(b, i))
                          for b in range(bsz_c) for i in range(blk_c)]
                _dispatch_blocks(tb_ref, n_entries, p, blocks, xs_buf, g_buf)

        def load_x(i):
            return lambda: (h2x_ref[0, i * LANES:(i + 1) * LANES, :], posx_ref[0, 0, i:i + 1, :],
                            selx_ref[0, 0, i:i + 1, :], affx_ref[0, 0, i:i + 1, :])

        blocks = [(s * blk_per_step + i, load_x(i)) for i in range(blk_per_step)]
        _dispatch_blocks(tb_ref, n_entries, p, blocks, xs_buf, g_buf)

    @pl.when(p >= 1)
    def _():
        wg = wg_ref[0, 0].astype(BF16)
        wu = wu_ref[0, 0].astype(BF16)
        wd = wd_ref[0, 0].astype(BF16)
        half = rows // 2
        for r0 in (0, half):
            xs = xs_scr[use, r0:r0 + half, :]
            hid = _silu(_dot(xs, wg)) * _dot(xs, wu)
            part = _dot(hid.astype(BF16), wd)
            acc_scr[r0:r0 + half, :] = jnp.where(s == 0, part, acc_scr[r0:r0 + half, :] + part)

        @pl.when(s == last)
        def _():
            g = g_scr[use, 0:rows, :]
            gated = acc_scr[...] * jnp.concatenate([g] * (D_MODEL // LANES), axis=-1)
            y_ref[0, 0:rows, :] = gated.astype(BF16)
            y_ref[0, rows:, :] = jnp.zeros((y_ref.shape[1] - rows, D_MODEL), BF16)


def _moe_experts(tb, route_x, h2x, route_c, h2c, w_gate, w_up, w_down, *, layer):
    bsz, n, d = h2x.shape
    n_ff = EXPERT_FF // FF_TILE
    x_step = bsz * n // n_ff
    steps_per_b = n // x_step
    blk_per_step = x_step // LANES
    n_ctx = 0 if h2c is None else h2c.shape[0] * h2c.shape[1]
    rows = bsz * CAPACITY_FACTOR * n // N_EXPERTS + CAPACITY_FACTOR * n_ctx // N_EXPERTS
    rows_pad = rows + COMBINE_KW
    last_e = N_EXPERTS - 1

    def xs_idx(p, s):
        sc = jnp.where(p > last_e, n_ff - 1, s)
        return sc // steps_per_b, sc % steps_per_b

    def route_x_spec():
        return pl.BlockSpec((1, 1, blk_per_step, LANES),
                            lambda p, s, tb: (xs_idx(p, s)[0], jnp.minimum(p, last_e), xs_idx(p, s)[1], 0))

    in_specs = [route_x_spec(), route_x_spec(), route_x_spec(),
                pl.BlockSpec((1, x_step, d), lambda p, s, tb: (xs_idx(p, s)[0], xs_idx(p, s)[1], 0))]
    args = list(route_x) + [h2x]
    if n_ctx:
        bc, nc, _ = h2c.shape
        rc = pl.BlockSpec((bc, 1, nc // LANES, LANES), lambda p, s, tb: (0, jnp.minimum(p, last_e), 0, 0))
        in_specs += [rc, rc, rc, pl.BlockSpec((bc, nc, d), lambda p, s, tb: (0, 0, 0))]
        args += list(route_c) + [h2c]
    ffn_e = lambda p: jnp.maximum(p - 1, 0)
    in_specs += [pl.BlockSpec((1, 1, d, FF_TILE), lambda p, s, tb: (layer, ffn_e(p), 0, s)),
                 pl.BlockSpec((1, 1, d, FF_TILE), lambda p, s, tb: (layer, ffn_e(p), 0, s)),
                 pl.BlockSpec((1, 1, FF_TILE, d), lambda p, s, tb: (layer, ffn_e(p), s, 0))]
    args += [w_gate, w_up, w_down]
    return pl.pallas_call(
        functools.partial(_moe_kernel, n_ctx=n_ctx, rows=rows, n_entries=tb.shape[0] // 2),
        grid_spec=pltpu.PrefetchScalarGridSpec(
            num_scalar_prefetch=1,
            grid=(N_EXPERTS + 1, n_ff),
            in_specs=in_specs,
            out_specs=pl.BlockSpec((1, rows_pad, d), lambda p, s, tb: (ffn_e(p), 0, 0)),
            scratch_shapes=[pltpu.VMEM((rows, d), F32), pltpu.VMEM((2, rows_pad, LANES), F32),
                            pltpu.VMEM((2, rows_pad, d), BF16)],
        ),
        out_shape=jax.ShapeDtypeStruct((N_EXPERTS, rows_pad, d), BF16),
        compiler_params=_cparams("arbitrary", "arbitrary"),
        name="moe_experts",
    )(tb, *args)


def _combine_kernel(tb_ref, pos_ref, sel_ref, xmid_ref, gate_ref, g_ref, b_ref, y_hbm, o_ref,
                    y_scr, stack_scr, sem, *, cap, slot0, blk_per_step, nblk, n_entries):
    b = pl.program_id(0)
    j = pl.program_id(1)
    win = y_scr.shape[1]

    @pl.when(j == 0)
    def _():
        row0 = pl.multiple_of(slot0 + b * cap, 16)
        cp = pltpu.make_async_copy(y_hbm.at[:, pl.ds(row0, win), :], y_scr, sem)
        cp.start()
        cp.wait()

    set_base = (slot0 + b * cap).astype(F32)
    tn_dims = (((0,), (0,)), ((), ()))
    for i in range(blk_per_step):
        entry = (b * nblk + j * blk_per_step + i) * N_EXPERTS
        rows = slice(i * LANES, (i + 1) * LANES)

        def hits(e, kw, entry=entry, i=i):
            base = tb_ref[entry + e] - (slot0 + b * cap)
            base16 = pl.multiple_of(lax.shift_left(lax.shift_right_logical(base, 4), 4), 16)
            k = lax.broadcasted_iota(jnp.int32, (kw, LANES), 0).astype(F32)
            rel = pos_ref[0, i, e:e + 1, :] - (set_base + base16.astype(F32))
            hit = (k == rel) & (sel_ref[0, i, e:e + 1, :] > 0.5)
            return base16, jnp.where(hit, 1.0, 0.0).astype(BF16)

        def finish(moe, rows=rows):
            r = DEEPNORM_ALPHA * xmid_ref[0, rows, :] + gate_ref[0] * moe
            o_ref[0, rows, :] = _layer_norm(r) * g_ref[...] + b_ref[...]

        most = tb_ref[n_entries + entry]
        for e in range(1, N_EXPERTS):
            most = jnp.maximum(most, tb_ref[n_entries + entry + e])

        @pl.when(most <= SMALL_COUNT)
        def _():
            sel_all = []
            for e in range(N_EXPERTS):
                base16, sel_t = hits(e, COMBINE_SMALL_KW)
                sel_all.append(sel_t)
                stack_scr[e * COMBINE_SMALL_KW:(e + 1) * COMBINE_SMALL_KW, :] = (
                    y_scr[e, pl.ds(base16, COMBINE_SMALL_KW), :])
            finish(lax.dot_general(jnp.concatenate(sel_all, axis=0), stack_scr[...], tn_dims,
                                   preferred_element_type=F32))

        @pl.when(most > SMALL_COUNT)
        def _():
            acc = jnp.zeros((LANES, D_MODEL), F32)
            for e in range(N_EXPERTS):
                base16, sel_t = hits(e, COMBINE_KW)
                acc = acc + lax.dot_general(sel_t, y_scr[e, pl.ds(base16, COMBINE_KW), :], tn_dims,
                                            preferred_element_type=F32)
            finish(acc)


def _combine_post(tb, pos, sel, x_mid, gate2, ln_g, ln_b, y, *, slot0, tm):
    bsz, n, d = x_mid.shape
    nblk = n // LANES
    cap = CAPACITY_FACTOR * n // N_EXPERTS
    blk_per_step = tm // LANES
    tok = pl.BlockSpec((1, tm, d), lambda b, j, tb: (b, j, 0))
    route = pl.BlockSpec((1, blk_per_step, N_EXPERTS, LANES), lambda b, j, tb: (b, j, 0, 0))
    vec = pl.BlockSpec((1, d), lambda b, j, tb: (0, 0))
    return pl.pallas_call(
        functools.partial(_combine_kernel, cap=cap, slot0=slot0, blk_per_step=blk_per_step, nblk=nblk,
                          n_entries=bsz * nblk * N_EXPERTS),
        grid_spec=pltpu.PrefetchScalarGridSpec(
            num_scalar_prefetch=1,
            grid=(bsz, n // tm),
            in_specs=[route, route, tok, pl.BlockSpec((1, 1, d), lambda b, j, tb: (b, 0, 0)), vec, vec,
                      pl.BlockSpec(memory_space=pl.ANY)],
            out_specs=tok,
            scratch_shapes=[pltpu.VMEM((N_EXPERTS, cap + COMBINE_KW, d), BF16),
                            pltpu.VMEM((N_EXPERTS * COMBINE_SMALL_KW, d), BF16),
                            pltpu.SemaphoreType.DMA(())],
        ),
        out_shape=jax.ShapeDtypeStruct(x_mid.shape, F32),
        compiler_params=_cparams("arbitrary", "arbitrary"),
        name="combine_post",
    )(tb, pos, sel, x_mid, gate2, ln_g.reshape(1, d), ln_b.reshape(1, d), y)


def _tile_table(*lane_replicated):
    return jnp.concatenate([a[..., 0].astype(jnp.int32).reshape(-1) for a in lane_replicated])


def _expert_major(a):
    return jnp.swapaxes(a, 1, 2)


def kernel(x, c, ctx, c_ctx, w_mod, b_mod, w_in, b_in, attn_sink, s5_lam_re, s5_lam_im, s5_log_dt,
           s5_b_re, s5_b_im, s5_c_re, s5_c_im, s5_d, s5_w_glu, s5_b_glu, conv_w_dw, conv_b_dw,
           conv_ln_g, conv_ln_b, conv_w_pw, conv_b_pw, w_out, b_out, ln1_g, ln1_b, w_router,
           exp_w_gate, exp_w_up, exp_w_down, ln2_g, ln2_b):
    bsz, seq, d = x.shape
    lc = ctx.shape[1]
    tm_x, tm_c = min(1024, seq), lc

    cond = jnp.zeros((SUBLANES, d), F32).at[:bsz].set(c).at[bsz].set(c_ctx)
    mods = _modulation(cond, w_mod, b_mod)
    cos_t, sin_t = _rope_tables(seq)
    s5_mask = _s5_mask()
    zero_state = jnp.zeros((2 * bsz, SUBLANES, 2 * LANES), F32)

    xc = ctx
    for l in range(DEPTH):
        last = l == DEPTH - 1
        mod_x = [mods[l, :bsz, k * d:(k + 1) * d].reshape(bsz, 1, d) for k in range(6)]
        mod_c = [jnp.broadcast_to(mods[l, bsz, k * d:(k + 1) * d].reshape(1, 1, d), (bsz, 1, d))
                 for k in range(6)]
        w_in_bf = w_in[l].astype(BF16)
        w_out_bf = w_out[l].astype(BF16)
        w_glu_bf = s5_w_glu[l].astype(BF16)
        w_pw_bf = conv_w_pw[l].astype(BF16)
        wr = jnp.pad(w_router[l], ((0, 0), (0, LANES - N_EXPERTS)))
        wr_hi, wr_lo = _split_bf16(wr)
        sink_rep = jnp.broadcast_to(attn_sink[l][:, None], (N_Q_HEADS, LANES))
        bst, a_tiles, cwide = _s5_params(s5_lam_re[l], s5_lam_im[l], s5_log_dt[l], s5_b_re[l], s5_b_im[l],
                                       s5_c_re[l], s5_c_im[l])

        q, k, v, u, cg = _in_projection(x, mod_x[0], mod_x[1], w_in_bf, b_in[l], cos_t, sin_t,
                                        rope=True, tm=tm_x)
        q_c, k_c, v_c, u_c, cg_c = _in_projection(xc, mod_c[0], mod_c[1], w_in_bf, b_in[l],
                                                  cos_t[:lc], sin_t[:lc], rope=False, tm=tm_c)

        attn_x = _attention(q, k, v, k_c, v_c, sink_rep, window=True)
        yf_c, yb_c, h_ctx = _s5_scan(u_c, zero_state, s5_mask, bst, a_tiles, cwide, tc=lc)
        yf, yb, _ = _s5_scan(u, h_ctx, s5_mask, bst, a_tiles, cwide, tc=256)
        conv_args = (conv_w_dw[l], conv_b_dw[l], conv_ln_g[l], conv_ln_b[l], w_pw_bf, conv_b_pw[l])
        conv_x = _conformer_conv(cg, *conv_args, tm=256)
        mix_args = (s5_d[l], w_glu_bf, s5_b_glu[l], w_out_bf, b_out[l])
        x_mid, h2, logits = _mixer_output(attn_x, yf, yb, u, conv_x, x, *mix_args, mod_x[2], ln1_g[l],
                                          ln1_b[l], mod_x[3], mod_x[4], wr_hi, wr_lo, tm=tm_x)
        aff_x, sel_x, pos_x, off_x, cnt_x = _router(logits, slot0=0)
        route_x = tuple(_expert_major(a) for a in (pos_x, sel_x, aff_x))
        tb_x = _tile_table(off_x, cnt_x)
        if not last:
            attn_c = _attention(q_c, None, None, k_c, v_c, sink_rep, window=False)
            conv_c = _conformer_conv(cg_c, *conv_args, tm=lc)
            xc_mid, hc2, logits_c = _mixer_output(attn_c, yf_c, yb_c, u_c, conv_c, xc, *mix_args, mod_c[2],
                                                  ln1_g[l], ln1_b[l], mod_c[3], mod_c[4], wr_hi, wr_lo,
                                                  tm=tm_c)
            slot0_c = bsz * CAPACITY_FACTOR * seq // N_EXPERTS
            aff_c, sel_c, pos_c, off_c, cnt_c = _router(logits_c, slot0=slot0_c)
            route_c = tuple(_expert_major(a) for a in (pos_c, sel_c, aff_c))
            tb_c = _tile_table(off_c, cnt_c)
            y = _moe_experts(_tile_table(off_x, off_c, cnt_x, cnt_c), route_x, h2, route_c, hc2,
                             exp_w_gate, exp_w_up, exp_w_down, layer=l)
            xc = _combine_post(tb_c, pos_c, sel_c, xc_mid, mod_c[5], ln2_g[l], ln2_b[l], y,
                               slot0=slot0_c, tm=lc)
        else:
            y = _moe_experts(tb_x, route_x, h2, None, None, exp_w_gate, exp_w_up, exp_w_down, layer=l)
        x = _combine_post(tb_x, pos_x, sel_x, x_mid, mod_x[5], ln2_g[l], ln2_b[l], y, slot0=0, tm=256)
    return x
```

```python
import functools
import math

import jax
import jax.numpy as jnp
import numpy as np
from jax import lax
from jax.experimental import pallas as pl
from jax.experimental.pallas import tpu as pltpu

D_MODEL = 1024
DEPTH = 2
GRID_W = 64
HEAD_DIM = 64
ATTN_WIDTH = D_MODEL // 2
N_Q_HEADS = ATTN_WIDTH // HEAD_DIM
N_KV_HEADS = N_Q_HEADS // 4
Q_PER_KV = N_Q_HEADS // N_KV_HEADS
KV_WIDTH = N_KV_HEADS * HEAD_DIM
BLOCK = 128
ROPE_BASE = 10000.0
NEG_INF = -1e30
S5_WIDTH = D_MODEL // 4
S5_GROUP = 16
S5_GROUPS = S5_WIDTH // S5_GROUP
S5_STATE = 64
CONV_WIDTH = D_MODEL - ATTN_WIDTH - S5_WIDTH
CONV_K = 31
CONV_HALO = 16
Q_END = ATTN_WIDTH
K_END = Q_END + KV_WIDTH
V_END = K_END + KV_WIDTH
U_END = V_END + S5_WIDTH
IN_WIDTH = U_END + 2 * CONV_WIDTH
N_EXPERTS = 16
EXPERT_FF = 2 * D_MODEL
CAPACITY_FACTOR = 2
DEEPNORM_ALPHA = (2.0 * DEPTH) ** 0.25
LN_EPS = 1e-5

LANES = 128
SUBLANES = 8
S5_CHUNKS = S5_WIDTH * S5_STATE // S5_GROUP // LANES
CH_PER_CHUNK = S5_WIDTH // S5_CHUNKS
VMEM_LIMIT = 56 * 1024 * 1024
MIX_CHUNK = 256

F32 = jnp.float32
BF16 = jnp.bfloat16


def _cparams(*sem):
    return pltpu.CompilerParams(dimension_semantics=sem, vmem_limit_bytes=VMEM_LIMIT)


def _dot(a, b):
    return jnp.dot(a, b, preferred_element_type=F32)


def _dot_nt(a, b):
    return lax.dot_general(a, b, (((1,), (1,)), ((), ())), preferred_element_type=F32)


def _split_bf16(x):
    hi = x.astype(BF16)
    lo = (x - hi.astype(F32)).astype(BF16)
    return hi, lo


def _dot3(a, b_hi, b_lo):
    a_hi, a_lo = _split_bf16(a)
    return _dot(a_hi, b_hi) + (_dot(a_lo, b_hi) + _dot(a_hi, b_lo))


def _sigmoid(x):
    return 1.0 / (1.0 + jnp.exp(-x))


def _silu(x):
    return x * _sigmoid(x)


def _gelu_tanh(x):
    c = math.sqrt(2.0 / math.pi)
    return 0.5 * x * (1.0 + jnp.tanh(c * (x + 0.044715 * (x * x * x))))


def _layer_norm(x):
    mu = jnp.mean(x, axis=-1, keepdims=True)
    xc = x - mu
    var = jnp.mean(xc * xc, axis=-1, keepdims=True)
    return xc * lax.rsqrt(var + LN_EPS)


def _mod_kernel(c_ref, w_ref, b_ref, o_ref):
    s = _silu(c_ref[...])
    w = w_ref[0]
    w_hi, w_lo = _split_bf16(w)
    o_ref[0] = _dot3(s, w_hi, w_lo) + b_ref[0]


def _modulation(cond, w_mod, b_mod):
    tn = 1536
    n = w_mod.shape[-1]
    return pl.pallas_call(
        _mod_kernel,
        grid=(DEPTH, n // tn),
        in_specs=[
            pl.BlockSpec((SUBLANES, D_MODEL), lambda l, j: (0, 0)),
            pl.BlockSpec((1, D_MODEL, tn), lambda l, j: (l, 0, j)),
            pl.BlockSpec((1, 1, tn), lambda l, j: (l, 0, j)),
        ],
        out_specs=pl.BlockSpec((1, SUBLANES, tn), lambda l, j: (l, 0, j)),
        out_shape=jax.ShapeDtypeStruct((DEPTH, SUBLANES, n), F32),
        compiler_params=_cparams("arbitrary", "arbitrary"),
        name="modulation",
    )(cond, w_mod, b_mod.reshape(DEPTH, 1, n))


def _rope_chunk(x, cos, sin_signed):
    lane = lax.broadcasted_iota(jnp.int32, x.shape, 1)
    first = (lane % 32) < 16
    partner = jnp.where(first, pltpu.roll(x, LANES - 16, 1), pltpu.roll(x, 16, 1))
    return x * cos + partner * sin_signed


def _mod_spec(layer, k):
    return pl.BlockSpec((1, SUBLANES, D_MODEL), lambda *_: (layer, 0, k))


def _mod_vec(ref, mod_row):
    row = pl.program_id(0) if mod_row is None else mod_row
    return ref[0, pl.ds(row, 1), :]


def _inproj_kernel(x_ref, shift_ref, scale_ref, w_ref, b_ref, cos_ref, sin_ref,
                   q_ref, k_ref, v_ref, u_ref, cg_ref, *, rope, mod_row):
    shift = _mod_vec(shift_ref, mod_row)
    scale1 = 1.0 + _mod_vec(scale_ref, mod_row)
    tm = x_ref.shape[1]
    chunk = min(tm, MIX_CHUNK)
    scale = HEAD_DIM ** -0.5
    for r0 in range(0, tm, chunk):
        rows = slice(r0, r0 + chunk)
        h = _layer_norm(x_ref[0, rows, :]) * scale1 + shift
        p = _dot(h.astype(BF16), w_ref[...]) + b_ref[...]
        if rope:
            cos = cos_ref[rows, :]
            sin = sin_ref[rows, :]
        for j in range(ATTN_WIDTH // LANES):
            qc = p[:, j * LANES:(j + 1) * LANES]
            if rope:
                qc = _rope_chunk(qc, cos, sin)
            q_ref[0, rows, j * LANES:(j + 1) * LANES] = (qc * scale).astype(BF16)
        kc = p[:, Q_END:K_END]
        if rope:
            kc = _rope_chunk(kc, cos, sin)
        k_ref[0, rows, :] = kc.astype(BF16)
        v_ref[0, rows, :] = p[:, K_END:V_END].astype(BF16)
        u_ref[0, rows, :] = p[:, V_END:U_END]
        a = p[:, U_END:U_END + CONV_WIDTH]
        g = p[:, U_END + CONV_WIDTH:]
        cg_ref[0, rows, :] = a * _sigmoid(g)


def _in_projection(x, mods, w_in_bf, b_in, cos_t, sin_t, *, layer, mod_row, rope, tm):
    bsz, seq, _ = x.shape
    tok = lambda w: pl.BlockSpec((1, tm, w), lambda b, i: (b, i, 0))
    tab = pl.BlockSpec((tm, LANES), lambda b, i: (i, 0))
    return pl.pallas_call(
        functools.partial(_inproj_kernel, rope=rope, mod_row=mod_row),
        grid=(bsz, seq // tm),
        in_specs=[
            tok(D_MODEL), _mod_spec(layer, 0), _mod_spec(layer, 1),
            pl.BlockSpec((D_MODEL, IN_WIDTH), lambda b, i: (0, 0)),
            pl.BlockSpec((1, IN_WIDTH), lambda b, i: (0, 0)),
            tab, tab,
        ],
        out_specs=[tok(ATTN_WIDTH), tok(KV_WIDTH), tok(KV_WIDTH), tok(S5_WIDTH), tok(CONV_WIDTH)],
        out_shape=[
            jax.ShapeDtypeStruct((bsz, seq, ATTN_WIDTH), BF16),
            jax.ShapeDtypeStruct((bsz, seq, KV_WIDTH), BF16),
            jax.ShapeDtypeStruct((bsz, seq, KV_WIDTH), BF16),
            jax.ShapeDtypeStruct((bsz, seq, S5_WIDTH), F32),
            jax.ShapeDtypeStruct((bsz, seq, CONV_WIDTH), F32),
        ],
        compiler_params=_cparams("arbitrary", "arbitrary"),
        name="in_projection",
    )(x, mods, mods, w_in_bf, b_in.reshape(1, IN_WIDTH), cos_t, sin_t)


def _rope_tables(seq):
    lane = np.arange(LANES)
    i = lane % HEAD_DIM
    use_row = i < HEAD_DIM // 2
    f = HEAD_DIM // 4
    inv_freq = jnp.asarray(ROPE_BASE, F32) ** (-jnp.asarray(i % f, F32) / f)
    t = jnp.arange(seq, dtype=jnp.int32)
    pos = jnp.where(use_row[None, :], (t // GRID_W)[:, None], (t % GRID_W)[:, None]).astype(F32)
    ang = pos * inv_freq[None, :]
    sign = jnp.where((lane % (2 * f)) < f, -1.0, 1.0).astype(F32)
    return jnp.cos(ang), jnp.sin(ang) * sign[None, :]


ATTN_SUBBLOCKS = 4


def _attn_kernel(*refs, window):
    if window:
        n_kv = ATTN_SUBBLOCKS + 2
        q_ref = refs[0]
        k_refs = refs[1:1 + n_kv]
        v_refs = refs[1 + n_kv:1 + 2 * n_kv]
        kx_ref, vx_ref, sink_ref, o_ref = refs[1 + 2 * n_kv:]
        subs = ATTN_SUBBLOCKS
    else:
        q_ref, kx_ref, vx_ref, sink_ref, o_ref = refs
        subs = 1
    step = pl.program_id(1)
    n_steps = pl.num_programs(1)
    tq = q_ref.shape[1] // subs
    rows = Q_PER_KV * tq
    if window:
        row = lax.broadcasted_iota(jnp.int32, (rows, BLOCK), 0) % tq
        col = lax.broadcasted_iota(jnp.int32, (rows, BLOCK), 1)

    def with_ones(v, ks):
        return jnp.concatenate([v[:, ks], jnp.ones((v.shape[0], HEAD_DIM), BF16)], axis=-1)

    for sub in range(subs):
        q = q_ref[0, sub * tq:(sub + 1) * tq, :]
        if window:
            kp_ref, kc_ref, kn_ref = k_refs[sub:sub + 3]
            vp_ref, vc_ref, vn_ref = v_refs[sub:sub + 3]
            ok_prev = (col >= row) if sub > 0 else (col >= row) & (step > 0)
            ok_next = (col <= row) if sub < subs - 1 else (col <= row) & (step < n_steps - 1)
        outs = []
        for g in range(N_KV_HEADS):
            ks = slice(g * HEAD_DIM, (g + 1) * HEAD_DIM)
            heads = range(g * Q_PER_KV, (g + 1) * Q_PER_KV)
            qs = jnp.concatenate([q[:, h * HEAD_DIM:(h + 1) * HEAD_DIM] for h in heads], axis=0)
            sink = jnp.concatenate([jnp.broadcast_to(sink_ref[h:h + 1, 0:1], (tq, 1)) for h in heads], axis=0)
            lx = kx_ref.shape[1]
            s_x = _dot_nt(qs, kx_ref[0][:, ks])
            m_el = s_x[:, 0:LANES]
            for c in range(1, lx // LANES):
                m_el = jnp.maximum(m_el, s_x[:, c * LANES:(c + 1) * LANES])
            if window:
                s_p = jnp.where(ok_prev, _dot_nt(qs, kp_ref[0][:, ks]), NEG_INF)
                s_c = _dot_nt(qs, kc_ref[0][:, ks])
                s_n = jnp.where(ok_next, _dot_nt(qs, kn_ref[0][:, ks]), NEG_INF)
                m_el = jnp.maximum(jnp.maximum(m_el, s_c), jnp.maximum(s_p, s_n))
            m = jnp.maximum(jnp.max(m_el, axis=-1, keepdims=True), sink)
            acc = _dot(jnp.exp(s_x - m).astype(BF16), with_ones(vx_ref[0], ks))
            if window:
                for s_w, v_ref in ((s_p, vp_ref), (s_c, vc_ref), (s_n, vn_ref)):
                    acc = acc + _dot(jnp.exp(s_w - m).astype(BF16), with_ones(v_ref[0], ks))
            o = acc[:, 0:HEAD_DIM] / (acc[:, HEAD_DIM:] + jnp.exp(sink - m))
            outs += [o[i * tq:(i + 1) * tq] for i in range(Q_PER_KV)]
        o_ref[0, sub * tq:(sub + 1) * tq, :] = jnp.concatenate(outs, axis=-1).astype(BF16)


def _attention(q, k, v, k_ctx, v_ctx, sink_rep, *, window):
    bsz, seq, _ = q.shape
    lc = k_ctx.shape[1]
    ctx_spec = pl.BlockSpec((1, lc, KV_WIDTH), lambda b, i: (b, 0, 0))
    sink_spec = pl.BlockSpec((N_Q_HEADS, LANES), lambda b, i: (0, 0))
    if window:
        tq = BLOCK * ATTN_SUBBLOCKS
        nb = seq // BLOCK

        def kv_spec(j):
            return pl.BlockSpec((1, BLOCK, KV_WIDTH),
                                lambda b, i: (b, jnp.clip(ATTN_SUBBLOCKS * i - 1 + j, 0, nb - 1), 0))

        kv_specs = [kv_spec(j) for j in range(ATTN_SUBBLOCKS + 2)]
        in_specs = ([pl.BlockSpec((1, tq, ATTN_WIDTH), lambda b, i: (b, i, 0))] + kv_specs + kv_specs
                    + [ctx_spec, ctx_spec, sink_spec])
        args = (q,) + (k,) * len(kv_specs) + (v,) * len(kv_specs) + (k_ctx, v_ctx, sink_rep)
    else:
        tq = seq
        in_specs = [pl.BlockSpec((1, tq, ATTN_WIDTH), lambda b, i: (b, i, 0)), ctx_spec, ctx_spec, sink_spec]
        args = (q, k_ctx, v_ctx, sink_rep)
    return pl.pallas_call(
        functools.partial(_attn_kernel, window=window),
        grid=(bsz, seq // tq),
        in_specs=in_specs,
        out_specs=pl.BlockSpec((1, tq, ATTN_WIDTH), lambda b, i: (b, i, 0)),
        out_shape=jax.ShapeDtypeStruct((bsz, seq, ATTN_WIDTH), BF16),
        compiler_params=_cparams("arbitrary", "arbitrary"),
        name="window_attention" if window else "context_attention",
    )(*args)


def _s5_kernel(uf_ref, ub_ref, h0_ref, mask_ref, bst_ref, a_ref, cw_ref,
               yf_ref, yb_ref, hfin_ref, lhs_scr, bu_scr, hs_scr, h_scr, *, bsz, tc):
    i = pl.program_id(0)

    @pl.when(i == 0)
    def _():
        h_scr[...] = h0_ref[...]

    mask = mask_ref[...]
    n_chain = 2 * bsz

    for d, u_ref in enumerate((uf_ref, ub_ref)):
        for b in range(bsz):
            for j in range(tc // 2):
                pair = [jnp.broadcast_to(u_ref[b, 2 * j + k:2 * j + k + 1, :], (SUBLANES, S5_WIDTH)) * mask
                        for k in range(2)]
                lhs_scr[d * bsz + b, 2 * SUBLANES * j:2 * SUBLANES * (j + 1), :] = (
                    jnp.concatenate(pair, axis=0).astype(BF16))
    for c in range(n_chain):
        bu_scr[c] = _dot(lhs_scr[c], bst_ref[c // bsz])

    a_re = [a_ref[d, 0] for d in range(2)]
    a_im = [a_ref[d, 1] for d in range(2)]

    def step(t, carry):
        new = []
        for c in range(n_chain):
            d = c // bsz
            tt = t if d == 0 else tc - 1 - t
            r0 = pl.multiple_of(tt * SUBLANES, SUBLANES)
            h_re, h_im = carry[2 * c], carry[2 * c + 1]
            n_re = a_re[d] * h_re - a_im[d] * h_im + bu_scr[c, pl.ds(r0, SUBLANES), 0:LANES]
            n_im = a_re[d] * h_im + a_im[d] * h_re + bu_scr[c, pl.ds(r0, SUBLANES), LANES:2 * LANES]
            hs_scr[c, 0, pl.ds(r0, SUBLANES), :] = n_re
            hs_scr[c, 1, pl.ds(r0, SUBLANES), :] = n_im
            new += [n_re, n_im]
        return tuple(new)

    init = []
    for c in range(n_chain):
        init += [h_scr[c, :, 0:LANES], h_scr[c, :, LANES:2 * LANES]]
    fin = lax.fori_loop(0, tc, step, tuple(init), unroll=8)
    for c in range(n_chain):
        h_scr[c, :, 0:LANES] = fin[2 * c]
        h_scr[c, :, LANES:2 * LANES] = fin[2 * c + 1]
    hfin_ref[...] = h_scr[...]

    for d, y_ref in enumerate((yf_ref, yb_ref)):
        for b in range(bsz):
            c = d * bsz + b
            parts = [hs_scr[c, ri, pl.ds(s, tc, stride=SUBLANES), :].astype(BF16)
                     for s in range(SUBLANES) for ri in range(2)]
            y_ref[b] = _dot(jnp.concatenate(parts, axis=-1), cw_ref[d])


def _s5_scan(u, h0, mask, bst, a_tiles, cwide, *, tc):
    bsz, seq, _ = u.shape
    nch = seq // tc
    full = lambda shape: pl.BlockSpec(shape, lambda i: (0,) * len(shape))
    fwd = pl.BlockSpec((bsz, tc, S5_WIDTH), lambda i: (0, i, 0))
    bwd = pl.BlockSpec((bsz, tc, S5_WIDTH), lambda i: (0, nch - 1 - i, 0))
    state = (2 * bsz, SUBLANES, 2 * LANES)
    rows = SUBLANES * tc
    return pl.pallas_call(
        functools.partial(_s5_kernel, bsz=bsz, tc=tc),
        grid=(nch,),
        in_specs=[fwd, bwd, full(state), full(mask.shape), full(bst.shape), full(a_tiles.shape),
                  full(cwide.shape)],
        out_specs=[fwd, bwd, full(state)],
        out_shape=[jax.ShapeDtypeStruct(u.shape, F32), jax.ShapeDtypeStruct(u.shape, F32),
                   jax.ShapeDtypeStruct(state, F32)],
        scratch_shapes=[pltpu.VMEM((2 * bsz, rows, S5_WIDTH), BF16),
                        pltpu.VMEM((2 * bsz, rows, 2 * LANES), F32),
                        pltpu.VMEM((2 * bsz, 2, rows, LANES), F32),
                        pltpu.VMEM(state, F32)],
        compiler_params=_cparams("arbitrary"),
        name="s5_scan",
    )(u, u, h0, mask, bst, a_tiles, cwide)


def _s5_mask():
    m = np.arange(S5_WIDTH)[None, :] // CH_PER_CHUNK == np.arange(SUBLANES)[:, None]
    return jnp.asarray(m.astype(np.float32), F32)


def _s5_params(lam_re, lam_im, log_dt, b_re, b_im, c_re, c_im):
    dt = jnp.exp(log_dt)[..., None]
    mag = jnp.exp(lam_re * dt)
    l_re = mag * jnp.cos(lam_im * dt)
    l_im = mag * jnp.sin(lam_im * dt)
    den = lam_re * lam_re + lam_im * lam_im
    f_re = ((l_re - 1.0) * lam_re + l_im * lam_im) / den
    f_im = (l_im * lam_re - (l_re - 1.0) * lam_im) / den
    bb_re = f_re[..., None] * b_re - f_im[..., None] * b_im
    bb_im = f_re[..., None] * b_im + f_im[..., None] * b_re
    a_tiles = jnp.stack([l_re.reshape(2, SUBLANES, LANES), l_im.reshape(2, SUBLANES, LANES)], axis=1)

    half = S5_GROUPS // SUBLANES
    eye = jnp.eye(half, dtype=F32)

    def in_mat(bb):
        t = bb.reshape(2, SUBLANES, half, S5_STATE, S5_GROUP)
        m = jnp.einsum('dsgpc,gh->dsgchp', t, eye)
        return m.reshape(2, S5_WIDTH, half * S5_STATE)

    def out_mat(cc):
        t = cc.reshape(2, SUBLANES, half, S5_GROUP, S5_STATE)
        m = jnp.einsum('dsgcp,st,gh->dsgpthc', t, jnp.eye(SUBLANES, dtype=F32), eye)
        return m.reshape(2, SUBLANES, half * S5_STATE, S5_WIDTH)

    bst = jnp.concatenate([in_mat(bb_re), in_mat(bb_im)], axis=-1).astype(BF16)
    cwide = jnp.stack([out_mat(c_re), out_mat(-c_im)], axis=2)
    cwide = cwide.reshape(2, 2 * SUBLANES * LANES, S5_WIDTH).astype(BF16)
    return bst, a_tiles, cwide


def _conv_kernel(prev_ref, cur_ref, next_ref, wdw_ref, bdw_ref, g_ref, b_ref, wpw_ref, bpw_ref,
                 o_ref, win_ref, *, tm):
    i = pl.program_id(1)
    nt = pl.num_programs(1)
    zero = jnp.zeros((CONV_HALO, CONV_WIDTH), F32)
    win_ref[0:CONV_HALO] = jnp.where(i > 0, prev_ref[0], zero)
    win_ref[CONV_HALO:CONV_HALO + tm] = cur_ref[0]
    win_ref[CONV_HALO + tm:] = jnp.where(i < nt - 1, next_ref[0], zero)
    acc = jnp.zeros((tm, CONV_WIDTH), F32) + bdw_ref[...]
    for r in range(SUBLANES):
        z = win_ref[0:tm + SUBLANES, :] * wdw_ref[r:r + 1, :]
        for a in range(1, (CONV_K + 1) // SUBLANES):
            j = a * SUBLANES
            z = z + win_ref[j:j + tm + SUBLANES, :] * wdw_ref[j + r:j + r + 1, :]
        acc = acc + z[r:r + tm]
    h = _silu(_layer_norm(acc) * g_ref[...] + b_ref[...])
    o_ref[0] = (_dot(h.astype(BF16), wpw_ref[...]) + bpw_ref[...]).astype(BF16)


def _conformer_conv(cg, w_dw, b_dw, ln_g, ln_b, w_pw_bf, b_pw, *, tm):
    bsz, seq, _ = cg.shape
    hb = tm // CONV_HALO
    last = seq // CONV_HALO - 1
    row = lambda a: a.reshape(1, CONV_WIDTH)
    vec = pl.BlockSpec((1, CONV_WIDTH), lambda b, i: (0, 0))
    return pl.pallas_call(
        functools.partial(_conv_kernel, tm=tm),
        grid=(bsz, seq // tm),
        in_specs=[
            pl.BlockSpec((1, CONV_HALO, CONV_WIDTH), lambda b, i: (b, jnp.maximum(i * hb - 1, 0), 0)),
            pl.BlockSpec((1, tm, CONV_WIDTH), lambda b, i: (b, i, 0)),
            pl.BlockSpec((1, CONV_HALO, CONV_WIDTH), lambda b, i: (b, jnp.minimum((i + 1) * hb, last), 0)),
            pl.BlockSpec((CONV_K + 1, CONV_WIDTH), lambda b, i: (0, 0)),
            vec, vec, vec,
            pl.BlockSpec((CONV_WIDTH, CONV_WIDTH), lambda b, i: (0, 0)),
            vec,
        ],
        out_specs=pl.BlockSpec((1, tm, CONV_WIDTH), lambda b, i: (b, i, 0)),
        out_shape=jax.ShapeDtypeStruct((bsz, seq, CONV_WIDTH), BF16),
        scratch_shapes=[pltpu.VMEM((tm + 2 * CONV_HALO, CONV_WIDTH), F32)],
        compiler_params=_cparams("arbitrary", "arbitrary"),
        name="conformer_conv",
    )(cg, cg, cg, jnp.pad(w_dw.reshape(CONV_K, CONV_WIDTH), ((1, 0), (0, 0))), row(b_dw), row(ln_g),
      row(ln_b), w_pw_bf, row(b_pw))


def _mixout_kernel(attn_ref, yf_ref, yb_ref, u_ref, conv_ref, x_ref,
                   dskip_ref, wglu_ref, bglu_ref, wout_ref, bout_ref,
                   gate_ref, g1_ref, b1_ref, shift_ref, scale_ref, wr_hi_ref, wr_lo_ref,
                   xmid_ref, h2_ref, logit_ref, *, mod_row):
    gate = _mod_vec(gate_ref, mod_row)
    shift = _mod_vec(shift_ref, mod_row)
    scale1 = 1.0 + _mod_vec(scale_ref, mod_row)
    tm = x_ref.shape[1]
    chunk = min(tm, MIX_CHUNK)
    for r0 in range(0, tm, chunk):
        rows = slice(r0, r0 + chunk)
        y = dskip_ref[...] * u_ref[0, rows, :] + yf_ref[0, rows, :] + yb_ref[0, rows, :]
        z = _gelu_tanh(y)
        s5 = z * _sigmoid(_dot(z.astype(BF16), wglu_ref[...]) + bglu_ref[...])
        y_mix = (_dot(attn_ref[0, rows, :], wout_ref[0:ATTN_WIDTH, :])
                 + _dot(s5.astype(BF16), wout_ref[ATTN_WIDTH:ATTN_WIDTH + S5_WIDTH, :])
                 + _dot(conv_ref[0, rows, :], wout_ref[ATTN_WIDTH + S5_WIDTH:, :])
                 + bout_ref[...])
        r = DEEPNORM_ALPHA * x_ref[0, rows, :] + gate * y_mix
        x_mid = _layer_norm(r) * g1_ref[...] + b1_ref[...]
        xmid_ref[0, rows, :] = x_mid
        h2 = _layer_norm(x_mid) * scale1 + shift
        h2_ref[0, rows, :] = h2.astype(BF16)
        h_hi, h_lo = _split_bf16(h2)
        logit_ref[0, :, rows] = (_dot_nt(wr_hi_ref[...], h_hi)
                                 + (_dot_nt(wr_hi_ref[...], h_lo) + _dot_nt(wr_lo_ref[...], h_hi)))


def _mixer_output(attn, yf, yb, u, conv, x, d_skip, w_glu_bf, b_glu, w_out_bf, b_out,
                  mods, ln_g, ln_b, wr_hi, wr_lo, *, layer, mod_row, tm):
    bsz, seq, _ = x.shape
    tok = lambda w: pl.BlockSpec((1, tm, w), lambda b, i: (b, i, 0))
    full = lambda r, c: pl.BlockSpec((r, c), lambda b, i: (0, 0))
    gate1, shift2, scale2 = mods, mods, mods
    row = lambda a: a.reshape(1, -1)
    return pl.pallas_call(
        functools.partial(_mixout_kernel, mod_row=mod_row),
        grid=(bsz, seq // tm),
        in_specs=[
            tok(ATTN_WIDTH), tok(S5_WIDTH), tok(S5_WIDTH), tok(S5_WIDTH), tok(CONV_WIDTH), tok(D_MODEL),
            full(1, S5_WIDTH), full(S5_WIDTH, S5_WIDTH), full(1, S5_WIDTH),
            full(D_MODEL, D_MODEL), full(1, D_MODEL),
            _mod_spec(layer, 2), full(1, D_MODEL), full(1, D_MODEL), _mod_spec(layer, 3), _mod_spec(layer, 4),
            full(LANES, D_MODEL), full(LANES, D_MODEL),
        ],
        out_specs=[tok(D_MODEL), tok(D_MODEL), pl.BlockSpec((1, LANES, tm), lambda b, i: (b, 0, i))],
        out_shape=[
            jax.ShapeDtypeStruct((bsz, seq, D_MODEL), F32),
            jax.ShapeDtypeStruct((bsz, seq, D_MODEL), BF16),
            jax.ShapeDtypeStruct((bsz, LANES, seq), F32),
        ],
        compiler_params=_cparams("arbitrary", "arbitrary"),
        name="mixer_output",
    )(attn, yf, yb, u, conv, x, row(d_skip), w_glu_bf, row(b_glu), w_out_bf, row(b_out),
      gate1, row(ln_g), row(ln_b), shift2, scale2, wr_hi, wr_lo)


def _token_cumsum(m, tri, ones):
    nblk = m.shape[0]
    m2 = m.reshape(nblk * N_EXPERTS, LANES).astype(BF16)
    within = _dot(m2, tri).reshape(nblk, N_EXPERTS, LANES)
    tot = _dot(m2, ones).reshape(nblk, N_EXPERTS, LANES)
    offs = []
    run = jnp.zeros((N_EXPERTS, LANES), F32)
    for j in range(nblk):
        offs.append(run)
        run = run + tot[j]
    off = jnp.stack(offs, axis=0)
    return within + off, off, tot


def _router_kernel(logit_ref, tri_ref, ones_ref, aff_ref, sel_ref, pos_ref, off_ref, cnt_ref, *, cap, slot0):
    b = pl.program_id(0)
    nblk = aff_ref.shape[1]

    def soft(j, carry):
        r0 = pl.multiple_of(j * LANES, LANES)
        t = logit_ref[0, 0:N_EXPERTS, pl.ds(r0, LANES)]
        ex = jnp.exp(t - jnp.max(t, axis=0, keepdims=True))
        aff_ref[0, j] = ex / jnp.sum(ex, axis=0, keepdims=True)
        return carry

    lax.fori_loop(0, nblk, soft, 0, unroll=min(nblk, 4))
    aff = aff_ref[0]

    def enough(cand):
        cnt = jnp.sum(jnp.where(aff >= cand[None], 1.0, 0.0), axis=0)
        return jnp.sum(cnt, axis=-1, keepdims=True) >= cap

    p = jnp.full((N_EXPERTS, LANES), 2.0, F32)
    for k in range(6, -1, -1):
        cand = p * (2.0 ** -(2 ** k))
        p = jnp.where(enough(cand), p, cand)
    thr = 0.5 * p
    thr = jnp.where(enough(thr), thr, 0.0)

    def refine(_, carry):
        lo, step = carry
        cand = lo + step
        return jnp.where(enough(cand), cand, lo), 0.5 * step

    thr, _ = lax.fori_loop(0, 23, refine, (thr, 0.5 * thr))
    gt = aff > thr[None]
    eq = aff == thr[None]
    n_gt = jnp.sum(jnp.sum(jnp.where(gt, 1.0, 0.0), axis=0), axis=-1, keepdims=True)
    need = cap - n_gt
    tri = tri_ref[...]
    ones = ones_ref[...]
    cum_eq, _, _ = _token_cumsum(jnp.where(eq, 1.0, 0.0), tri, ones)
    sel = jnp.where(gt | (eq & (cum_eq <= need[None])), 1.0, 0.0)
    cum_sel, off, cnt = _token_cumsum(sel, tri, ones)
    base = (slot0 + b * cap).astype(F32)
    sel_ref[0] = sel
    pos_ref[0] = cum_sel - sel + base
    off_ref[0] = off + base
    cnt_ref[0] = cnt


def _router(logits, *, slot0):
    bsz, _, n = logits.shape
    nblk = n // LANES
    cap = CAPACITY_FACTOR * n // N_EXPERTS
    idx = np.arange(LANES)
    tri = jnp.asarray((idx[:, None] <= idx[None, :]).astype(np.float32), BF16)
    ones = jnp.ones((LANES, LANES), BF16)
    shape = (bsz, nblk, N_EXPERTS, LANES)
    out = pl.BlockSpec((1, nblk, N_EXPERTS, LANES), lambda b: (b, 0, 0, 0))
    sq = pl.BlockSpec((LANES, LANES), lambda b: (0, 0))
    return pl.pallas_call(
        functools.partial(_router_kernel, cap=cap, slot0=slot0),
        grid=(bsz,),
        in_specs=[pl.BlockSpec((1, LANES, n), lambda b: (b, 0, 0)), sq, sq],
        out_specs=[out] * 5,
        out_shape=[jax.ShapeDtypeStruct(shape, F32)] * 5,
        compiler_params=_cparams("arbitrary"),
        name="router",
    )(logits, tri, ones)


COMBINE_KW = LANES + 16
SMALL_COUNT = 32
COMBINE_SMALL_KW = SMALL_COUNT + 16
FF_TILE = 256


def _dispatch_blocks(tb_ref, n_entries, e, blocks, xs_buf, g_buf):
    most = tb_ref[n_entries + blocks[0][0] * N_EXPERTS + e]
    for tile, _ in blocks[1:]:
        most = jnp.maximum(most, tb_ref[n_entries + tile * N_EXPERTS + e])

    def run(kw):
        k = lax.broadcasted_iota(jnp.int32, (kw, LANES), 0).astype(F32)
        for tile, load in blocks:
            h2_blk, pos_row, sel_row, aff_row = load()
            base = tb_ref[tile * N_EXPERTS + e]
            base16 = pl.multiple_of(lax.shift_left(lax.shift_right_logical(base, 4), 4), 16)
            hit = (k == (pos_row - base16.astype(F32))) & (sel_row > 0.5)
            sel_t = jnp.where(hit, 1.0, 0.0).astype(BF16)
            xs_buf[pl.ds(base16, kw), :] += _dot(sel_t, h2_blk).astype(BF16)
            g = jnp.sum(jnp.where(hit, aff_row, 0.0), axis=-1, keepdims=True)
            g_buf[pl.ds(base16, kw), :] += jnp.broadcast_to(g, (kw, LANES))

    @pl.when(most <= SMALL_COUNT)
    def _():
        run(COMBINE_SMALL_KW)

    @pl.when(most > SMALL_COUNT)
    def _():
        run(COMBINE_KW)


def _moe_kernel(*refs, n_ctx, rows, n_entries):
    tb_ref = refs[0]
    if n_ctx:
        (posx_ref, selx_ref, affx_ref, h2x_ref, posc_ref, selc_ref, affc_ref, h2c_ref,
         wg_ref, wu_ref, wd_ref, y_ref, acc_scr, g_scr, xs_scr) = refs[1:]
    else:
        (posx_ref, selx_ref, affx_ref, h2x_ref, wg_ref, wu_ref, wd_ref, y_ref,
         acc_scr, g_scr, xs_scr) = refs[1:]
    p = pl.program_id(0)
    s = pl.program_id(1)
    last = pl.num_programs(1) - 1
    fill = lax.rem(p, 2)
    use = 1 - fill
    blk_per_step = posx_ref.shape[2]

    @pl.when(p < N_EXPERTS)
    def _():
        xs_buf = xs_scr.at[fill]
        g_buf = g_scr.at[fill]

        @pl.when(s == 0)
        def _():
            xs_buf[...] = jnp.zeros(xs_buf.shape, BF16)
            g_buf[...] = jnp.zeros(g_buf.shape, F32)

        if n_ctx:
            @pl.when(s == 0)
            def _():
                bsz_c, blk_c = posc_ref.shape[0], posc_ref.shape[2]

                def load_c(b, i):
                    return lambda: (h2c_ref[b, i * LANES:(i + 1) * LANES, :], posc_ref[b, 0, i:i + 1, :],
                                    selc_ref[b, 0, i:i + 1, :], affc_ref[b, 0, i:i + 1, :])

                blocks = [((last + 1) * blk_per_step + b * blk_c + i, load_c(b, i))
                          for b in range(bsz_c) for i in range(blk_c)]
                _dispatch_blocks(tb_ref, n_entries, p, blocks, xs_buf, g_buf)

        def load_x(i):
            return lambda: (h2x_ref[0, i * LANES:(i + 1) * LANES, :], posx_ref[0, 0, i:i + 1, :],
                            selx_ref[0, 0, i:i + 1, :], affx_ref[0, 0, i:i + 1, :])

        blocks = [(s * blk_per_step + i, load_x(i)) for i in range(blk_per_step)]
        _dispatch_blocks(tb_ref, n_entries, p, blocks, xs_buf, g_buf)

    @pl.when(p >= 1)
    def _():
        wg = wg_ref[0, 0].astype(BF16)
        wu = wu_ref[0, 0].astype(BF16)
        wd = wd_ref[0, 0].astype(BF16)
        half = rows // 2
        for r0 in (0, half):
            xs = xs_scr[use, r0:r0 + half, :]
            hid = _silu(_dot(xs, wg)) * _dot(xs, wu)
            part = _dot(hid.astype(BF16), wd)
            acc_scr[r0:r0 + half, :] = jnp.where(s == 0, part, acc_scr[r0:r0 + half, :] + part)

        @pl.when(s == last)
        def _():
            g = g_scr[use, 0:rows, :]
            gated = acc_scr[...] * jnp.concatenate([g] * (D_MODEL // LANES), axis=-1)
            y_ref[0, 0:rows, :] = gated.astype(BF16)
            y_ref[0, rows:, :] = jnp.zeros((y_ref.shape[1] - rows, D_MODEL), BF16)


def _moe_experts(tb, route_x, h2x, route_c, h2c, w_gate, w_up, w_down, *, layer):
    bsz, n, d = h2x.shape
    n_ff = EXPERT_FF // FF_TILE
    x_step = bsz * n // n_ff
    steps_per_b = n // x_step
    blk_per_step = x_step // LANES
    n_ctx = 0 if h2c is None else h2c.shape[0] * h2c.shape[1]
    rows = bsz * CAPACITY_FACTOR * n // N_EXPERTS + CAPACITY_FACTOR * n_ctx // N_EXPERTS
    rows_pad = rows + COMBINE_KW
    last_e = N_EXPERTS - 1

    def xs_idx(p, s):
        sc = jnp.where(p > last_e, n_ff - 1, s)
        return sc // steps_per_b, sc % steps_per_b

    def route_x_spec():
        return pl.BlockSpec((1, 1, blk_per_step, LANES),
                            lambda p, s, tb: (xs_idx(p, s)[0], jnp.minimum(p, last_e), xs_idx(p, s)[1], 0))

    in_specs = [route_x_spec(), route_x_spec(), route_x_spec(),
                pl.BlockSpec((1, x_step, d), lambda p, s, tb: (xs_idx(p, s)[0], xs_idx(p, s)[1], 0))]
    args = list(route_x) + [h2x]
    if n_ctx:
        bc, nc, _ = h2c.shape
        rc = pl.BlockSpec((bc, 1, nc // LANES, LANES), lambda p, s, tb: (0, jnp.minimum(p, last_e), 0, 0))
        in_specs += [rc, rc, rc, pl.BlockSpec((bc, nc, d), lambda p, s, tb: (0, 0, 0))]
        args += list(route_c) + [h2c]
    ffn_e = lambda p: jnp.maximum(p - 1, 0)
    in_specs += [pl.BlockSpec((1, 1, d, FF_TILE), lambda p, s, tb: (layer, ffn_e(p), 0, s)),
                 pl.BlockSpec((1, 1, d, FF_TILE), lambda p, s, tb: (layer, ffn_e(p), 0, s)),
                 pl.BlockSpec((1, 1, FF_TILE, d), lambda p, s, tb: (layer, ffn_e(p), s, 0))]
    args += [w_gate, w_up, w_down]
    return pl.pallas_call(
        functools.partial(_moe_kernel, n_ctx=n_ctx, rows=rows, n_entries=tb.shape[0] // 2),
        grid_spec=pltpu.PrefetchScalarGridSpec(
            num_scalar_prefetch=1,
            grid=(N_EXPERTS + 1, n_ff),
            in_specs=in_specs,
            out_specs=pl.BlockSpec((1, rows_pad, d), lambda p, s, tb: (ffn_e(p), 0, 0)),
            scratch_shapes=[pltpu.VMEM((rows, d), F32), pltpu.VMEM((2, rows_pad, LANES), F32),
                            pltpu.VMEM((2, rows_pad, d), BF16)],
        ),
        out_shape=jax.ShapeDtypeStruct((N_EXPERTS, rows_pad, d), BF16),
        compiler_params=_cparams("arbitrary", "arbitrary"),
        name="moe_experts",
    )(tb, *args)


def _combine_kernel(tb_ref, pos_ref, sel_ref, xmid_ref, gate_ref, g_ref, b_ref, y_hbm, o_ref,
                    y_scr, stack_scr, sem, *, cap, slot0, blk_per_step, nblk, n_entries, mod_row):
    b = pl.program_id(0)
    j = pl.program_id(1)
    win = y_scr.shape[1]
    gate = _mod_vec(gate_ref, mod_row)

    @pl.when(j == 0)
    def _():
        row0 = pl.multiple_of(slot0 + b * cap, 16)
        cp = pltpu.make_async_copy(y_hbm.at[:, pl.ds(row0, win), :], y_scr, sem)
        cp.start()
        cp.wait()

    set_base = (slot0 + b * cap).astype(F32)
    tn_dims = (((0,), (0,)), ((), ()))
    for i in range(blk_per_step):
        entry = (b * nblk + j * blk_per_step + i) * N_EXPERTS
        rows = slice(i * LANES, (i + 1) * LANES)

        def hits(e, kw, entry=entry, i=i):
            base = tb_ref[entry + e] - (slot0 + b * cap)
            base16 = pl.multiple_of(lax.shift_left(lax.shift_right_logical(base, 4), 4), 16)
            k = lax.broadcasted_iota(jnp.int32, (kw, LANES), 0).astype(F32)
            rel = pos_ref[0, i, e:e + 1, :] - (set_base + base16.astype(F32))
            hit = (k == rel) & (sel_ref[0, i, e:e + 1, :] > 0.5)
            return base16, jnp.where(hit, 1.0, 0.0).astype(BF16)

        def finish(moe, rows=rows):
            r = DEEPNORM_ALPHA * xmid_ref[0, rows, :] + gate * moe
            o_ref[0, rows, :] = _layer_norm(r) * g_ref[...] + b_ref[...]

        most = tb_ref[n_entries + entry]
        for e in range(1, N_EXPERTS):
            most = jnp.maximum(most, tb_ref[n_entries + entry + e])

        @pl.when(most <= SMALL_COUNT)
        def _():
            sel_all = []
            for e in range(N_EXPERTS):
                base16, sel_t = hits(e, COMBINE_SMALL_KW)
                sel_all.append(sel_t)
                stack_scr[e * COMBINE_SMALL_KW:(e + 1) * COMBINE_SMALL_KW, :] = (
                    y_scr[e, pl.ds(base16, COMBINE_SMALL_KW), :])
            finish(lax.dot_general(jnp.concatenate(sel_all, axis=0), stack_scr[...], tn_dims,
                                   preferred_element_type=F32))

        @pl.when(most > SMALL_COUNT)
        def _():
            acc = jnp.zeros((LANES, D_MODEL), F32)
            for e in range(N_EXPERTS):
                base16, sel_t = hits(e, COMBINE_KW)
                acc = acc + lax.dot_general(sel_t, y_scr[e, pl.ds(base16, COMBINE_KW), :], tn_dims,
                                            preferred_element_type=F32)
            finish(acc)


def _combine_post(tb, pos, sel, x_mid, mods, ln_g, ln_b, y, *, layer, mod_row, slot0, tm):
    bsz, n, d = x_mid.shape
    nblk = n // LANES
    cap = CAPACITY_FACTOR * n // N_EXPERTS
    blk_per_step = tm // LANES
    tok = pl.BlockSpec((1, tm, d), lambda b, j, tb: (b, j, 0))
    route = pl.BlockSpec((1, blk_per_step, N_EXPERTS, LANES), lambda b, j, tb: (b, j, 0, 0))
    vec = pl.BlockSpec((1, d), lambda b, j, tb: (0, 0))
    return pl.pallas_call(
        functools.partial(_combine_kernel, cap=cap, slot0=slot0, blk_per_step=blk_per_step, nblk=nblk,
                          n_entries=bsz * nblk * N_EXPERTS, mod_row=mod_row),
        grid_spec=pltpu.PrefetchScalarGridSpec(
            num_scalar_prefetch=1,
            grid=(bsz, n // tm),
            in_specs=[route, route, tok, _mod_spec(layer, 5), vec, vec,
                      pl.BlockSpec(memory_space=pl.ANY)],
            out_specs=tok,
            scratch_shapes=[pltpu.VMEM((N_EXPERTS, cap + COMBINE_KW, d), BF16),
                            pltpu.VMEM((N_EXPERTS * COMBINE_SMALL_KW, d), BF16),
                            pltpu.SemaphoreType.DMA(())],
        ),
        out_shape=jax.ShapeDtypeStruct(x_mid.shape, F32),
        compiler_params=_cparams("arbitrary", "arbitrary"),
        name="combine_post",
    )(tb, pos, sel, x_mid, mods, ln_g.reshape(1, d), ln_b.reshape(1, d), y)


def _tile_table(*lane_replicated):
    return jnp.concatenate([a[..., 0].astype(jnp.int32).reshape(-1) for a in lane_replicated])


def _expert_major(a):
    return jnp.swapaxes(a, 1, 2)


def kernel(x, c, ctx, c_ctx, w_mod, b_mod, w_in, b_in, attn_sink, s5_lam_re, s5_lam_im, s5_log_dt,
           s5_b_re, s5_b_im, s5_c_re, s5_c_im, s5_d, s5_w_glu, s5_b_glu, conv_w_dw, conv_b_dw,
           conv_ln_g, conv_ln_b, conv_w_pw, conv_b_pw, w_out, b_out, ln1_g, ln1_b, w_router,
           exp_w_gate, exp_w_up, exp_w_down, ln2_g, ln2_b):
    bsz, seq, d = x.shape
    lc = ctx.shape[1]
    tm_x, tm_c = min(1024, seq), lc

    cond = jnp.zeros((SUBLANES, d), F32).at[:bsz].set(c).at[bsz].set(c_ctx)
    mods = _modulation(cond, w_mod, b_mod)
    cos_t, sin_t = _rope_tables(seq)
    s5_mask = _s5_mask()
    zero_state = jnp.zeros((2 * bsz, SUBLANES, 2 * LANES), F32)

    xc = ctx
    ctx_row = bsz
    for l in range(DEPTH):
        last = l == DEPTH - 1
        w_in_bf = w_in[l].astype(BF16)
        w_out_bf = w_out[l].astype(BF16)
        w_glu_bf = s5_w_glu[l].astype(BF16)
        w_pw_bf = conv_w_pw[l].astype(BF16)
        wr = jnp.pad(w_router[l].T, ((0, LANES - N_EXPERTS), (0, 0)))
        wr_hi, wr_lo = _split_bf16(wr)
        sink_rep = jnp.broadcast_to(attn_sink[l][:, None], (N_Q_HEADS, LANES))
        bst, a_tiles, cwide = _s5_params(s5_lam_re[l], s5_lam_im[l], s5_log_dt[l], s5_b_re[l], s5_b_im[l],
                                         s5_c_re[l], s5_c_im[l])

        q, k, v, u, cg = _in_projection(x, mods, w_in_bf, b_in[l], cos_t, sin_t,
                                        layer=l, mod_row=None, rope=True, tm=tm_x)
        q_c, k_c, v_c, u_c, cg_c = _in_projection(xc, mods, w_in_bf, b_in[l], cos_t, sin_t,
                                                  layer=l, mod_row=ctx_row, rope=False, tm=tm_c)

        attn_x = _attention(q, k, v, k_c, v_c, sink_rep, window=True)
        yf_c, yb_c, h_ctx = _s5_scan(u_c, zero_state, s5_mask, bst, a_tiles, cwide, tc=lc)
        yf, yb, _ = _s5_scan(u, h_ctx, s5_mask, bst, a_tiles, cwide, tc=256)
        conv_args = (conv_w_dw[l], conv_b_dw[l], conv_ln_g[l], conv_ln_b[l], w_pw_bf, conv_b_pw[l])
        conv_x = _conformer_conv(cg, *conv_args, tm=256)
        mix_args = (s5_d[l], w_glu_bf, s5_b_glu[l], w_out_bf, b_out[l], mods, ln1_g[l], ln1_b[l], wr_hi, wr_lo)
        x_mid, h2, logits = _mixer_output(attn_x, yf, yb, u, conv_x, x, *mix_args,
                                          layer=l, mod_row=None, tm=tm_x)
        aff_x, sel_x, pos_x, off_x, cnt_x = _router(logits, slot0=0)
        route_x = tuple(_expert_major(a) for a in (pos_x, sel_x, aff_x))
        tb_x = _tile_table(off_x, cnt_x)
        if not last:
            attn_c = _attention(q_c, None, None, k_c, v_c, sink_rep, window=False)
            conv_c = _conformer_conv(cg_c, *conv_args, tm=lc)
            xc_mid, hc2, logits_c = _mixer_output(attn_c, yf_c, yb_c, u_c, conv_c, xc, *mix_args,
                                                  layer=l, mod_row=ctx_row, tm=tm_c)
            slot0_c = bsz * CAPACITY_FACTOR * seq // N_EXPERTS
            aff_c, sel_c, pos_c, off_c, cnt_c = _router(logits_c, slot0=slot0_c)
            route_c = tuple(_expert_major(a) for a in (pos_c, sel_c, aff_c))
            tb_c = _tile_table(off_c, cnt_c)
            y = _moe_experts(_tile_table(off_x, off_c, cnt_x, cnt_c), route_x, h2, route_c, hc2,
                             exp_w_gate, exp_w_up, exp_w_down, layer=l)
            xc = _combine_post(tb_c, pos_c, sel_c, xc_mid, mods, ln2_g[l], ln2_b[l], y,
                               layer=l, mod_row=ctx_row, slot0=slot0_c, tm=lc)
        else:
            y = _moe_experts(tb_x, route_x, h2, None, None, exp_w_gate, exp_w_up, exp_w_down, layer=l)
        x = _combine_post(tb_x, pos_x, sel_x, x_mid, mods, ln2_g[l], ln2_b[l], y,
                          layer=l, mod_row=None, slot0=0, tm=256)
    return x
```

```python
import functools
import math

import jax
import jax.numpy as jnp
import numpy as np
from jax import lax
from jax.experimental import pallas as pl
from jax.experimental.pallas import tpu as pltpu

D_MODEL = 1024
DEPTH = 2
GRID_W = 64
HEAD_DIM = 64
ATTN_WIDTH = D_MODEL // 2
N_Q_HEADS = ATTN_WIDTH // HEAD_DIM
N_KV_HEADS = N_Q_HEADS // 4
Q_PER_KV = N_Q_HEADS // N_KV_HEADS
KV_WIDTH = N_KV_HEADS * HEAD_DIM
BLOCK = 128
ROPE_BASE = 10000.0
NEG_INF = -1e30
S5_WIDTH = D_MODEL // 4
S5_GROUP = 16
S5_GROUPS = S5_WIDTH // S5_GROUP
S5_STATE = 64
CONV_WIDTH = D_MODEL - ATTN_WIDTH - S5_WIDTH
CONV_K = 31
CONV_HALO = 16
Q_END = ATTN_WIDTH
K_END = Q_END + KV_WIDTH
V_END = K_END + KV_WIDTH
U_END = V_END + S5_WIDTH
IN_WIDTH = U_END + 2 * CONV_WIDTH
N_EXPERTS = 16
EXPERT_FF = 2 * D_MODEL
CAPACITY_FACTOR = 2
DEEPNORM_ALPHA = (2.0 * DEPTH) ** 0.25
LN_EPS = 1e-5

LANES = 128
SUBLANES = 8
S5_CHUNKS = S5_WIDTH * S5_STATE // S5_GROUP // LANES
CH_PER_CHUNK = S5_WIDTH // S5_CHUNKS
VMEM_LIMIT = 56 * 1024 * 1024
MIX_CHUNK = 256

F32 = jnp.float32
BF16 = jnp.bfloat16


def _cparams(*sem):
    return pltpu.CompilerParams(dimension_semantics=sem, vmem_limit_bytes=VMEM_LIMIT)


def _dot(a, b):
    return jnp.dot(a, b, preferred_element_type=F32)


def _dot_nt(a, b):
    return lax.dot_general(a, b, (((1,), (1,)), ((), ())), preferred_element_type=F32)


def _split_bf16(x):
    hi = x.astype(BF16)
    lo = (x - hi.astype(F32)).astype(BF16)
    return hi, lo


def _dot3(a, b_hi, b_lo):
    a_hi, a_lo = _split_bf16(a)
    return _dot(a_hi, b_hi) + (_dot(a_lo, b_hi) + _dot(a_hi, b_lo))


def _sigmoid(x):
    return 1.0 / (1.0 + jnp.exp(-x))


def _silu(x):
    return x * _sigmoid(x)


def _gelu_tanh(x):
    c = math.sqrt(2.0 / math.pi)
    return 0.5 * x * (1.0 + jnp.tanh(c * (x + 0.044715 * (x * x * x))))


def _layer_norm(x):
    mu = jnp.mean(x, axis=-1, keepdims=True)
    xc = x - mu
    var = jnp.mean(xc * xc, axis=-1, keepdims=True)
    return xc * lax.rsqrt(var + LN_EPS)


def _mod_kernel(c_ref, w_ref, b_ref, o_ref):
    s = _silu(c_ref[...])
    w = w_ref[0]
    w_hi, w_lo = _split_bf16(w)
    o_ref[0] = _dot3(s, w_hi, w_lo) + b_ref[0]


def _modulation(cond, w_mod, b_mod):
    tn = 1536
    n = w_mod.shape[-1]
    return pl.pallas_call(
        _mod_kernel,
        grid=(DEPTH, n // tn),
        in_specs=[
            pl.BlockSpec((SUBLANES, D_MODEL), lambda l, j: (0, 0)),
            pl.BlockSpec((1, D_MODEL, tn), lambda l, j: (l, 0, j)),
            pl.BlockSpec((1, 1, tn), lambda l, j: (l, 0, j)),
        ],
        out_specs=pl.BlockSpec((1, SUBLANES, tn), lambda l, j: (l, 0, j)),
        out_shape=jax.ShapeDtypeStruct((DEPTH, SUBLANES, n), F32),
        compiler_params=_cparams("arbitrary", "arbitrary"),
        name="modulation",
    )(cond, w_mod, b_mod.reshape(DEPTH, 1, n))


def _rope_chunk(x, cos, sin_signed):
    lane = lax.broadcasted_iota(jnp.int32, x.shape, 1)
    first = (lane % 32) < 16
    partner = jnp.where(first, pltpu.roll(x, LANES - 16, 1), pltpu.roll(x, 16, 1))
    return x * cos + partner * sin_signed


def _mod_spec(layer, k):
    return pl.BlockSpec((1, SUBLANES, D_MODEL), lambda *_: (layer, 0, k))


def _layer_spec(layer, *shape):
    return pl.BlockSpec((1,) + shape, lambda *_: (layer,) + (0,) * len(shape))


def _mod_vec(ref, mod_row):
    row = pl.program_id(0) if mod_row is None else mod_row
    return ref[0, pl.ds(row, 1), :]


def _inproj_kernel(x_ref, shift_ref, scale_ref, w_ref, b_ref, cos_ref, sin_ref,
                   q_ref, k_ref, v_ref, u_ref, cg_ref, *, rope, mod_row):
    shift = _mod_vec(shift_ref, mod_row)
    scale1 = 1.0 + _mod_vec(scale_ref, mod_row)
    tm = x_ref.shape[1]
    chunk = min(tm, MIX_CHUNK)
    scale = HEAD_DIM ** -0.5
    for r0 in range(0, tm, chunk):
        rows = slice(r0, r0 + chunk)
        h = _layer_norm(x_ref[0, rows, :]) * scale1 + shift
        p = _dot(h.astype(BF16), w_ref[0]) + b_ref[0]
        if rope:
            cos = cos_ref[rows, :]
            sin = sin_ref[rows, :]
        for j in range(ATTN_WIDTH // LANES):
            qc = p[:, j * LANES:(j + 1) * LANES]
            if rope:
                qc = _rope_chunk(qc, cos, sin)
            q_ref[0, rows, j * LANES:(j + 1) * LANES] = (qc * scale).astype(BF16)
        kc = p[:, Q_END:K_END]
        if rope:
            kc = _rope_chunk(kc, cos, sin)
        k_ref[0, rows, :] = kc.astype(BF16)
        v_ref[0, rows, :] = p[:, K_END:V_END].astype(BF16)
        u_ref[0, rows, :] = p[:, V_END:U_END]
        a = p[:, U_END:U_END + CONV_WIDTH]
        g = p[:, U_END + CONV_WIDTH:]
        cg_ref[0, rows, :] = a * _sigmoid(g)


def _in_projection(x, mods, w_in_bf, b_in, cos_t, sin_t, *, layer, mod_row, rope, tm):
    bsz, seq, _ = x.shape
    tok = lambda w: pl.BlockSpec((1, tm, w), lambda b, i: (b, i, 0))
    tab = pl.BlockSpec((tm, LANES), lambda b, i: (i, 0))
    return pl.pallas_call(
        functools.partial(_inproj_kernel, rope=rope, mod_row=mod_row),
        grid=(bsz, seq // tm),
        in_specs=[
            tok(D_MODEL), _mod_spec(layer, 0), _mod_spec(layer, 1),
            _layer_spec(layer, D_MODEL, IN_WIDTH), _layer_spec(layer, 1, IN_WIDTH),
            tab, tab,
        ],
        out_specs=[tok(ATTN_WIDTH), tok(KV_WIDTH), tok(KV_WIDTH), tok(S5_WIDTH), tok(CONV_WIDTH)],
        out_shape=[
            jax.ShapeDtypeStruct((bsz, seq, ATTN_WIDTH), BF16),
            jax.ShapeDtypeStruct((bsz, seq, KV_WIDTH), BF16),
            jax.ShapeDtypeStruct((bsz, seq, KV_WIDTH), BF16),
            jax.ShapeDtypeStruct((bsz, seq, S5_WIDTH), F32),
            jax.ShapeDtypeStruct((bsz, seq, CONV_WIDTH), F32),
        ],
        compiler_params=_cparams("arbitrary", "arbitrary"),
        name="in_projection",
    )(x, mods, mods, w_in_bf, b_in.reshape(DEPTH, 1, IN_WIDTH), cos_t, sin_t)


def _rope_tables(seq):
    f = HEAD_DIM // 4
    inv_freq = jnp.asarray(ROPE_BASE, F32) ** (-jnp.arange(f, dtype=F32) / f)

    def axis_tables(n_pos):
        ang = jnp.arange(n_pos, dtype=jnp.int32).astype(F32)[:, None] * inv_freq[None, :]
        cos, sin = jnp.cos(ang), jnp.sin(ang)
        return jnp.concatenate([cos, cos], axis=-1), jnp.concatenate([-sin, sin], axis=-1)

    n_rows = seq // GRID_W
    cos_r, sin_r = axis_tables(n_rows)
    cos_c, sin_c = axis_tables(GRID_W)
    by_row = lambda a: jnp.repeat(a, GRID_W, axis=0)
    by_col = lambda a: jnp.tile(a, (n_rows, 1))
    cos = jnp.concatenate([by_row(cos_r), by_col(cos_c)], axis=-1)
    sin = jnp.concatenate([by_row(sin_r), by_col(sin_c)], axis=-1)
    return jnp.tile(cos, (1, LANES // HEAD_DIM)), jnp.tile(sin, (1, LANES // HEAD_DIM))


ATTN_SUBBLOCKS = 4


def _attn_kernel(*refs, window):
    if window:
        n_kv = ATTN_SUBBLOCKS + 2
        q_ref = refs[0]
        k_refs = refs[1:1 + n_kv]
        v_refs = refs[1 + n_kv:1 + 2 * n_kv]
        kx_ref, vx_ref, sink_ref, o_ref = refs[1 + 2 * n_kv:]
        subs = ATTN_SUBBLOCKS
    else:
        q_ref, kx_ref, vx_ref, sink_ref, o_ref = refs
        subs = 1
    step = pl.program_id(1)
    n_steps = pl.num_programs(1)
    tq = q_ref.shape[1] // subs
    rows = Q_PER_KV * tq
    if window:
        row = lax.broadcasted_iota(jnp.int32, (rows, BLOCK), 0) % tq
        col = lax.broadcasted_iota(jnp.int32, (rows, BLOCK), 1)

    def with_ones(v, ks):
        return jnp.concatenate([v[:, ks], jnp.ones((v.shape[0], HEAD_DIM), BF16)], axis=-1)

    for sub in range(subs):
        q = q_ref[0, sub * tq:(sub + 1) * tq, :]
        if window:
            kp_ref, kc_ref, kn_ref = k_refs[sub:sub + 3]
            vp_ref, vc_ref, vn_ref = v_refs[sub:sub + 3]
            ok_prev = (col >= row) if sub > 0 else (col >= row) & (step > 0)
            ok_next = (col <= row) if sub < subs - 1 else (col <= row) & (step < n_steps - 1)
        outs = []
        for g in range(N_KV_HEADS):
            ks = slice(g * HEAD_DIM, (g + 1) * HEAD_DIM)
            heads = range(g * Q_PER_KV, (g + 1) * Q_PER_KV)
            qs = jnp.concatenate([q[:, h * HEAD_DIM:(h + 1) * HEAD_DIM] for h in heads], axis=0)
            sink = jnp.concatenate([jnp.broadcast_to(sink_ref[0, h:h + 1, 0:1], (tq, 1)) for h in heads], axis=0)
            lx = kx_ref.shape[1]
            s_x = _dot_nt(qs, kx_ref[0][:, ks])
            m_el = s_x[:, 0:LANES]
            for c in range(1, lx // LANES):
                m_el = jnp.maximum(m_el, s_x[:, c * LANES:(c + 1) * LANES])
            if window:
                s_p = jnp.where(ok_prev, _dot_nt(qs, kp_ref[0][:, ks]), NEG_INF)
                s_c = _dot_nt(qs, kc_ref[0][:, ks])
                s_n = jnp.where(ok_next, _dot_nt(qs, kn_ref[0][:, ks]), NEG_INF)
                m_el = jnp.maximum(jnp.maximum(m_el, s_c), jnp.maximum(s_p, s_n))
            m = jnp.maximum(jnp.max(m_el, axis=-1, keepdims=True), sink)
            acc = _dot(jnp.exp(s_x - m).astype(BF16), with_ones(vx_ref[0], ks))
            if window:
                for s_w, v_ref in ((s_p, vp_ref), (s_c, vc_ref), (s_n, vn_ref)):
                    acc = acc + _dot(jnp.exp(s_w - m).astype(BF16), with_ones(v_ref[0], ks))
            o = acc[:, 0:HEAD_DIM] / (acc[:, HEAD_DIM:] + jnp.exp(sink - m))
            outs += [o[i * tq:(i + 1) * tq] for i in range(Q_PER_KV)]
        o_ref[0, sub * tq:(sub + 1) * tq, :] = jnp.concatenate(outs, axis=-1).astype(BF16)


def _attention(q, k, v, k_ctx, v_ctx, sink_rep, *, layer, window):
    bsz, seq, _ = q.shape
    lc = k_ctx.shape[1]
    ctx_spec = pl.BlockSpec((1, lc, KV_WIDTH), lambda b, i: (b, 0, 0))
    sink_spec = _layer_spec(layer, N_Q_HEADS, LANES)
    if window:
        tq = BLOCK * ATTN_SUBBLOCKS
        nb = seq // BLOCK

        def kv_spec(j):
            return pl.BlockSpec((1, BLOCK, KV_WIDTH),
                                lambda b, i: (b, jnp.clip(ATTN_SUBBLOCKS * i - 1 + j, 0, nb - 1), 0))

        kv_specs = [kv_spec(j) for j in range(ATTN_SUBBLOCKS + 2)]
        in_specs = ([pl.BlockSpec((1, tq, ATTN_WIDTH), lambda b, i: (b, i, 0))] + kv_specs + kv_specs
                    + [ctx_spec, ctx_spec, sink_spec])
        args = (q,) + (k,) * len(kv_specs) + (v,) * len(kv_specs) + (k_ctx, v_ctx, sink_rep)
    else:
        tq = seq
        in_specs = [pl.BlockSpec((1, tq, ATTN_WIDTH), lambda b, i: (b, i, 0)), ctx_spec, ctx_spec, sink_spec]
        args = (q, k_ctx, v_ctx, sink_rep)
    return pl.pallas_call(
        functools.partial(_attn_kernel, window=window),
        grid=(bsz, seq // tq),
        in_specs=in_specs,
        out_specs=pl.BlockSpec((1, tq, ATTN_WIDTH), lambda b, i: (b, i, 0)),
        out_shape=jax.ShapeDtypeStruct((bsz, seq, ATTN_WIDTH), BF16),
        compiler_params=_cparams("arbitrary", "arbitrary"),
        name="window_attention" if window else "context_attention",
    )(*args)


def _s5_kernel(uf_ref, ub_ref, h0_ref, mask_ref, bst_ref, a_ref, cw_ref,
               yf_ref, yb_ref, hfin_ref, lhs_scr, bu_scr, hs_scr, h_scr, *, bsz, tc):
    i = pl.program_id(0)

    @pl.when(i == 0)
    def _():
        h_scr[...] = h0_ref[...]

    mask = mask_ref[...]
    n_chain = 2 * bsz

    for d, u_ref in enumerate((uf_ref, ub_ref)):
        for b in range(bsz):
            for j in range(tc // 2):
                pair = [jnp.broadcast_to(u_ref[b, 2 * j + k:2 * j + k + 1, :], (SUBLANES, S5_WIDTH)) * mask
                        for k in range(2)]
                lhs_scr[d * bsz + b, 2 * SUBLANES * j:2 * SUBLANES * (j + 1), :] = (
                    jnp.concatenate(pair, axis=0).astype(BF16))
    for c in range(n_chain):
        bu_scr[c] = _dot(lhs_scr[c], bst_ref[0, c // bsz])

    a_re = [a_ref[0, d, 0] for d in range(2)]
    a_im = [a_ref[0, d, 1] for d in range(2)]

    def step(t, carry):
        new = []
        for c in range(n_chain):
            d = c // bsz
            tt = t if d == 0 else tc - 1 - t
            r0 = pl.multiple_of(tt * SUBLANES, SUBLANES)
            h_re, h_im = carry[2 * c], carry[2 * c + 1]
            n_re = a_re[d] * h_re - a_im[d] * h_im + bu_scr[c, pl.ds(r0, SUBLANES), 0:LANES]
            n_im = a_re[d] * h_im + a_im[d] * h_re + bu_scr[c, pl.ds(r0, SUBLANES), LANES:2 * LANES]
            hs_scr[c, 0, pl.ds(r0, SUBLANES), :] = n_re
            hs_scr[c, 1, pl.ds(r0, SUBLANES), :] = n_im
            new += [n_re, n_im]
        return tuple(new)

    init = []
    for c in range(n_chain):
        init += [h_scr[c, :, 0:LANES], h_scr[c, :, LANES:2 * LANES]]
    fin = lax.fori_loop(0, tc, step, tuple(init), unroll=8)
    for c in range(n_chain):
        h_scr[c, :, 0:LANES] = fin[2 * c]
        h_scr[c, :, LANES:2 * LANES] = fin[2 * c + 1]
    hfin_ref[...] = h_scr[...]

    for d, y_ref in enumerate((yf_ref, yb_ref)):
        for b in range(bsz):
            c = d * bsz + b
            parts = [hs_scr[c, ri, pl.ds(s, tc, stride=SUBLANES), :].astype(BF16)
                     for s in range(SUBLANES) for ri in range(2)]
            y_ref[b] = _dot(jnp.concatenate(parts, axis=-1), cw_ref[0, d])


def _s5_scan(u, h0, mask, bst, a_tiles, cwide, *, layer, tc):
    bsz, seq, _ = u.shape
    nch = seq // tc
    full = lambda shape: pl.BlockSpec(shape, lambda i: (0,) * len(shape))
    fwd = pl.BlockSpec((bsz, tc, S5_WIDTH), lambda i: (0, i, 0))
    bwd = pl.BlockSpec((bsz, tc, S5_WIDTH), lambda i: (0, nch - 1 - i, 0))
    state = (2 * bsz, SUBLANES, 2 * LANES)
    rows = SUBLANES * tc
    return pl.pallas_call(
        functools.partial(_s5_kernel, bsz=bsz, tc=tc),
        grid=(nch,),
        in_specs=[fwd, bwd, full(state), full(mask.shape), _layer_spec(layer, *bst.shape[1:]),
                  _layer_spec(layer, *a_tiles.shape[1:]), _layer_spec(layer, *cwide.shape[1:])],
        out_specs=[fwd, bwd, full(state)],
        out_shape=[jax.ShapeDtypeStruct(u.shape, F32), jax.ShapeDtypeStruct(u.shape, F32),
                   jax.ShapeDtypeStruct(state, F32)],
        scratch_shapes=[pltpu.VMEM((2 * bsz, rows, S5_WIDTH), BF16),
                        pltpu.VMEM((2 * bsz, rows, 2 * LANES), F32),
                        pltpu.VMEM((2 * bsz, 2, rows, LANES), F32),
                        pltpu.VMEM(state, F32)],
        compiler_params=_cparams("arbitrary"),
        name="s5_scan",
    )(u, u, h0, mask, bst, a_tiles, cwide)


def _s5_mask():
    m = np.arange(S5_WIDTH)[None, :] // CH_PER_CHUNK == np.arange(SUBLANES)[:, None]
    return jnp.asarray(m.astype(np.float32), F32)


def _s5_params(lam_re, lam_im, log_dt, b_re, b_im, c_re, c_im):
    dt = jnp.exp(log_dt)[..., None]
    mag = jnp.exp(lam_re * dt)
    l_re = mag * jnp.cos(lam_im * dt)
    l_im = mag * jnp.sin(lam_im * dt)
    den = lam_re * lam_re + lam_im * lam_im
    f_re = ((l_re - 1.0) * lam_re + l_im * lam_im) / den
    f_im = (l_im * lam_re - (l_re - 1.0) * lam_im) / den
    bb_re = f_re[..., None] * b_re - f_im[..., None] * b_im
    bb_im = f_re[..., None] * b_im + f_im[..., None] * b_re
    a_tiles = jnp.stack([l_re.reshape(2, SUBLANES, LANES), l_im.reshape(2, SUBLANES, LANES)], axis=1)

    half = S5_GROUPS // SUBLANES
    eye = jnp.eye(half, dtype=F32)

    def in_mat(bb):
        t = bb.reshape(2, SUBLANES, half, S5_STATE, S5_GROUP)
        m = jnp.einsum('dsgpc,gh->dsgchp', t, eye)
        return m.reshape(2, S5_WIDTH, half * S5_STATE)

    def out_mat(cc):
        t = cc.reshape(2, SUBLANES, half, S5_GROUP, S5_STATE)
        m = jnp.einsum('dsgcp,st,gh->dsgpthc', t, jnp.eye(SUBLANES, dtype=F32), eye)
        return m.reshape(2, SUBLANES, half * S5_STATE, S5_WIDTH)

    bst = jnp.concatenate([in_mat(bb_re), in_mat(bb_im)], axis=-1).astype(BF16)
    cwide = jnp.stack([out_mat(c_re), out_mat(-c_im)], axis=2)
    cwide = cwide.reshape(2, 2 * SUBLANES * LANES, S5_WIDTH).astype(BF16)
    return bst, a_tiles, cwide


def _conv_kernel(prev_ref, cur_ref, next_ref, wdw_ref, bdw_ref, g_ref, b_ref, wpw_ref, bpw_ref,
                 o_ref, win_ref, *, tm):
    i = pl.program_id(1)
    nt = pl.num_programs(1)
    zero = jnp.zeros((CONV_HALO, CONV_WIDTH), F32)
    win_ref[0:CONV_HALO] = jnp.where(i > 0, prev_ref[0], zero)
    win_ref[CONV_HALO:CONV_HALO + tm] = cur_ref[0]
    win_ref[CONV_HALO + tm:] = jnp.where(i < nt - 1, next_ref[0], zero)
    acc = jnp.zeros((tm, CONV_WIDTH), F32) + bdw_ref[0]
    for r in range(SUBLANES):
        z = win_ref[0:tm + SUBLANES, :] * wdw_ref[0, r:r + 1, :]
        for a in range(1, (CONV_K + 1) // SUBLANES):
            j = a * SUBLANES
            z = z + win_ref[j:j + tm + SUBLANES, :] * wdw_ref[0, j + r:j + r + 1, :]
        acc = acc + z[r:r + tm]
    h = _silu(_layer_norm(acc) * g_ref[0] + b_ref[0])
    o_ref[0] = (_dot(h.astype(BF16), wpw_ref[0]) + bpw_ref[0]).astype(BF16)


def _conformer_conv(cg, w_dw, b_dw, ln_g, ln_b, w_pw_bf, b_pw, *, layer, tm):
    bsz, seq, _ = cg.shape
    hb = tm // CONV_HALO
    last = seq // CONV_HALO - 1
    vec = _layer_spec(layer, 1, CONV_WIDTH)
    return pl.pallas_call(
        functools.partial(_conv_kernel, tm=tm),
        grid=(bsz, seq // tm),
        in_specs=[
            pl.BlockSpec((1, CONV_HALO, CONV_WIDTH), lambda b, i: (b, jnp.maximum(i * hb - 1, 0), 0)),
            pl.BlockSpec((1, tm, CONV_WIDTH), lambda b, i: (b, i, 0)),
            pl.BlockSpec((1, CONV_HALO, CONV_WIDTH), lambda b, i: (b, jnp.minimum((i + 1) * hb, last), 0)),
            _layer_spec(layer, CONV_K + 1, CONV_WIDTH),
            vec, vec, vec,
            _layer_spec(layer, CONV_WIDTH, CONV_WIDTH),
            vec,
        ],
        out_specs=pl.BlockSpec((1, tm, CONV_WIDTH), lambda b, i: (b, i, 0)),
        out_shape=jax.ShapeDtypeStruct((bsz, seq, CONV_WIDTH), BF16),
        scratch_shapes=[pltpu.VMEM((tm + 2 * CONV_HALO, CONV_WIDTH), F32)],
        compiler_params=_cparams("arbitrary", "arbitrary"),
        name="conformer_conv",
    )(cg, cg, cg, w_dw, b_dw, ln_g, ln_b, w_pw_bf, b_pw)


def _mixout_kernel(attn_ref, yf_ref, yb_ref, u_ref, conv_ref, x_ref,
                   dskip_ref, wglu_ref, bglu_ref, wout_ref, bout_ref,
                   gate_ref, g1_ref, b1_ref, shift_ref, scale_ref, wr_hi_ref, wr_lo_ref,
                   xmid_ref, h2_ref, logit_ref, *, mod_row):
    gate = _mod_vec(gate_ref, mod_row)
    shift = _mod_vec(shift_ref, mod_row)
    scale1 = 1.0 + _mod_vec(scale_ref, mod_row)
    tm = x_ref.shape[1]
    chunk = min(tm, MIX_CHUNK)
    for r0 in range(0, tm, chunk):
        rows = slice(r0, r0 + chunk)
        y = dskip_ref[0] * u_ref[0, rows, :] + yf_ref[0, rows, :] + yb_ref[0, rows, :]
        z = _gelu_tanh(y)
        s5 = z * _sigmoid(_dot(z.astype(BF16), wglu_ref[0]) + bglu_ref[0])
        y_mix = (_dot(attn_ref[0, rows, :], wout_ref[0, 0:ATTN_WIDTH, :])
                 + _dot(s5.astype(BF16), wout_ref[0, ATTN_WIDTH:ATTN_WIDTH + S5_WIDTH, :])
                 + _dot(conv_ref[0, rows, :], wout_ref[0, ATTN_WIDTH + S5_WIDTH:, :])
                 + bout_ref[0])
        r = DEEPNORM_ALPHA * x_ref[0, rows, :] + gate * y_mix
        x_mid = _layer_norm(r) * g1_ref[0] + b1_ref[0]
        xmid_ref[0, rows, :] = x_mid
        h2 = _layer_norm(x_mid) * scale1 + shift
        h2_ref[0, rows, :] = h2.astype(BF16)
        h_hi, h_lo = _split_bf16(h2)
        logit_ref[0, :, rows] = (_dot_nt(wr_hi_ref[0], h_hi)
                                 + (_dot_nt(wr_hi_ref[0], h_lo) + _dot_nt(wr_lo_ref[0], h_hi)))


def _mixer_output(attn, yf, yb, u, conv, x, d_skip, w_glu_bf, b_glu, w_out_bf, b_out,
                  mods, ln_g, ln_b, wr_hi, wr_lo, *, layer, mod_row, tm):
    bsz, seq, _ = x.shape
    tok = lambda w: pl.BlockSpec((1, tm, w), lambda b, i: (b, i, 0))
    lay = lambda r, c: _layer_spec(layer, r, c)
    return pl.pallas_call(
        functools.partial(_mixout_kernel, mod_row=mod_row),
        grid=(bsz, seq // tm),
        in_specs=[
            tok(ATTN_WIDTH), tok(S5_WIDTH), tok(S5_WIDTH), tok(S5_WIDTH), tok(CONV_WIDTH), tok(D_MODEL),
            lay(1, S5_WIDTH), lay(S5_WIDTH, S5_WIDTH), lay(1, S5_WIDTH),
            lay(D_MODEL, D_MODEL), lay(1, D_MODEL),
            _mod_spec(layer, 2), lay(1, D_MODEL), lay(1, D_MODEL), _mod_spec(layer, 3), _mod_spec(layer, 4),
            lay(LANES, D_MODEL), lay(LANES, D_MODEL),
        ],
        out_specs=[tok(D_MODEL), tok(D_MODEL), pl.BlockSpec((1, LANES, tm), lambda b, i: (b, 0, i))],
        out_shape=[
            jax.ShapeDtypeStruct((bsz, seq, D_MODEL), F32),
            jax.ShapeDtypeStruct((bsz, seq, D_MODEL), BF16),
            jax.ShapeDtypeStruct((bsz, LANES, seq), F32),
        ],
        compiler_params=_cparams("arbitrary", "arbitrary"),
        name="mixer_output",
    )(attn, yf, yb, u, conv, x, d_skip, w_glu_bf, b_glu, w_out_bf, b_out,
      mods, ln_g, ln_b, mods, mods, wr_hi, wr_lo)


def _token_cumsum(m, tri, ones):
    nblk = m.shape[0]
    m2 = m.reshape(nblk * N_EXPERTS, LANES).astype(BF16)
    within = _dot(m2, tri).reshape(nblk, N_EXPERTS, LANES)
    tot = _dot(m2, ones).reshape(nblk, N_EXPERTS, LANES)
    offs = []
    run = jnp.zeros((N_EXPERTS, LANES), F32)
    for j in range(nblk):
        offs.append(run)
        run = run + tot[j]
    off = jnp.stack(offs, axis=0)
    return within + off, off, tot


def _router_kernel(logit_ref, tri_ref, ones_ref, aff_ref, sel_ref, pos_ref, off_ref, cnt_ref, *, cap, slot0):
    b = pl.program_id(0)
    nblk = aff_ref.shape[1]

    def soft(j, carry):
        r0 = pl.multiple_of(j * LANES, LANES)
        t = logit_ref[0, 0:N_EXPERTS, pl.ds(r0, LANES)]
        ex = jnp.exp(t - jnp.max(t, axis=0, keepdims=True))
        aff_ref[0, j] = ex / jnp.sum(ex, axis=0, keepdims=True)
        return carry

    lax.fori_loop(0, nblk, soft, 0, unroll=min(nblk, 4))
    aff = aff_ref[0]

    def enough(cand):
        cnt = jnp.sum(jnp.where(aff >= cand[None], 1.0, 0.0), axis=0)
        return jnp.sum(cnt, axis=-1, keepdims=True) >= cap

    p = jnp.full((N_EXPERTS, LANES), 2.0, F32)
    for k in range(6, -1, -1):
        cand = p * (2.0 ** -(2 ** k))
        p = jnp.where(enough(cand), p, cand)
    thr = 0.5 * p
    thr = jnp.where(enough(thr), thr, 0.0)

    def refine(_, carry):
        lo, step = carry
        cand = lo + step
        return jnp.where(enough(cand), cand, lo), 0.5 * step

    thr, _ = lax.fori_loop(0, 23, refine, (thr, 0.5 * thr))
    gt = aff > thr[None]
    eq = aff == thr[None]
    n_gt = jnp.sum(jnp.sum(jnp.where(gt, 1.0, 0.0), axis=0), axis=-1, keepdims=True)
    need = cap - n_gt
    tri = tri_ref[...]
    ones = ones_ref[...]
    cum_eq, _, _ = _token_cumsum(jnp.where(eq, 1.0, 0.0), tri, ones)
    sel = jnp.where(gt | (eq & (cum_eq <= need[None])), 1.0, 0.0)
    cum_sel, off, cnt = _token_cumsum(sel, tri, ones)
    base = (slot0 + b * cap).astype(F32)
    sel_ref[0] = sel
    pos_ref[0] = cum_sel - sel + base
    off_ref[0] = off + base
    cnt_ref[0] = cnt


def _router(logits, *, slot0):
    bsz, _, n = logits.shape
    nblk = n // LANES
    cap = CAPACITY_FACTOR * n // N_EXPERTS
    idx = np.arange(LANES)
    tri = jnp.asarray((idx[:, None] <= idx[None, :]).astype(np.float32), BF16)
    ones = jnp.ones((LANES, LANES), BF16)
    shape = (bsz, nblk, N_EXPERTS, LANES)
    out = pl.BlockSpec((1, nblk, N_EXPERTS, LANES), lambda b: (b, 0, 0, 0))
    sq = pl.BlockSpec((LANES, LANES), lambda b: (0, 0))
    return pl.pallas_call(
        functools.partial(_router_kernel, cap=cap, slot0=slot0),
        grid=(bsz,),
        in_specs=[pl.BlockSpec((1, LANES, n), lambda b: (b, 0, 0)), sq, sq],
        out_specs=[out] * 5,
        out_shape=[jax.ShapeDtypeStruct(shape, F32)] * 5,
        compiler_params=_cparams("arbitrary"),
        name="router",
    )(logits, tri, ones)


COMBINE_KW = LANES + 16
SMALL_COUNT = 32
COMBINE_SMALL_KW = SMALL_COUNT + 16
FF_TILE = 256


def _dispatch_blocks(tb_ref, n_entries, e, blocks, xs_buf, g_buf):
    most = tb_ref[n_entries + blocks[0][0] * N_EXPERTS + e]
    for tile, _ in blocks[1:]:
        most = jnp.maximum(most, tb_ref[n_entries + tile * N_EXPERTS + e])

    def run(kw):
        k = lax.broadcasted_iota(jnp.int32, (kw, LANES), 0).astype(F32)
        for tile, load in blocks:
            h2_blk, pos_row, sel_row, aff_row = load()
            base = tb_ref[tile * N_EXPERTS + e]
            base16 = pl.multiple_of(lax.shift_left(lax.shift_right_logical(base, 4), 4), 16)
            hit = (k == (pos_row - base16.astype(F32))) & (sel_row > 0.5)
            sel_t = jnp.where(hit, 1.0, 0.0).astype(BF16)
            xs_buf[pl.ds(base16, kw), :] += _dot(sel_t, h2_blk).astype(BF16)
            g = jnp.sum(jnp.where(hit, aff_row, 0.0), axis=-1, keepdims=True)
            g_buf[pl.ds(base16, kw), :] += jnp.broadcast_to(g, (kw, LANES))

    @pl.when(most <= SMALL_COUNT)
    def _():
        run(COMBINE_SMALL_KW)

    @pl.when(most > SMALL_COUNT)
    def _():
        run(COMBINE_KW)


def _moe_kernel(*refs, n_ctx, rows, n_entries):
    tb_ref = refs[0]
    if n_ctx:
        (posx_ref, selx_ref, affx_ref, h2x_ref, posc_ref, selc_ref, affc_ref, h2c_ref,
         wg_ref, wu_ref, wd_ref, y_ref, acc_scr, g_scr, xs_scr) = refs[1:]
    else:
        (posx_ref, selx_ref, affx_ref, h2x_ref, wg_ref, wu_ref, wd_ref, y_ref,
         acc_scr, g_scr, xs_scr) = refs[1:]
    p = pl.program_id(0)
    s = pl.program_id(1)
    last = pl.num_programs(1) - 1
    fill = lax.rem(p, 2)
    use = 1 - fill
    blk_per_step = posx_ref.shape[2]

    @pl.when(p < N_EXPERTS)
    def _():
        xs_buf = xs_scr.at[fill]
        g_buf = g_scr.at[fill]

        @pl.when(s == 0)
        def _():
            xs_buf[...] = jnp.zeros(xs_buf.shape, BF16)
            g_buf[...] = jnp.zeros(g_buf.shape, F32)

        if n_ctx:
            @pl.when(s == 0)
            def _():
                bsz_c, blk_c = posc_ref.shape[0], posc_ref.shape[2]

                def load_c(b, i):
                    return lambda: (h2c_ref[b, i * LANES:(i + 1) * LANES, :], posc_ref[b, 0, i:i + 1, :],
                                    selc_ref[b, 0, i:i + 1, :], affc_ref[b, 0, i:i + 1, :])

                blocks = [((last + 1) * blk_per_step + b * blk_c + i, load_c(b, i))
                          for b in range(bsz_c) for i in range(blk_c)]
                _dispatch_blocks(tb_ref, n_entries, p, blocks, xs_buf, g_buf)

        def load_x(i):
            return lambda: (h2x_ref[0, i * LANES:(i + 1) * LANES, :], posx_ref[0, 0, i:i + 1, :],
                            selx_ref[0, 0, i:i + 1, :], affx_ref[0, 0, i:i + 1, :])

        blocks = [(s * blk_per_step + i, load_x(i)) for i in range(blk_per_step)]
        _dispatch_blocks(tb_ref, n_entries, p, blocks, xs_buf, g_buf)

    @pl.when(p >= 1)
    def _():
        wg = wg_ref[0, 0].astype(BF16)
        wu = wu_ref[0, 0].astype(BF16)
        wd = wd_ref[0, 0].astype(BF16)
        half = rows // 2
        for r0 in (0, half):
            xs = xs_scr[use, r0:r0 + half, :]
            hid = _silu(_dot(xs, wg)) * _dot(xs, wu)
            part = _dot(hid.astype(BF16), wd)
            acc_scr[r0:r0 + half, :] = jnp.where(s == 0, part, acc_scr[r0:r0 + half, :] + part)

        @pl.when(s == last)
        def _():
            g = g_scr[use, 0:rows, :]
            gated = acc_scr[...] * jnp.concatenate([g] * (D_MODEL // LANES), axis=-1)
            y_ref[0, 0:rows, :] = gated.astype(BF16)
            y_ref[0, rows:, :] = jnp.zeros((y_ref.shape[1] - rows, D_MODEL), BF16)


def _moe_experts(tb, route_x, h2x, route_c, h2c, w_gate, w_up, w_down, *, layer):
    bsz, n, d = h2x.shape
    n_ff = EXPERT_FF // FF_TILE
    x_step = bsz * n // n_ff
    steps_per_b = n // x_step
    blk_per_step = x_step // LANES
    n_ctx = 0 if h2c is None else h2c.shape[0] * h2c.shape[1]
    rows = bsz * CAPACITY_FACTOR * n // N_EXPERTS + CAPACITY_FACTOR * n_ctx // N_EXPERTS
    rows_pad = rows + COMBINE_KW
    last_e = N_EXPERTS - 1

    def xs_idx(p, s):
        sc = jnp.where(p > last_e, n_ff - 1, s)
        return sc // steps_per_b, sc % steps_per_b

    def route_x_spec():
        return pl.BlockSpec((1, 1, blk_per_step, LANES),
                            lambda p, s, tb: (xs_idx(p, s)[0], jnp.minimum(p, last_e), xs_idx(p, s)[1], 0))

    in_specs = [route_x_spec(), route_x_spec(), route_x_spec(),
                pl.BlockSpec((1, x_step, d), lambda p, s, tb: (xs_idx(p, s)[0], xs_idx(p, s)[1], 0))]
    args = list(route_x) + [h2x]
    if n_ctx:
        bc, nc, _ = h2c.shape
        rc = pl.BlockSpec((bc, 1, nc // LANES, LANES), lambda p, s, tb: (0, jnp.minimum(p, last_e), 0, 0))
        in_specs += [rc, rc, rc, pl.BlockSpec((bc, nc, d), lambda p, s, tb: (0, 0, 0))]
        args += list(route_c) + [h2c]
    ffn_e = lambda p: jnp.maximum(p - 1, 0)
    in_specs += [pl.BlockSpec((1, 1, d, FF_TILE), lambda p, s, tb: (layer, ffn_e(p), 0, s)),
                 pl.BlockSpec((1, 1, d, FF_TILE), lambda p, s, tb: (layer, ffn_e(p), 0, s)),
                 pl.BlockSpec((1, 1, FF_TILE, d), lambda p, s, tb: (layer, ffn_e(p), s, 0))]
    args += [w_gate, w_up, w_down]
    return pl.pallas_call(
        functools.partial(_moe_kernel, n_ctx=n_ctx, rows=rows, n_entries=tb.shape[0] // 2),
        grid_spec=pltpu.PrefetchScalarGridSpec(
            num_scalar_prefetch=1,
            grid=(N_EXPERTS + 1, n_ff),
            in_specs=in_specs,
            out_specs=pl.BlockSpec((1, rows_pad, d), lambda p, s, tb: (ffn_e(p), 0, 0)),
            scratch_shapes=[pltpu.VMEM((rows, d), F32), pltpu.VMEM((2, rows_pad, LANES), F32),
                            pltpu.VMEM((2, rows_pad, d), BF16)],
        ),
        out_shape=jax.ShapeDtypeStruct((N_EXPERTS, rows_pad, d), BF16),
        compiler_params=_cparams("arbitrary", "arbitrary"),
        name="moe_experts",
    )(tb, *args)


def _combine_kernel(tb_ref, pos_ref, sel_ref, xmid_ref, gate_ref, g_ref, b_ref, y_hbm, o_ref,
                    y_scr, stack_scr, sem, *, cap, slot0, blk_per_step, nblk, n_entries, mod_row):
    b = pl.program_id(0)
    j = pl.program_id(1)
    win = y_scr.shape[1]
    gate = _mod_vec(gate_ref, mod_row)

    @pl.when(j == 0)
    def _():
        row0 = pl.multiple_of(slot0 + b * cap, 16)
        cp = pltpu.make_async_copy(y_hbm.at[:, pl.ds(row0, win), :], y_scr, sem)
        cp.start()
        cp.wait()

    set_base = (slot0 + b * cap).astype(F32)
    tn_dims = (((0,), (0,)), ((), ()))
    for i in range(blk_per_step):
        entry = (b * nblk + j * blk_per_step + i) * N_EXPERTS
        rows = slice(i * LANES, (i + 1) * LANES)

        def hits(e, kw, entry=entry, i=i):
            base = tb_ref[entry + e] - (slot0 + b * cap)
            base16 = pl.multiple_of(lax.shift_left(lax.shift_right_logical(base, 4), 4), 16)
            k = lax.broadcasted_iota(jnp.int32, (kw, LANES), 0).astype(F32)
            rel = pos_ref[0, i, e:e + 1, :] - (set_base + base16.astype(F32))
            hit = (k == rel) & (sel_ref[0, i, e:e + 1, :] > 0.5)
            return base16, jnp.where(hit, 1.0, 0.0).astype(BF16)

        def finish(moe, rows=rows):
            r = DEEPNORM_ALPHA * xmid_ref[0, rows, :] + gate * moe
            o_ref[0, rows, :] = _layer_norm(r) * g_ref[0] + b_ref[0]

        most = tb_ref[n_entries + entry]
        for e in range(1, N_EXPERTS):
            most = jnp.maximum(most, tb_ref[n_entries + entry + e])

        @pl.when(most <= SMALL_COUNT)
        def _():
            sel_all = []
            for e in range(N_EXPERTS):
                base16, sel_t = hits(e, COMBINE_SMALL_KW)
                sel_all.append(sel_t)
                stack_scr[e * COMBINE_SMALL_KW:(e + 1) * COMBINE_SMALL_KW, :] = (
                    y_scr[e, pl.ds(base16, COMBINE_SMALL_KW), :])
            finish(lax.dot_general(jnp.concatenate(sel_all, axis=0), stack_scr[...], tn_dims,
                                   preferred_element_type=F32))

        @pl.when(most > SMALL_COUNT)
        def _():
            acc = jnp.zeros((LANES, D_MODEL), F32)
            for e in range(N_EXPERTS):
                base16, sel_t = hits(e, COMBINE_KW)
                acc = acc + lax.dot_general(sel_t, y_scr[e, pl.ds(base16, COMBINE_KW), :], tn_dims,
                                            preferred_element_type=F32)
            finish(acc)


def _combine_post(tb, pos, sel, x_mid, mods, ln_g, ln_b, y, *, layer, mod_row, slot0, tm):
    bsz, n, d = x_mid.shape
    nblk = n // LANES
    cap = CAPACITY_FACTOR * n // N_EXPERTS
    blk_per_step = tm // LANES
    tok = pl.BlockSpec((1, tm, d), lambda b, j, tb: (b, j, 0))
    route = pl.BlockSpec((1, blk_per_step, N_EXPERTS, LANES), lambda b, j, tb: (b, j, 0, 0))
    vec = _layer_spec(layer, 1, d)
    return pl.pallas_call(
        functools.partial(_combine_kernel, cap=cap, slot0=slot0, blk_per_step=blk_per_step, nblk=nblk,
                          n_entries=bsz * nblk * N_EXPERTS, mod_row=mod_row),
        grid_spec=pltpu.PrefetchScalarGridSpec(
            num_scalar_prefetch=1,
            grid=(bsz, n // tm),
            in_specs=[route, route, tok, _mod_spec(layer, 5), vec, vec,
                      pl.BlockSpec(memory_space=pl.ANY)],
            out_specs=tok,
            scratch_shapes=[pltpu.VMEM((N_EXPERTS, cap + COMBINE_KW, d), BF16),
                            pltpu.VMEM((N_EXPERTS * COMBINE_SMALL_KW, d), BF16),
                            pltpu.SemaphoreType.DMA(())],
        ),
        out_shape=jax.ShapeDtypeStruct(x_mid.shape, F32),
        compiler_params=_cparams("arbitrary", "arbitrary"),
        name="combine_post",
    )(tb, pos, sel, x_mid, mods, ln_g, ln_b, y)


def _tile_table(*lane_replicated):
    return jnp.concatenate([a[..., 0].astype(jnp.int32).reshape(-1) for a in lane_replicated])


def _expert_major(a):
    return jnp.swapaxes(a, 1, 2)


def kernel(x, c, ctx, c_ctx, w_mod, b_mod, w_in, b_in, attn_sink, s5_lam_re, s5_lam_im, s5_log_dt,
           s5_b_re, s5_b_im, s5_c_re, s5_c_im, s5_d, s5_w_glu, s5_b_glu, conv_w_dw, conv_b_dw,
           conv_ln_g, conv_ln_b, conv_w_pw, conv_b_pw, w_out, b_out, ln1_g, ln1_b, w_router,
           exp_w_gate, exp_w_up, exp_w_down, ln2_g, ln2_b):
    bsz, seq, d = x.shape
    lc = ctx.shape[1]
    tm_x, tm_c = min(1024, seq), lc

    cond = jnp.zeros((SUBLANES, d), F32).at[:bsz].set(c).at[bsz].set(c_ctx)
    mods = _modulation(cond, w_mod, b_mod)
    cos_t, sin_t = _rope_tables(seq)
    s5_mask = _s5_mask()
    zero_state = jnp.zeros((2 * bsz, SUBLANES, 2 * LANES), F32)

    vec3 = lambda a: a.reshape(DEPTH, 1, -1)
    w_in_bf, w_out_bf = w_in.astype(BF16), w_out.astype(BF16)
    w_glu_bf, w_pw_bf = s5_w_glu.astype(BF16), conv_w_pw.astype(BF16)
    wr = jnp.pad(jnp.swapaxes(w_router, 1, 2), ((0, 0), (0, LANES - N_EXPERTS), (0, 0)))
    wr_hi, wr_lo = _split_bf16(wr)
    sink_rep = jnp.broadcast_to(attn_sink[:, :, None], (DEPTH, N_Q_HEADS, LANES))
    bst, a_tiles, cwide = jax.vmap(_s5_params)(s5_lam_re, s5_lam_im, s5_log_dt, s5_b_re, s5_b_im,
                                               s5_c_re, s5_c_im)
    w_dw = jnp.pad(conv_w_dw.reshape(DEPTH, CONV_K, CONV_WIDTH), ((0, 0), (1, 0), (0, 0)))
    conv_args = (w_dw, vec3(conv_b_dw), vec3(conv_ln_g), vec3(conv_ln_b), w_pw_bf, vec3(conv_b_pw))
    mix_args = (vec3(s5_d), w_glu_bf, vec3(s5_b_glu), w_out_bf, vec3(b_out), mods, vec3(ln1_g), vec3(ln1_b),
                wr_hi, wr_lo)
    b_in3, ln2_g3, ln2_b3 = vec3(b_in), vec3(ln2_g), vec3(ln2_b)

    xc = ctx
    ctx_row = bsz
    for l in range(DEPTH):
        last = l == DEPTH - 1
        q, k, v, u, cg = _in_projection(x, mods, w_in_bf, b_in3, cos_t, sin_t,
                                        layer=l, mod_row=None, rope=True, tm=tm_x)
        q_c, k_c, v_c, u_c, cg_c = _in_projection(xc, mods, w_in_bf, b_in3, cos_t, sin_t,
                                                  layer=l, mod_row=ctx_row, rope=False, tm=tm_c)

        attn_x = _attention(q, k, v, k_c, v_c, sink_rep, layer=l, window=True)
        yf_c, yb_c, h_ctx = _s5_scan(u_c, zero_state, s5_mask, bst, a_tiles, cwide, layer=l, tc=lc)
        yf, yb, _ = _s5_scan(u, h_ctx, s5_mask, bst, a_tiles, cwide, layer=l, tc=256)
        conv_x = _conformer_conv(cg, *conv_args, layer=l, tm=256)
        x_mid, h2, logits = _mixer_output(attn_x, yf, yb, u, conv_x, x, *mix_args,
                                          layer=l, mod_row=None, tm=tm_x)
        aff_x, sel_x, pos_x, off_x, cnt_x = _router(logits, slot0=0)
        route_x = tuple(_expert_major(a) for a in (pos_x, sel_x, aff_x))
        tb_x = _tile_table(off_x, cnt_x)
        if not last:
            attn_c = _attention(q_c, None, None, k_c, v_c, sink_rep, layer=l, window=False)
            conv_c = _conformer_conv(cg_c, *conv_args, layer=l, tm=lc)
            xc_mid, hc2, logits_c = _mixer_output(attn_c, yf_c, yb_c, u_c, conv_c, xc, *mix_args,
                                                  layer=l, mod_row=ctx_row, tm=tm_c)
            slot0_c = bsz * CAPACITY_FACTOR * seq // N_EXPERTS
            aff_c, sel_c, pos_c, off_c, cnt_c = _router(logits_c, slot0=slot0_c)
            route_c = tuple(_expert_major(a) for a in (pos_c, sel_c, aff_c))
            tb_c = _tile_table(off_c, cnt_c)
            y = _moe_experts(_tile_table(off_x, off_c, cnt_x, cnt_c), route_x, h2, route_c, hc2,
                             exp_w_gate, exp_w_up, exp_w_down, layer=l)
            xc = _combine_post(tb_c, pos_c, sel_c, xc_mid, mods, ln2_g3, ln2_b3, y,
                               layer=l, mod_row=ctx_row, slot0=slot0_c, tm=lc)
        else:
            y = _moe_experts(tb_x, route_x, h2, None, None, exp_w_gate, exp_w_up, exp_w_down, layer=l)
        x = _combine_post(tb_x, pos_x, sel_x, x_mid, mods, ln2_g3, ln2_b3, y,
                          layer=l, mod_row=None, slot0=0, tm=256)
    return x
```

```python
import functools
import math

import jax
import jax.numpy as jnp
import numpy as np
from jax import lax
from jax.experimental import pallas as pl
from jax.experimental.pallas import tpu as pltpu

D_MODEL = 1024
DEPTH = 2
GRID_W = 64
HEAD_DIM = 64
ATTN_WIDTH = D_MODEL // 2
N_Q_HEADS = ATTN_WIDTH // HEAD_DIM
N_KV_HEADS = N_Q_HEADS // 4
Q_PER_KV = N_Q_HEADS // N_KV_HEADS
KV_WIDTH = N_KV_HEADS * HEAD_DIM
BLOCK = 128
ROPE_BASE = 10000.0
NEG_INF = -1e30
S5_WIDTH = D_MODEL // 4
S5_GROUP = 16
S5_GROUPS = S5_WIDTH // S5_GROUP
S5_STATE = 64
CONV_WIDTH = D_MODEL - ATTN_WIDTH - S5_WIDTH
CONV_K = 31
CONV_HALO = 16
Q_END = ATTN_WIDTH
K_END = Q_END + KV_WIDTH
V_END = K_END + KV_WIDTH
U_END = V_END + S5_WIDTH
IN_WIDTH = U_END + 2 * CONV_WIDTH
N_EXPERTS = 16
EXPERT_FF = 2 * D_MODEL
CAPACITY_FACTOR = 2
DEEPNORM_ALPHA = (2.0 * DEPTH) ** 0.25
LN_EPS = 1e-5

LANES = 128
SUBLANES = 8
S5_CHUNKS = S5_WIDTH * S5_STATE // S5_GROUP // LANES
CH_PER_CHUNK = S5_WIDTH // S5_CHUNKS
VMEM_LIMIT = 56 * 1024 * 1024
MIX_CHUNK = 256

F32 = jnp.float32
BF16 = jnp.bfloat16


def _cparams(*sem):
    return pltpu.CompilerParams(dimension_semantics=sem, vmem_limit_bytes=VMEM_LIMIT)


def _dot(a, b):
    return jnp.dot(a, b, preferred_element_type=F32)


def _dot_nt(a, b):
    return lax.dot_general(a, b, (((1,), (1,)), ((), ())), preferred_element_type=F32)


def _split_bf16(x):
    hi = x.astype(BF16)
    lo = (x - hi.astype(F32)).astype(BF16)
    return hi, lo


def _dot3(a, b_hi, b_lo):
    a_hi, a_lo = _split_bf16(a)
    return _dot(a_hi, b_hi) + (_dot(a_lo, b_hi) + _dot(a_hi, b_lo))


def _sigmoid(x):
    return 1.0 / (1.0 + jnp.exp(-x))


def _silu(x):
    return x * _sigmoid(x)


def _gelu_tanh(x):
    c = math.sqrt(2.0 / math.pi)
    return 0.5 * x * (1.0 + jnp.tanh(c * (x + 0.044715 * (x * x * x))))


def _layer_norm(x):
    mu = jnp.mean(x, axis=-1, keepdims=True)
    xc = x - mu
    var = jnp.mean(xc * xc, axis=-1, keepdims=True)
    return xc * lax.rsqrt(var + LN_EPS)


def _mod_kernel(c_ref, w_ref, b_ref, o_ref):
    s = _silu(c_ref[...])
    w = w_ref[0]
    w_hi, w_lo = _split_bf16(w)
    o_ref[0] = _dot3(s, w_hi, w_lo) + b_ref[0]


def _modulation(cond, w_mod, b_mod):
    tn = 1536
    n = w_mod.shape[-1]
    return pl.pallas_call(
        _mod_kernel,
        grid=(DEPTH, n // tn),
        in_specs=[
            pl.BlockSpec((SUBLANES, D_MODEL), lambda l, j: (0, 0)),
            pl.BlockSpec((1, D_MODEL, tn), lambda l, j: (l, 0, j)),
            pl.BlockSpec((1, 1, tn), lambda l, j: (l, 0, j)),
        ],
        out_specs=pl.BlockSpec((1, SUBLANES, tn), lambda l, j: (l, 0, j)),
        out_shape=jax.ShapeDtypeStruct((DEPTH, SUBLANES, n), F32),
        compiler_params=_cparams("arbitrary", "arbitrary"),
        name="modulation",
    )(cond, w_mod, b_mod.reshape(DEPTH, 1, n))


def _rope_chunk(x, cos, sin_signed):
    lane = lax.broadcasted_iota(jnp.int32, x.shape, 1)
    first = (lane % 32) < 16
    partner = jnp.where(first, pltpu.roll(x, LANES - 16, 1), pltpu.roll(x, 16, 1))
    return x * cos + partner * sin_signed


def _mod_spec(layer, k):
    return pl.BlockSpec((1, SUBLANES, D_MODEL), lambda *_: (layer, 0, k))


def _layer_spec(layer, *shape):
    return pl.BlockSpec((1,) + shape, lambda *_: (layer,) + (0,) * len(shape))


def _mod_vec(ref, mod_row):
    row = pl.program_id(0) if mod_row is None else mod_row
    return ref[0, pl.ds(row, 1), :]


def _inproj_kernel(x_ref, shift_ref, scale_ref, w_ref, b_ref, cos_ref, sin_ref,
                   q_ref, k_ref, v_ref, u_ref, cg_ref, *, rope, mod_row):
    shift = _mod_vec(shift_ref, mod_row)
    scale1 = 1.0 + _mod_vec(scale_ref, mod_row)
    tm = x_ref.shape[1]
    chunk = min(tm, MIX_CHUNK)
    scale = HEAD_DIM ** -0.5
    for r0 in range(0, tm, chunk):
        rows = slice(r0, r0 + chunk)
        h = _layer_norm(x_ref[0, rows, :]) * scale1 + shift
        p = _dot(h.astype(BF16), w_ref[0]) + b_ref[0]
        if rope:
            cos = cos_ref[rows, :]
            sin = sin_ref[rows, :]
        for j in range(ATTN_WIDTH // LANES):
            qc = p[:, j * LANES:(j + 1) * LANES]
            if rope:
                qc = _rope_chunk(qc, cos, sin)
            q_ref[0, rows, j * LANES:(j + 1) * LANES] = (qc * scale).astype(BF16)
        kc = p[:, Q_END:K_END]
        if rope:
            kc = _rope_chunk(kc, cos, sin)
        k_ref[0, rows, :] = kc.astype(BF16)
        v_ref[0, rows, :] = p[:, K_END:V_END].astype(BF16)
        u_ref[0, rows, :] = p[:, V_END:U_END]
        a = p[:, U_END:U_END + CONV_WIDTH]
        g = p[:, U_END + CONV_WIDTH:]
        cg_ref[0, rows, :] = a * _sigmoid(g)


def _in_projection(x, mods, w_in_bf, b_in, cos_t, sin_t, *, layer, mod_row, rope, tm):
    bsz, seq, _ = x.shape
    tok = lambda w: pl.BlockSpec((1, tm, w), lambda b, i: (b, i, 0))
    tab = pl.BlockSpec((tm, LANES), lambda b, i: (i, 0))
    return pl.pallas_call(
        functools.partial(_inproj_kernel, rope=rope, mod_row=mod_row),
        grid=(bsz, seq // tm),
        in_specs=[
            tok(D_MODEL), _mod_spec(layer, 0), _mod_spec(layer, 1),
            _layer_spec(layer, D_MODEL, IN_WIDTH), _layer_spec(layer, 1, IN_WIDTH),
            tab, tab,
        ],
        out_specs=[tok(ATTN_WIDTH), tok(KV_WIDTH), tok(KV_WIDTH), tok(S5_WIDTH), tok(CONV_WIDTH)],
        out_shape=[
            jax.ShapeDtypeStruct((bsz, seq, ATTN_WIDTH), BF16),
            jax.ShapeDtypeStruct((bsz, seq, KV_WIDTH), BF16),
            jax.ShapeDtypeStruct((bsz, seq, KV_WIDTH), BF16),
            jax.ShapeDtypeStruct((bsz, seq, S5_WIDTH), F32),
            jax.ShapeDtypeStruct((bsz, seq, CONV_WIDTH), F32),
        ],
        compiler_params=_cparams("arbitrary", "arbitrary"),
        name="in_projection",
    )(x, mods, mods, w_in_bf, b_in.reshape(DEPTH, 1, IN_WIDTH), cos_t, sin_t)


def _rope_tables(seq):
    f = HEAD_DIM // 4
    inv_freq = jnp.asarray(ROPE_BASE, F32) ** (-jnp.arange(f, dtype=F32) / f)

    def axis_tables(n_pos):
        ang = jnp.arange(n_pos, dtype=jnp.int32).astype(F32)[:, None] * inv_freq[None, :]
        cos, sin = jnp.cos(ang), jnp.sin(ang)
        return jnp.concatenate([cos, cos], axis=-1), jnp.concatenate([-sin, sin], axis=-1)

    n_rows = seq // GRID_W
    cos_r, sin_r = axis_tables(n_rows)
    cos_c, sin_c = axis_tables(GRID_W)
    by_row = lambda a: jnp.repeat(a, GRID_W, axis=0)
    by_col = lambda a: jnp.tile(a, (n_rows, 1))
    cos = jnp.concatenate([by_row(cos_r), by_col(cos_c)], axis=-1)
    sin = jnp.concatenate([by_row(sin_r), by_col(sin_c)], axis=-1)
    return jnp.tile(cos, (1, LANES // HEAD_DIM)), jnp.tile(sin, (1, LANES // HEAD_DIM))


ATTN_SUBBLOCKS = 4


def _attn_kernel(*refs, window):
    if window:
        n_kv = ATTN_SUBBLOCKS + 2
        q_ref = refs[0]
        k_refs = refs[1:1 + n_kv]
        v_refs = refs[1 + n_kv:1 + 2 * n_kv]
        kx_ref, vx_ref, sink_ref, o_ref = refs[1 + 2 * n_kv:]
        subs = ATTN_SUBBLOCKS
    else:
        q_ref, kx_ref, vx_ref, sink_ref, o_ref = refs
        subs = 1
    step = pl.program_id(1)
    n_steps = pl.num_programs(1)
    tq = q_ref.shape[1] // subs
    rows = Q_PER_KV * tq
    if window:
        row = lax.broadcasted_iota(jnp.int32, (rows, BLOCK), 0) % tq
        col = lax.broadcasted_iota(jnp.int32, (rows, BLOCK), 1)

    def with_ones(v, ks):
        return jnp.concatenate([v[:, ks], jnp.ones((v.shape[0], HEAD_DIM), BF16)], axis=-1)

    for sub in range(subs):
        q = q_ref[0, sub * tq:(sub + 1) * tq, :]
        if window:
            kp_ref, kc_ref, kn_ref = k_refs[sub:sub + 3]
            vp_ref, vc_ref, vn_ref = v_refs[sub:sub + 3]
            ok_prev = (col >= row) if sub > 0 else (col >= row) & (step > 0)
            ok_next = (col <= row) if sub < subs - 1 else (col <= row) & (step < n_steps - 1)
        outs = []
        for g in range(N_KV_HEADS):
            ks = slice(g * HEAD_DIM, (g + 1) * HEAD_DIM)
            heads = range(g * Q_PER_KV, (g + 1) * Q_PER_KV)
            qs = jnp.concatenate([q[:, h * HEAD_DIM:(h + 1) * HEAD_DIM] for h in heads], axis=0)
            sink = jnp.concatenate([jnp.broadcast_to(sink_ref[0, h:h + 1, 0:1], (tq, 1)) for h in heads], axis=0)
            lx = kx_ref.shape[1]
            s_x = _dot_nt(qs, kx_ref[0][:, ks])
            m_el = s_x[:, 0:LANES]
            for c in range(1, lx // LANES):
                m_el = jnp.maximum(m_el, s_x[:, c * LANES:(c + 1) * LANES])
            if window:
                s_p = jnp.where(ok_prev, _dot_nt(qs, kp_ref[0][:, ks]), NEG_INF)
                s_c = _dot_nt(qs, kc_ref[0][:, ks])
                s_n = jnp.where(ok_next, _dot_nt(qs, kn_ref[0][:, ks]), NEG_INF)
                m_el = jnp.maximum(jnp.maximum(m_el, s_c), jnp.maximum(s_p, s_n))
            m = jnp.maximum(jnp.max(m_el, axis=-1, keepdims=True), sink)
            acc = _dot(jnp.exp(s_x - m).astype(BF16), with_ones(vx_ref[0], ks))
            if window:
                for s_w, v_ref in ((s_p, vp_ref), (s_c, vc_ref), (s_n, vn_ref)):
                    acc = acc + _dot(jnp.exp(s_w - m).astype(BF16), with_ones(v_ref[0], ks))
            o = acc[:, 0:HEAD_DIM] / (acc[:, HEAD_DIM:] + jnp.exp(sink - m))
            outs += [o[i * tq:(i + 1) * tq] for i in range(Q_PER_KV)]
        o_ref[0, sub * tq:(sub + 1) * tq, :] = jnp.concatenate(outs, axis=-1).astype(BF16)


def _attention(q, k, v, k_ctx, v_ctx, sink_rep, *, layer, window):
    bsz, seq, _ = q.shape
    lc = k_ctx.shape[1]
    ctx_spec = pl.BlockSpec((1, lc, KV_WIDTH), lambda b, i: (b, 0, 0))
    sink_spec = _layer_spec(layer, N_Q_HEADS, LANES)
    if window:
        tq = BLOCK * ATTN_SUBBLOCKS
        nb = seq // BLOCK

        def kv_spec(j):
            return pl.BlockSpec((1, BLOCK, KV_WIDTH),
                                lambda b, i: (b, jnp.clip(ATTN_SUBBLOCKS * i - 1 + j, 0, nb - 1), 0))

        kv_specs = [kv_spec(j) for j in range(ATTN_SUBBLOCKS + 2)]
        in_specs = ([pl.BlockSpec((1, tq, ATTN_WIDTH), lambda b, i: (b, i, 0))] + kv_specs + kv_specs
                    + [ctx_spec, ctx_spec, sink_spec])
        args = (q,) + (k,) * len(kv_specs) + (v,) * len(kv_specs) + (k_ctx, v_ctx, sink_rep)
    else:
        tq = seq
        in_specs = [pl.BlockSpec((1, tq, ATTN_WIDTH), lambda b, i: (b, i, 0)), ctx_spec, ctx_spec, sink_spec]
        args = (q, k_ctx, v_ctx, sink_rep)
    return pl.pallas_call(
        functools.partial(_attn_kernel, window=window),
        grid=(bsz, seq // tq),
        in_specs=in_specs,
        out_specs=pl.BlockSpec((1, tq, ATTN_WIDTH), lambda b, i: (b, i, 0)),
        out_shape=jax.ShapeDtypeStruct((bsz, seq, ATTN_WIDTH), BF16),
        compiler_params=_cparams("arbitrary", "arbitrary"),
        name="window_attention" if window else "context_attention",
    )(*args)


def _s5_kernel(uf_ref, ub_ref, h0_ref, mask_ref, bst_ref, a_ref, cw_ref,
               yf_ref, yb_ref, hfin_ref, lhs_scr, bu_scr, hs_scr, h_scr, *, bsz, tc):
    i = pl.program_id(0)

    @pl.when(i == 0)
    def _():
        h_scr[...] = h0_ref[...]

    mask = mask_ref[...]
    n_chain = 2 * bsz

    for d, u_ref in enumerate((uf_ref, ub_ref)):
        for b in range(bsz):
            for j in range(tc // 2):
                pair = [jnp.broadcast_to(u_ref[b, 2 * j + k:2 * j + k + 1, :], (SUBLANES, S5_WIDTH)) * mask
                        for k in range(2)]
                lhs_scr[d * bsz + b, 2 * SUBLANES * j:2 * SUBLANES * (j + 1), :] = (
                    jnp.concatenate(pair, axis=0).astype(BF16))
    for c in range(n_chain):
        bu_scr[c] = _dot(lhs_scr[c], bst_ref[0, c // bsz])

    a_re = [a_ref[0, d, 0] for d in range(2)]
    a_im = [a_ref[0, d, 1] for d in range(2)]

    def step(t, carry):
        new = []
        for c in range(n_chain):
            d = c // bsz
            tt = t if d == 0 else tc - 1 - t
            r0 = pl.multiple_of(tt * SUBLANES, SUBLANES)
            h_re, h_im = carry[2 * c], carry[2 * c + 1]
            n_re = a_re[d] * h_re - a_im[d] * h_im + bu_scr[c, pl.ds(r0, SUBLANES), 0:LANES]
            n_im = a_re[d] * h_im + a_im[d] * h_re + bu_scr[c, pl.ds(r0, SUBLANES), LANES:2 * LANES]
            hs_scr[c, 0, pl.ds(r0, SUBLANES), :] = n_re
            hs_scr[c, 1, pl.ds(r0, SUBLANES), :] = n_im
            new += [n_re, n_im]
        return tuple(new)

    init = []
    for c in range(n_chain):
        init += [h_scr[c, :, 0:LANES], h_scr[c, :, LANES:2 * LANES]]
    fin = lax.fori_loop(0, tc, step, tuple(init), unroll=8)
    for c in range(n_chain):
        h_scr[c, :, 0:LANES] = fin[2 * c]
        h_scr[c, :, LANES:2 * LANES] = fin[2 * c + 1]
    hfin_ref[...] = h_scr[...]

    for d, y_ref in enumerate((yf_ref, yb_ref)):
        for b in range(bsz):
            c = d * bsz + b
            parts = [hs_scr[c, ri, pl.ds(s, tc, stride=SUBLANES), :].astype(BF16)
                     for s in range(SUBLANES) for ri in range(2)]
            y_ref[b] = _dot(jnp.concatenate(parts, axis=-1), cw_ref[0, d])


def _s5_scan(u, h0, mask, bst, a_tiles, cwide, *, layer, tc):
    bsz, seq, _ = u.shape
    nch = seq // tc
    full = lambda shape: pl.BlockSpec(shape, lambda i: (0,) * len(shape))
    fwd = pl.BlockSpec((bsz, tc, S5_WIDTH), lambda i: (0, i, 0))
    bwd = pl.BlockSpec((bsz, tc, S5_WIDTH), lambda i: (0, nch - 1 - i, 0))
    state = (2 * bsz, SUBLANES, 2 * LANES)
    rows = SUBLANES * tc
    return pl.pallas_call(
        functools.partial(_s5_kernel, bsz=bsz, tc=tc),
        grid=(nch,),
        in_specs=[fwd, bwd, full(state), full(mask.shape), _layer_spec(layer, *bst.shape[1:]),
                  _layer_spec(layer, *a_tiles.shape[1:]), _layer_spec(layer, *cwide.shape[1:])],
        out_specs=[fwd, bwd, full(state)],
        out_shape=[jax.ShapeDtypeStruct(u.shape, F32), jax.ShapeDtypeStruct(u.shape, F32),
                   jax.ShapeDtypeStruct(state, F32)],
        scratch_shapes=[pltpu.VMEM((2 * bsz, rows, S5_WIDTH), BF16),
                        pltpu.VMEM((2 * bsz, rows, 2 * LANES), F32),
                        pltpu.VMEM((2 * bsz, 2, rows, LANES), F32),
                        pltpu.VMEM(state, F32)],
        compiler_params=_cparams("arbitrary"),
        name="s5_scan",
    )(u, u, h0, mask, bst, a_tiles, cwide)


def _s5_mask():
    m = np.arange(S5_WIDTH)[None, :] // CH_PER_CHUNK == np.arange(SUBLANES)[:, None]
    return jnp.asarray(m.astype(np.float32), F32)


def _s5_params(lam_re, lam_im, log_dt, b_re, b_im, c_re, c_im):
    dt = jnp.exp(log_dt)[..., None]
    mag = jnp.exp(lam_re * dt)
    l_re = mag * jnp.cos(lam_im * dt)
    l_im = mag * jnp.sin(lam_im * dt)
    den = lam_re * lam_re + lam_im * lam_im
    f_re = ((l_re - 1.0) * lam_re + l_im * lam_im) / den
    f_im = (l_im * lam_re - (l_re - 1.0) * lam_im) / den
    bb_re = f_re[..., None] * b_re - f_im[..., None] * b_im
    bb_im = f_re[..., None] * b_im + f_im[..., None] * b_re
    a_tiles = jnp.stack([l_re.reshape(2, SUBLANES, LANES), l_im.reshape(2, SUBLANES, LANES)], axis=1)

    half = S5_GROUPS // SUBLANES
    eye = jnp.eye(half, dtype=F32)

    def in_mat(bb):
        t = bb.reshape(2, SUBLANES, half, S5_STATE, S5_GROUP)
        m = jnp.einsum('dsgpc,gh->dsgchp', t, eye)
        return m.reshape(2, S5_WIDTH, half * S5_STATE)

    bst = jnp.concatenate([in_mat(bb_re), in_mat(bb_im)], axis=-1).astype(BF16)
    t = jnp.stack([c_re, -c_im], axis=1).astype(BF16)
    t = t.reshape(2, 2, SUBLANES, half, S5_GROUP, S5_STATE)
    cwide = jnp.einsum('drsgcp,st,gh->dsrgpthc', t, jnp.eye(SUBLANES, dtype=BF16), eye.astype(BF16))
    return bst, a_tiles, cwide.reshape(2, 2 * SUBLANES * LANES, S5_WIDTH)


def _conv_kernel(prev_ref, cur_ref, next_ref, wdw_ref, bdw_ref, g_ref, b_ref, wpw_ref, bpw_ref,
                 o_ref, win_ref, *, tm):
    i = pl.program_id(1)
    nt = pl.num_programs(1)
    zero = jnp.zeros((CONV_HALO, CONV_WIDTH), F32)
    win_ref[0:CONV_HALO] = jnp.where(i > 0, prev_ref[0], zero)
    win_ref[CONV_HALO:CONV_HALO + tm] = cur_ref[0]
    win_ref[CONV_HALO + tm:] = jnp.where(i < nt - 1, next_ref[0], zero)
    acc = jnp.zeros((tm, CONV_WIDTH), F32) + bdw_ref[0]
    for r in range(SUBLANES):
        z = win_ref[0:tm + SUBLANES, :] * wdw_ref[0, r:r + 1, :]
        for a in range(1, (CONV_K + 1) // SUBLANES):
            j = a * SUBLANES
            z = z + win_ref[j:j + tm + SUBLANES, :] * wdw_ref[0, j + r:j + r + 1, :]
        acc = acc + z[r:r + tm]
    h = _silu(_layer_norm(acc) * g_ref[0] + b_ref[0])
    o_ref[0] = (_dot(h.astype(BF16), wpw_ref[0]) + bpw_ref[0]).astype(BF16)


def _conformer_conv(cg, w_dw, b_dw, ln_g, ln_b, w_pw_bf, b_pw, *, layer, tm):
    bsz, seq, _ = cg.shape
    hb = tm // CONV_HALO
    last = seq // CONV_HALO - 1
    vec = _layer_spec(layer, 1, CONV_WIDTH)
    return pl.pallas_call(
        functools.partial(_conv_kernel, tm=tm),
        grid=(bsz, seq // tm),
        in_specs=[
            pl.BlockSpec((1, CONV_HALO, CONV_WIDTH), lambda b, i: (b, jnp.maximum(i * hb - 1, 0), 0)),
            pl.BlockSpec((1, tm, CONV_WIDTH), lambda b, i: (b, i, 0)),
            pl.BlockSpec((1, CONV_HALO, CONV_WIDTH), lambda b, i: (b, jnp.minimum((i + 1) * hb, last), 0)),
            _layer_spec(layer, CONV_K + 1, CONV_WIDTH),
            vec, vec, vec,
            _layer_spec(layer, CONV_WIDTH, CONV_WIDTH),
            vec,
        ],
        out_specs=pl.BlockSpec((1, tm, CONV_WIDTH), lambda b, i: (b, i, 0)),
        out_shape=jax.ShapeDtypeStruct((bsz, seq, CONV_WIDTH), BF16),
        scratch_shapes=[pltpu.VMEM((tm + 2 * CONV_HALO, CONV_WIDTH), F32)],
        compiler_params=_cparams("arbitrary", "arbitrary"),
        name="conformer_conv",
    )(cg, cg, cg, w_dw, b_dw, ln_g, ln_b, w_pw_bf, b_pw)


def _mixout_kernel(attn_ref, yf_ref, yb_ref, u_ref, conv_ref, x_ref,
                   dskip_ref, wglu_ref, bglu_ref, wout_ref, bout_ref,
                   gate_ref, g1_ref, b1_ref, shift_ref, scale_ref, wr_hi_ref, wr_lo_ref,
                   xmid_ref, h2_ref, logit_ref, *, mod_row):
    gate = _mod_vec(gate_ref, mod_row)
    shift = _mod_vec(shift_ref, mod_row)
    scale1 = 1.0 + _mod_vec(scale_ref, mod_row)
    tm = x_ref.shape[1]
    chunk = min(tm, MIX_CHUNK)
    for r0 in range(0, tm, chunk):
        rows = slice(r0, r0 + chunk)
        y = dskip_ref[0] * u_ref[0, rows, :] + yf_ref[0, rows, :] + yb_ref[0, rows, :]
        z = _gelu_tanh(y)
        s5 = z * _sigmoid(_dot(z.astype(BF16), wglu_ref[0]) + bglu_ref[0])
        y_mix = (_dot(attn_ref[0, rows, :], wout_ref[0, 0:ATTN_WIDTH, :])
                 + _dot(s5.astype(BF16), wout_ref[0, ATTN_WIDTH:ATTN_WIDTH + S5_WIDTH, :])
                 + _dot(conv_ref[0, rows, :], wout_ref[0, ATTN_WIDTH + S5_WIDTH:, :])
                 + bout_ref[0])
        r = DEEPNORM_ALPHA * x_ref[0, rows, :] + gate * y_mix
        x_mid = _layer_norm(r) * g1_ref[0] + b1_ref[0]
        xmid_ref[0, rows, :] = x_mid
        h2 = _layer_norm(x_mid) * scale1 + shift
        h2_ref[0, rows, :] = h2.astype(BF16)
        h_hi, h_lo = _split_bf16(h2)
        logit_ref[0, :, rows] = (_dot_nt(wr_hi_ref[0], h_hi)
                                 + (_dot_nt(wr_hi_ref[0], h_lo) + _dot_nt(wr_lo_ref[0], h_hi)))


def _mixer_output(attn, yf, yb, u, conv, x, d_skip, w_glu_bf, b_glu, w_out_bf, b_out,
                  mods, ln_g, ln_b, wr_hi, wr_lo, *, layer, mod_row, tm):
    bsz, seq, _ = x.shape
    tok = lambda w: pl.BlockSpec((1, tm, w), lambda b, i: (b, i, 0))
    lay = lambda r, c: _layer_spec(layer, r, c)
    return pl.pallas_call(
        functools.partial(_mixout_kernel, mod_row=mod_row),
        grid=(bsz, seq // tm),
        in_specs=[
            tok(ATTN_WIDTH), tok(S5_WIDTH), tok(S5_WIDTH), tok(S5_WIDTH), tok(CONV_WIDTH), tok(D_MODEL),
            lay(1, S5_WIDTH), lay(S5_WIDTH, S5_WIDTH), lay(1, S5_WIDTH),
            lay(D_MODEL, D_MODEL), lay(1, D_MODEL),
            _mod_spec(layer, 2), lay(1, D_MODEL), lay(1, D_MODEL), _mod_spec(layer, 3), _mod_spec(layer, 4),
            lay(LANES, D_MODEL), lay(LANES, D_MODEL),
        ],
        out_specs=[tok(D_MODEL), tok(D_MODEL), pl.BlockSpec((1, LANES, tm), lambda b, i: (b, 0, i))],
        out_shape=[
            jax.ShapeDtypeStruct((bsz, seq, D_MODEL), F32),
            jax.ShapeDtypeStruct((bsz, seq, D_MODEL), BF16),
            jax.ShapeDtypeStruct((bsz, LANES, seq), F32),
        ],
        compiler_params=_cparams("arbitrary", "arbitrary"),
        name="mixer_output",
    )(attn, yf, yb, u, conv, x, d_skip, w_glu_bf, b_glu, w_out_bf, b_out,
      mods, ln_g, ln_b, mods, mods, wr_hi, wr_lo)


def _token_cumsum(m, tri, ones):
    nblk = m.shape[0]
    m2 = m.reshape(nblk * N_EXPERTS, LANES).astype(BF16)
    within = _dot(m2, tri).reshape(nblk, N_EXPERTS, LANES)
    tot = _dot(m2, ones).reshape(nblk, N_EXPERTS, LANES)
    offs = []
    run = jnp.zeros((N_EXPERTS, LANES), F32)
    for j in range(nblk):
        offs.append(run)
        run = run + tot[j]
    off = jnp.stack(offs, axis=0)
    return within + off, off, tot


def _router_kernel(logit_ref, tri_ref, ones_ref, aff_ref, sel_ref, pos_ref, off_ref, cnt_ref, *, cap, slot0):
    b = pl.program_id(0)
    nblk = aff_ref.shape[1]

    def soft(j, carry):
        r0 = pl.multiple_of(j * LANES, LANES)
        t = logit_ref[0, 0:N_EXPERTS, pl.ds(r0, LANES)]
        ex = jnp.exp(t - jnp.max(t, axis=0, keepdims=True))
        aff_ref[0, j] = ex / jnp.sum(ex, axis=0, keepdims=True)
        return carry

    lax.fori_loop(0, nblk, soft, 0, unroll=min(nblk, 4))
    aff = aff_ref[0]

    def enough(cand):
        cnt = jnp.sum(jnp.where(aff >= cand[None], 1.0, 0.0), axis=0)
        return jnp.sum(cnt, axis=-1, keepdims=True) >= cap

    p = jnp.full((N_EXPERTS, LANES), 2.0, F32)
    for k in range(6, -1, -1):
        cand = p * (2.0 ** -(2 ** k))
        p = jnp.where(enough(cand), p, cand)
    thr = 0.5 * p
    thr = jnp.where(enough(thr), thr, 0.0)

    def refine(_, carry):
        lo, step = carry
        cand = lo + step
        return jnp.where(enough(cand), cand, lo), 0.5 * step

    thr, _ = lax.fori_loop(0, 23, refine, (thr, 0.5 * thr))
    gt = aff > thr[None]
    eq = aff == thr[None]
    n_gt = jnp.sum(jnp.sum(jnp.where(gt, 1.0, 0.0), axis=0), axis=-1, keepdims=True)
    need = cap - n_gt
    tri = tri_ref[...]
    ones = ones_ref[...]
    cum_eq, _, _ = _token_cumsum(jnp.where(eq, 1.0, 0.0), tri, ones)
    sel = jnp.where(gt | (eq & (cum_eq <= need[None])), 1.0, 0.0)
    cum_sel, off, cnt = _token_cumsum(sel, tri, ones)
    base = (slot0 + b * cap).astype(F32)
    sel_ref[0] = sel
    pos_ref[0] = cum_sel - sel + base
    off_ref[0] = off + base
    cnt_ref[0] = cnt


def _router(logits, *, slot0):
    bsz, _, n = logits.shape
    nblk = n // LANES
    cap = CAPACITY_FACTOR * n // N_EXPERTS
    idx = np.arange(LANES)
    tri = jnp.asarray((idx[:, None] <= idx[None, :]).astype(np.float32), BF16)
    ones = jnp.ones((LANES, LANES), BF16)
    shape = (bsz, nblk, N_EXPERTS, LANES)
    out = pl.BlockSpec((1, nblk, N_EXPERTS, LANES), lambda b: (b, 0, 0, 0))
    sq = pl.BlockSpec((LANES, LANES), lambda b: (0, 0))
    return pl.pallas_call(
        functools.partial(_router_kernel, cap=cap, slot0=slot0),
        grid=(bsz,),
        in_specs=[pl.BlockSpec((1, LANES, n), lambda b: (b, 0, 0)), sq, sq],
        out_specs=[out] * 5,
        out_shape=[jax.ShapeDtypeStruct(shape, F32)] * 5,
        compiler_params=_cparams("arbitrary"),
        name="router",
    )(logits, tri, ones)


COMBINE_KW = LANES + 16
SMALL_COUNT = 32
COMBINE_SMALL_KW = SMALL_COUNT + 16
FF_TILE = 256


def _dispatch_most(tb_ref, n_entries, e, blocks):
    most = tb_ref[n_entries + blocks[0][0] * N_EXPERTS + e]
    for tile, _ in blocks[1:]:
        most = jnp.maximum(most, tb_ref[n_entries + tile * N_EXPERTS + e])
    return most


def _dispatch_run(tb_ref, e, blocks, xs_buf, g_buf, kw):
    k = lax.broadcasted_iota(jnp.int32, (kw, LANES), 0).astype(F32)
    for tile, load in blocks:
        h2_blk, pos_row, sel_row, aff_row = load()
        base = tb_ref[tile * N_EXPERTS + e]
        base16 = pl.multiple_of(lax.shift_left(lax.shift_right_logical(base, 4), 4), 16)
        hit = (k == (pos_row - base16.astype(F32))) & (sel_row > 0.5)
        sel_t = jnp.where(hit, 1.0, 0.0).astype(BF16)
        xs_buf[pl.ds(base16, kw), :] += _dot(sel_t, h2_blk).astype(BF16)
        g = jnp.sum(jnp.where(hit, aff_row, 0.0), axis=-1, keepdims=True)
        g_buf[pl.ds(base16, kw), :] += jnp.broadcast_to(g, (kw, LANES))


def _moe_kernel(*refs, n_ctx, rows, n_entries):
    tb_ref = refs[0]
    if n_ctx:
        (posx_ref, selx_ref, affx_ref, h2x_ref, posc_ref, selc_ref, affc_ref, h2c_ref,
         wg_ref, wu_ref, wd_ref, y_ref, acc_scr, g_scr, xs_scr) = refs[1:]
    else:
        (posx_ref, selx_ref, affx_ref, h2x_ref, wg_ref, wu_ref, wd_ref, y_ref,
         acc_scr, g_scr, xs_scr) = refs[1:]
    p = pl.program_id(0)
    s = pl.program_id(1)
    last = pl.num_programs(1) - 1
    fill = lax.rem(p, 2)
    use = 1 - fill
    blk_per_step = posx_ref.shape[2]
    e = jnp.minimum(p, N_EXPERTS - 1)
    compacting = p < N_EXPERTS
    computing = p >= 1
    xs_buf = xs_scr.at[fill]
    g_buf = g_scr.at[fill]

    @pl.when(compacting & (s == 0))
    def _():
        xs_buf[...] = jnp.zeros(xs_buf.shape, BF16)
        g_buf[...] = jnp.zeros(g_buf.shape, F32)
        if n_ctx:
            bsz_c, blk_c = posc_ref.shape[0], posc_ref.shape[2]

            def load_c(b, i):
                return lambda: (h2c_ref[b, i * LANES:(i + 1) * LANES, :], posc_ref[b, 0, i:i + 1, :],
                                selc_ref[b, 0, i:i + 1, :], affc_ref[b, 0, i:i + 1, :])

            blocks_c = [((last + 1) * blk_per_step + b * blk_c + i, load_c(b, i))
                        for b in range(bsz_c) for i in range(blk_c)]
            _dispatch_run(tb_ref, e, blocks_c, xs_buf, g_buf, COMBINE_KW)

    def load_x(i):
        return lambda: (h2x_ref[0, i * LANES:(i + 1) * LANES, :], posx_ref[0, 0, i:i + 1, :],
                        selx_ref[0, 0, i:i + 1, :], affx_ref[0, 0, i:i + 1, :])

    blocks = [(s * blk_per_step + i, load_x(i)) for i in range(blk_per_step)]
    short = _dispatch_most(tb_ref, n_entries, e, blocks) <= SMALL_COUNT

    def ffn_tile():
        wg = wg_ref[0, 0].astype(BF16)
        wu = wu_ref[0, 0].astype(BF16)
        wd = wd_ref[0, 0].astype(BF16)
        half = rows // 2
        for r0 in (0, half):
            xs = xs_scr[use, r0:r0 + half, :]
            hid = _silu(_dot(xs, wg)) * _dot(xs, wu)
            part = _dot(hid.astype(BF16), wd)
            acc_scr[r0:r0 + half, :] = jnp.where(s == 0, part, acc_scr[r0:r0 + half, :] + part)

    for window, fits in ((COMBINE_SMALL_KW, short), (COMBINE_KW, jnp.logical_not(short))):
        @pl.when(compacting & computing & fits)
        def _(window=window):
            ffn_tile()
            _dispatch_run(tb_ref, e, blocks, xs_buf, g_buf, window)

        @pl.when(compacting & jnp.logical_not(computing) & fits)
        def _(window=window):
            _dispatch_run(tb_ref, e, blocks, xs_buf, g_buf, window)

    @pl.when(jnp.logical_not(compacting))
    def _():
        ffn_tile()

    @pl.when(computing & (s == last))
    def _():
        g = g_scr[use, 0:rows, :]
        gated = acc_scr[...] * jnp.concatenate([g] * (D_MODEL // LANES), axis=-1)
        y_ref[0, 0:rows, :] = gated.astype(BF16)
        y_ref[0, rows:, :] = jnp.zeros((y_ref.shape[1] - rows, D_MODEL), BF16)


def _moe_experts(tb, route_x, h2x, route_c, h2c, w_gate, w_up, w_down, *, layer):
    bsz, n, d = h2x.shape
    n_ff = EXPERT_FF // FF_TILE
    x_step = bsz * n // n_ff
    steps_per_b = n // x_step
    blk_per_step = x_step // LANES
    n_ctx = 0 if h2c is None else h2c.shape[0] * h2c.shape[1]
    rows = bsz * CAPACITY_FACTOR * n // N_EXPERTS + CAPACITY_FACTOR * n_ctx // N_EXPERTS
    rows_pad = rows + COMBINE_KW
    last_e = N_EXPERTS - 1

    def xs_idx(p, s):
        sc = jnp.where(p > last_e, n_ff - 1, s)
        return sc // steps_per_b, sc % steps_per_b

    def route_x_spec():
        return pl.BlockSpec((1, 1, blk_per_step, LANES),
                            lambda p, s, tb: (xs_idx(p, s)[0], jnp.minimum(p, last_e), xs_idx(p, s)[1], 0))

    in_specs = [route_x_spec(), route_x_spec(), route_x_spec(),
                pl.BlockSpec((1, x_step, d), lambda p, s, tb: (xs_idx(p, s)[0], xs_idx(p, s)[1], 0))]
    args = list(route_x) + [h2x]
    if n_ctx:
        bc, nc, _ = h2c.shape
        rc = pl.BlockSpec((bc, 1, nc // LANES, LANES), lambda p, s, tb: (0, jnp.minimum(p, last_e), 0, 0))
        in_specs += [rc, rc, rc, pl.BlockSpec((bc, nc, d), lambda p, s, tb: (0, 0, 0))]
        args += list(route_c) + [h2c]
    ffn_e = lambda p: jnp.maximum(p - 1, 0)
    in_specs += [pl.BlockSpec((1, 1, d, FF_TILE), lambda p, s, tb: (layer, ffn_e(p), 0, s)),
                 pl.BlockSpec((1, 1, d, FF_TILE), lambda p, s, tb: (layer, ffn_e(p), 0, s)),
                 pl.BlockSpec((1, 1, FF_TILE, d), lambda p, s, tb: (layer, ffn_e(p), s, 0))]
    args += [w_gate, w_up, w_down]
    return pl.pallas_call(
        functools.partial(_moe_kernel, n_ctx=n_ctx, rows=rows, n_entries=tb.shape[0] // 2),
        grid_spec=pltpu.PrefetchScalarGridSpec(
            num_scalar_prefetch=1,
            grid=(N_EXPERTS + 1, n_ff),
            in_specs=in_specs,
            out_specs=pl.BlockSpec((1, rows_pad, d), lambda p, s, tb: (ffn_e(p), 0, 0)),
            scratch_shapes=[pltpu.VMEM((rows, d), F32), pltpu.VMEM((2, rows_pad, LANES), F32),
                            pltpu.VMEM((2, rows_pad, d), BF16)],
        ),
        out_shape=jax.ShapeDtypeStruct((N_EXPERTS, rows_pad, d), BF16),
        compiler_params=_cparams("arbitrary", "arbitrary"),
        name="moe_experts",
    )(tb, *args)


def _combine_kernel(tb_ref, pos_ref, sel_ref, xmid_ref, gate_ref, g_ref, b_ref, y_hbm, o_ref,
                    y_scr, stack_scr, sem, *, cap, slot0, blk_per_step, nblk, n_entries, mod_row):
    b = pl.program_id(0)
    j = pl.program_id(1)
    win = y_scr.shape[1]
    gate = _mod_vec(gate_ref, mod_row)

    @pl.when(j == 0)
    def _():
        row0 = pl.multiple_of(slot0 + b * cap, 16)
        cp = pltpu.make_async_copy(y_hbm.at[:, pl.ds(row0, win), :], y_scr, sem)
        cp.start()
        cp.wait()

    set_base = (slot0 + b * cap).astype(F32)
    tn_dims = (((0,), (0,)), ((), ()))
    for i in range(blk_per_step):
        entry = (b * nblk + j * blk_per_step + i) * N_EXPERTS
        rows = slice(i * LANES, (i + 1) * LANES)

        def hits(e, kw, entry=entry, i=i):
            base = tb_ref[entry + e] - (slot0 + b * cap)
            base16 = pl.multiple_of(lax.shift_left(lax.shift_right_logical(base, 4), 4), 16)
            k = lax.broadcasted_iota(jnp.int32, (kw, LANES), 0).astype(F32)
            rel = pos_ref[0, i, e:e + 1, :] - (set_base + base16.astype(F32))
            hit = (k == rel) & (sel_ref[0, i, e:e + 1, :] > 0.5)
            return base16, jnp.where(hit, 1.0, 0.0).astype(BF16)

        def finish(moe, rows=rows):
            r = DEEPNORM_ALPHA * xmid_ref[0, rows, :] + gate * moe
            o_ref[0, rows, :] = _layer_norm(r) * g_ref[0] + b_ref[0]

        most = tb_ref[n_entries + entry]
        for e in range(1, N_EXPERTS):
            most = jnp.maximum(most, tb_ref[n_entries + entry + e])

        @pl.when(most <= SMALL_COUNT)
        def _():
            sel_all = []
            for e in range(N_EXPERTS):
                base16, sel_t = hits(e, COMBINE_SMALL_KW)
                sel_all.append(sel_t)
                stack_scr[e * COMBINE_SMALL_KW:(e + 1) * COMBINE_SMALL_KW, :] = (
                    y_scr[e, pl.ds(base16, COMBINE_SMALL_KW), :])
            finish(lax.dot_general(jnp.concatenate(sel_all, axis=0), stack_scr[...], tn_dims,
                                   preferred_element_type=F32))

        @pl.when(most > SMALL_COUNT)
        def _():
            acc = jnp.zeros((LANES, D_MODEL), F32)
            for e in range(N_EXPERTS):
                base16, sel_t = hits(e, COMBINE_KW)
                acc = acc + lax.dot_general(sel_t, y_scr[e, pl.ds(base16, COMBINE_KW), :], tn_dims,
                                            preferred_element_type=F32)
            finish(acc)


def _combine_post(tb, pos, sel, x_mid, mods, ln_g, ln_b, y, *, layer, mod_row, slot0, tm):
    bsz, n, d = x_mid.shape
    nblk = n // LANES
    cap = CAPACITY_FACTOR * n // N_EXPERTS
    blk_per_step = tm // LANES
    tok = pl.BlockSpec((1, tm, d), lambda b, j, tb: (b, j, 0))
    route = pl.BlockSpec((1, blk_per_step, N_EXPERTS, LANES), lambda b, j, tb: (b, j, 0, 0))
    vec = _layer_spec(layer, 1, d)
    return pl.pallas_call(
        functools.partial(_combine_kernel, cap=cap, slot0=slot0, blk_per_step=blk_per_step, nblk=nblk,
                          n_entries=bsz * nblk * N_EXPERTS, mod_row=mod_row),
        grid_spec=pltpu.PrefetchScalarGridSpec(
            num_scalar_prefetch=1,
            grid=(bsz, n // tm),
            in_specs=[route, route, tok, _mod_spec(layer, 5), vec, vec,
                      pl.BlockSpec(memory_space=pl.ANY)],
            out_specs=tok,
            scratch_shapes=[pltpu.VMEM((N_EXPERTS, cap + COMBINE_KW, d), BF16),
                            pltpu.VMEM((N_EXPERTS * COMBINE_SMALL_KW, d), BF16),
                            pltpu.SemaphoreType.DMA(())],
        ),
        out_shape=jax.ShapeDtypeStruct(x_mid.shape, F32),
        compiler_params=_cparams("arbitrary", "arbitrary"),
        name="combine_post",
    )(tb, pos, sel, x_mid, mods, ln_g, ln_b, y)


def _tile_table(*lane_replicated):
    return jnp.concatenate([a[..., 0].astype(jnp.int32).reshape(-1) for a in lane_replicated])


def _expert_major(a):
    return jnp.swapaxes(a, 1, 2)


def kernel(x, c, ctx, c_ctx, w_mod, b_mod, w_in, b_in, attn_sink, s5_lam_re, s5_lam_im, s5_log_dt,
           s5_b_re, s5_b_im, s5_c_re, s5_c_im, s5_d, s5_w_glu, s5_b_glu, conv_w_dw, conv_b_dw,
           conv_ln_g, conv_ln_b, conv_w_pw, conv_b_pw, w_out, b_out, ln1_g, ln1_b, w_router,
           exp_w_gate, exp_w_up, exp_w_down, ln2_g, ln2_b):
    bsz, seq, d = x.shape
    lc = ctx.shape[1]
    tm_x, tm_c = min(1024, seq), lc

    cond = jnp.zeros((SUBLANES, d), F32).at[:bsz].set(c).at[bsz].set(c_ctx)
    mods = _modulation(cond, w_mod, b_mod)
    cos_t, sin_t = _rope_tables(seq)
    s5_mask = _s5_mask()
    zero_state = jnp.zeros((2 * bsz, SUBLANES, 2 * LANES), F32)

    vec3 = lambda a: a.reshape(DEPTH, 1, -1)
    w_in_bf, w_out_bf = w_in.astype(BF16), w_out.astype(BF16)
    w_glu_bf, w_pw_bf = s5_w_glu.astype(BF16), conv_w_pw.astype(BF16)
    wr = jnp.pad(jnp.swapaxes(w_router, 1, 2), ((0, 0), (0, LANES - N_EXPERTS), (0, 0)))
    wr_hi, wr_lo = _split_bf16(wr)
    sink_rep = jnp.broadcast_to(attn_sink[:, :, None], (DEPTH, N_Q_HEADS, LANES))
    bst, a_tiles, cwide = jax.vmap(_s5_params)(s5_lam_re, s5_lam_im, s5_log_dt, s5_b_re, s5_b_im,
                                               s5_c_re, s5_c_im)
    w_dw = jnp.pad(conv_w_dw.reshape(DEPTH, CONV_K, CONV_WIDTH), ((0, 0), (1, 0), (0, 0)))
    conv_args = (w_dw, vec3(conv_b_dw), vec3(conv_ln_g), vec3(conv_ln_b), w_pw_bf, vec3(conv_b_pw))
    mix_args = (vec3(s5_d), w_glu_bf, vec3(s5_b_glu), w_out_bf, vec3(b_out), mods, vec3(ln1_g), vec3(ln1_b),
                wr_hi, wr_lo)
    b_in3, ln2_g3, ln2_b3 = vec3(b_in), vec3(ln2_g), vec3(ln2_b)

    xc = ctx
    ctx_row = bsz
    for l in range(DEPTH):
        last = l == DEPTH - 1
        q, k, v, u, cg = _in_projection(x, mods, w_in_bf, b_in3, cos_t, sin_t,
                                        layer=l, mod_row=None, rope=True, tm=tm_x)
        q_c, k_c, v_c, u_c, cg_c = _in_projection(xc, mods, w_in_bf, b_in3, cos_t, sin_t,
                                                  layer=l, mod_row=ctx_row, rope=False, tm=tm_c)

        attn_x = _attention(q, k, v, k_c, v_c, sink_rep, layer=l, window=True)
        yf_c, yb_c, h_ctx = _s5_scan(u_c, zero_state, s5_mask, bst, a_tiles, cwide, layer=l, tc=lc)
        yf, yb, _ = _s5_scan(u, h_ctx, s5_mask, bst, a_tiles, cwide, layer=l, tc=256)
        conv_x = _conformer_conv(cg, *conv_args, layer=l, tm=256)
        x_mid, h2, logits = _mixer_output(attn_x, yf, yb, u, conv_x, x, *mix_args,
                                          layer=l, mod_row=None, tm=tm_x)
        aff_x, sel_x, pos_x, off_x, cnt_x = _router(logits, slot0=0)
        route_x = tuple(_expert_major(a) for a in (pos_x, sel_x, aff_x))
        tb_x = _tile_table(off_x, cnt_x)
        if not last:
            attn_c = _attention(q_c, None, None, k_c, v_c, sink_rep, layer=l, window=False)
            conv_c = _conformer_conv(cg_c, *conv_args, layer=l, tm=lc)
            xc_mid, hc2, logits_c = _mixer_output(attn_c, yf_c, yb_c, u_c, conv_c, xc, *mix_args,
                                                  layer=l, mod_row=ctx_row, tm=tm_c)
            slot0_c = bsz * CAPACITY_FACTOR * seq // N_EXPERTS
            aff_c, sel_c, pos_c, off_c, cnt_c = _router(logits_c, slot0=slot0_c)
            route_c = tuple(_expert_major(a) for a in (pos_c, sel_c, aff_c))
            tb_c = _tile_table(off_c, cnt_c)
            y = _moe_experts(_tile_table(off_x, off_c, cnt_x, cnt_c), route_x, h2, route_c, hc2,
                             exp_w_gate, exp_w_up, exp_w_down, layer=l)
            xc = _combine_post(tb_c, pos_c, sel_c, xc_mid, mods, ln2_g3, ln2_b3, y,
                               layer=l, mod_row=ctx_row, slot0=slot0_c, tm=lc)
        else:
            y = _moe_experts(tb_x, route_x, h2, None, None, exp_w_gate, exp_w_up, exp_w_down, layer=l)
        x = _combine_post(tb_x, pos_x, sel_x, x_mid, mods, ln2_g3, ln2_b3, y,
                          layer=l, mod_row=None, slot0=0, tm=min(512, seq))
    return x
```

```python
import functools
import math

import jax
import jax.numpy as jnp
import numpy as np
from jax import lax
from jax.experimental import pallas as pl
from jax.experimental.pallas import tpu as pltpu

D_MODEL = 1024
DEPTH = 2
GRID_W = 64
HEAD_DIM = 64
ATTN_WIDTH = D_MODEL // 2
N_Q_HEADS = ATTN_WIDTH // HEAD_DIM
N_KV_HEADS = N_Q_HEADS // 4
Q_PER_KV = N_Q_HEADS // N_KV_HEADS
KV_WIDTH = N_KV_HEADS * HEAD_DIM
BLOCK = 128
ROPE_BASE = 10000.0
NEG_INF = -1e30
S5_WIDTH = D_MODEL // 4
S5_GROUP = 16
S5_GROUPS = S5_WIDTH // S5_GROUP
S5_STATE = 64
CONV_WIDTH = D_MODEL - ATTN_WIDTH - S5_WIDTH
CONV_K = 31
CONV_HALO = 16
Q_END = ATTN_WIDTH
K_END = Q_END + KV_WIDTH
V_END = K_END + KV_WIDTH
U_END = V_END + S5_WIDTH
IN_WIDTH = U_END + 2 * CONV_WIDTH
N_EXPERTS = 16
EXPERT_FF = 2 * D_MODEL
CAPACITY_FACTOR = 2
DEEPNORM_ALPHA = (2.0 * DEPTH) ** 0.25
LN_EPS = 1e-5

LANES = 128
SUBLANES = 8
S5_CHUNKS = S5_WIDTH * S5_STATE // S5_GROUP // LANES
CH_PER_CHUNK = S5_WIDTH // S5_CHUNKS
VMEM_LIMIT = 56 * 1024 * 1024
MIX_CHUNK = 256

F32 = jnp.float32
BF16 = jnp.bfloat16


def _cparams(*sem):
    return pltpu.CompilerParams(dimension_semantics=sem, vmem_limit_bytes=VMEM_LIMIT)


def _dot(a, b):
    return jnp.dot(a, b, preferred_element_type=F32)


def _dot_nt(a, b):
    return lax.dot_general(a, b, (((1,), (1,)), ((), ())), preferred_element_type=F32)


def _split_bf16(x):
    hi = x.astype(BF16)
    lo = (x - hi.astype(F32)).astype(BF16)
    return hi, lo


def _dot3(a, b_hi, b_lo):
    a_hi, a_lo = _split_bf16(a)
    return _dot(a_hi, b_hi) + (_dot(a_lo, b_hi) + _dot(a_hi, b_lo))


def _sigmoid(x):
    return 1.0 / (1.0 + jnp.exp(-x))


def _silu(x):
    return x * _sigmoid(x)


def _gelu_tanh(x):
    c = math.sqrt(2.0 / math.pi)
    return 0.5 * x * (1.0 + jnp.tanh(c * (x + 0.044715 * (x * x * x))))


def _layer_norm(x):
    mu = jnp.mean(x, axis=-1, keepdims=True)
    xc = x - mu
    var = jnp.mean(xc * xc, axis=-1, keepdims=True)
    return xc * lax.rsqrt(var + LN_EPS)


def _mod_kernel(c_ref, w_ref, b_ref, o_ref):
    s = _silu(c_ref[...])
    w = w_ref[0]
    w_hi, w_lo = _split_bf16(w)
    o_ref[0] = _dot3(s, w_hi, w_lo) + b_ref[0]


def _modulation(cond, w_mod, b_mod):
    tn = 1536
    n = w_mod.shape[-1]
    return pl.pallas_call(
        _mod_kernel,
        grid=(DEPTH, n // tn),
        in_specs=[
            pl.BlockSpec((SUBLANES, D_MODEL), lambda l, j: (0, 0)),
            pl.BlockSpec((1, D_MODEL, tn), lambda l, j: (l, 0, j)),
            pl.BlockSpec((1, 1, tn), lambda l, j: (l, 0, j)),
        ],
        out_specs=pl.BlockSpec((1, SUBLANES, tn), lambda l, j: (l, 0, j)),
        out_shape=jax.ShapeDtypeStruct((DEPTH, SUBLANES, n), F32),
        compiler_params=_cparams("arbitrary", "arbitrary"),
        name="modulation",
    )(cond, w_mod, b_mod.reshape(DEPTH, 1, n))


def _rope_chunk(x, cos, sin_signed):
    lane = lax.broadcasted_iota(jnp.int32, x.shape, 1)
    first = (lane % 32) < 16
    partner = jnp.where(first, pltpu.roll(x, LANES - 16, 1), pltpu.roll(x, 16, 1))
    return x * cos + partner * sin_signed


def _mod_spec(layer, k):
    return pl.BlockSpec((1, SUBLANES, D_MODEL), lambda *_: (layer, 0, k))


def _layer_spec(layer, *shape):
    return pl.BlockSpec((1,) + shape, lambda *_: (layer,) + (0,) * len(shape))


def _mod_vec(ref, mod_row):
    row = pl.program_id(0) if mod_row is None else mod_row
    return ref[0, pl.ds(row, 1), :]


def _inproj_kernel(x_ref, shift_ref, scale_ref, w_ref, b_ref, cos_ref, sin_ref,
                   q_ref, k_ref, v_ref, u_ref, cg_ref, *, rope, mod_row):
    shift = _mod_vec(shift_ref, mod_row)
    scale1 = 1.0 + _mod_vec(scale_ref, mod_row)
    tm = x_ref.shape[1]
    chunk = min(tm, MIX_CHUNK)
    scale = HEAD_DIM ** -0.5
    for r0 in range(0, tm, chunk):
        rows = slice(r0, r0 + chunk)
        h = _layer_norm(x_ref[0, rows, :]) * scale1 + shift
        p = _dot(h.astype(BF16), w_ref[0]) + b_ref[0]
        if rope:
            cos = cos_ref[rows, :]
            sin = sin_ref[rows, :]
        for j in range(ATTN_WIDTH // LANES):
            qc = p[:, j * LANES:(j + 1) * LANES]
            if rope:
                qc = _rope_chunk(qc, cos, sin)
            q_ref[0, rows, j * LANES:(j + 1) * LANES] = (qc * scale).astype(BF16)
        kc = p[:, Q_END:K_END]
        if rope:
            kc = _rope_chunk(kc, cos, sin)
        k_ref[0, rows, :] = kc.astype(BF16)
        v_ref[0, rows, :] = p[:, K_END:V_END].astype(BF16)
        u_ref[0, rows, :] = p[:, V_END:U_END]
        a = p[:, U_END:U_END + CONV_WIDTH]
        g = p[:, U_END + CONV_WIDTH:]
        cg_ref[0, rows, :] = a * _sigmoid(g)


def _in_projection(x, mods, w_in_bf, b_in, cos_t, sin_t, *, layer, mod_row, rope, tm):
    bsz, seq, _ = x.shape
    tok = lambda w: pl.BlockSpec((1, tm, w), lambda b, i: (b, i, 0))
    tab = pl.BlockSpec((tm, LANES), lambda b, i: (i, 0))
    return pl.pallas_call(
        functools.partial(_inproj_kernel, rope=rope, mod_row=mod_row),
        grid=(bsz, seq // tm),
        in_specs=[
            tok(D_MODEL), _mod_spec(layer, 0), _mod_spec(layer, 1),
            _layer_spec(layer, D_MODEL, IN_WIDTH), _layer_spec(layer, 1, IN_WIDTH),
            tab, tab,
        ],
        out_specs=[tok(ATTN_WIDTH), tok(KV_WIDTH), tok(KV_WIDTH), tok(S5_WIDTH), tok(CONV_WIDTH)],
        out_shape=[
            jax.ShapeDtypeStruct((bsz, seq, ATTN_WIDTH), BF16),
            jax.ShapeDtypeStruct((bsz, seq, KV_WIDTH), BF16),
            jax.ShapeDtypeStruct((bsz, seq, KV_WIDTH), BF16),
            jax.ShapeDtypeStruct((bsz, seq, S5_WIDTH), F32),
            jax.ShapeDtypeStruct((bsz, seq, CONV_WIDTH), F32),
        ],
        compiler_params=_cparams("arbitrary", "arbitrary"),
        name="in_projection",
    )(x, mods, mods, w_in_bf, b_in.reshape(DEPTH, 1, IN_WIDTH), cos_t, sin_t)


def _rope_tables(seq):
    f = HEAD_DIM // 4
    inv_freq = jnp.asarray(ROPE_BASE, F32) ** (-jnp.arange(f, dtype=F32) / f)

    def axis_tables(n_pos):
        ang = jnp.arange(n_pos, dtype=jnp.int32).astype(F32)[:, None] * inv_freq[None, :]
        cos, sin = jnp.cos(ang), jnp.sin(ang)
        return jnp.concatenate([cos, cos], axis=-1), jnp.concatenate([-sin, sin], axis=-1)

    n_rows = seq // GRID_W
    cos_r, sin_r = axis_tables(n_rows)
    cos_c, sin_c = axis_tables(GRID_W)
    by_row = lambda a: jnp.repeat(a, GRID_W, axis=0)
    by_col = lambda a: jnp.tile(a, (n_rows, 1))
    cos = jnp.concatenate([by_row(cos_r), by_col(cos_c)], axis=-1)
    sin = jnp.concatenate([by_row(sin_r), by_col(sin_c)], axis=-1)
    return jnp.tile(cos, (1, LANES // HEAD_DIM)), jnp.tile(sin, (1, LANES // HEAD_DIM))


ATTN_SUBBLOCKS = 8


def _attn_kernel(*refs, window):
    if window:
        n_kv = ATTN_SUBBLOCKS + 2
        q_ref = refs[0]
        k_refs = refs[1:1 + n_kv]
        v_refs = refs[1 + n_kv:1 + 2 * n_kv]
        kx_ref, vx_ref, sink_ref, o_ref = refs[1 + 2 * n_kv:]
        subs = ATTN_SUBBLOCKS
    else:
        q_ref, kx_ref, vx_ref, sink_ref, o_ref = refs
        subs = 1
    step = pl.program_id(1)
    n_steps = pl.num_programs(1)
    tq = q_ref.shape[1] // subs
    rows = Q_PER_KV * tq
    if window:
        row = lax.broadcasted_iota(jnp.int32, (rows, BLOCK), 0) % tq
        col = lax.broadcasted_iota(jnp.int32, (rows, BLOCK), 1)

    def with_ones(v, ks):
        return jnp.concatenate([v[:, ks], jnp.ones((v.shape[0], HEAD_DIM), BF16)], axis=-1)

    for sub in range(subs):
        q = q_ref[0, sub * tq:(sub + 1) * tq, :]
        if window:
            kp_ref, kc_ref, kn_ref = k_refs[sub:sub + 3]
            vp_ref, vc_ref, vn_ref = v_refs[sub:sub + 3]
            ok_prev = (col >= row) if sub > 0 else (col >= row) & (step > 0)
            ok_next = (col <= row) if sub < subs - 1 else (col <= row) & (step < n_steps - 1)
        outs = []
        for g in range(N_KV_HEADS):
            ks = slice(g * HEAD_DIM, (g + 1) * HEAD_DIM)
            heads = range(g * Q_PER_KV, (g + 1) * Q_PER_KV)
            qs = jnp.concatenate([q[:, h * HEAD_DIM:(h + 1) * HEAD_DIM] for h in heads], axis=0)
            sink = jnp.concatenate([jnp.broadcast_to(sink_ref[0, h:h + 1, 0:1], (tq, 1)) for h in heads], axis=0)
            lx = kx_ref.shape[1]
            s_x = _dot_nt(qs, kx_ref[0][:, ks])
            m_el = s_x[:, 0:LANES]
            for c in range(1, lx // LANES):
                m_el = jnp.maximum(m_el, s_x[:, c * LANES:(c + 1) * LANES])
            if window:
                s_p = jnp.where(ok_prev, _dot_nt(qs, kp_ref[0][:, ks]), NEG_INF)
                s_c = _dot_nt(qs, kc_ref[0][:, ks])
                s_n = jnp.where(ok_next, _dot_nt(qs, kn_ref[0][:, ks]), NEG_INF)
                m_el = jnp.maximum(jnp.maximum(m_el, s_c), jnp.maximum(s_p, s_n))
            m = jnp.maximum(jnp.max(m_el, axis=-1, keepdims=True), sink)
            acc = _dot(jnp.exp(s_x - m).astype(BF16), with_ones(vx_ref[0], ks))
            if window:
                for s_w, v_ref in ((s_p, vp_ref), (s_c, vc_ref), (s_n, vn_ref)):
                    acc = acc + _dot(jnp.exp(s_w - m).astype(BF16), with_ones(v_ref[0], ks))
            o = acc[:, 0:HEAD_DIM] / (acc[:, HEAD_DIM:] + jnp.exp(sink - m))
            outs += [o[i * tq:(i + 1) * tq] for i in range(Q_PER_KV)]
        o_ref[0, sub * tq:(sub + 1) * tq, :] = jnp.concatenate(outs, axis=-1).astype(BF16)


def _attention(q, k, v, k_ctx, v_ctx, sink_rep, *, layer, window):
    bsz, seq, _ = q.shape
    lc = k_ctx.shape[1]
    ctx_spec = pl.BlockSpec((1, lc, KV_WIDTH), lambda b, i: (b, 0, 0))
    sink_spec = _layer_spec(layer, N_Q_HEADS, LANES)
    if window:
        tq = BLOCK * ATTN_SUBBLOCKS
        nb = seq // BLOCK

        def kv_spec(j):
            return pl.BlockSpec((1, BLOCK, KV_WIDTH),
                                lambda b, i: (b, jnp.clip(ATTN_SUBBLOCKS * i - 1 + j, 0, nb - 1), 0))

        kv_specs = [kv_spec(j) for j in range(ATTN_SUBBLOCKS + 2)]
        in_specs = ([pl.BlockSpec((1, tq, ATTN_WIDTH), lambda b, i: (b, i, 0))] + kv_specs + kv_specs
                    + [ctx_spec, ctx_spec, sink_spec])
        args = (q,) + (k,) * len(kv_specs) + (v,) * len(kv_specs) + (k_ctx, v_ctx, sink_rep)
    else:
        tq = seq
        in_specs = [pl.BlockSpec((1, tq, ATTN_WIDTH), lambda b, i: (b, i, 0)), ctx_spec, ctx_spec, sink_spec]
        args = (q, k_ctx, v_ctx, sink_rep)
    return pl.pallas_call(
        functools.partial(_attn_kernel, window=window),
        grid=(bsz, seq // tq),
        in_specs=in_specs,
        out_specs=pl.BlockSpec((1, tq, ATTN_WIDTH), lambda b, i: (b, i, 0)),
        out_shape=jax.ShapeDtypeStruct((bsz, seq, ATTN_WIDTH), BF16),
        compiler_params=_cparams("arbitrary", "arbitrary"),
        name="window_attention" if window else "context_attention",
    )(*args)


def _s5_kernel(uf_ref, ub_ref, h0_ref, mask_ref, bst_ref, a_ref, cw_ref,
               yf_ref, yb_ref, hfin_ref, lhs_scr, bu_scr, hs_scr, h_scr, *, bsz, tc):
    i = pl.program_id(0)

    @pl.when(i == 0)
    def _():
        h_scr[...] = h0_ref[...]

    mask = mask_ref[...]
    n_chain = 2 * bsz

    for d, u_ref in enumerate((uf_ref, ub_ref)):
        for b in range(bsz):
            for j in range(tc // 2):
                pair = [jnp.broadcast_to(u_ref[b, 2 * j + k:2 * j + k + 1, :], (SUBLANES, S5_WIDTH)) * mask
                        for k in range(2)]
                lhs_scr[d * bsz + b, 2 * SUBLANES * j:2 * SUBLANES * (j + 1), :] = (
                    jnp.concatenate(pair, axis=0).astype(BF16))
    for c in range(n_chain):
        bu_scr[c] = _dot(lhs_scr[c], bst_ref[0, c // bsz])

    a_re = [a_ref[0, d, 0] for d in range(2)]
    a_im = [a_ref[0, d, 1] for d in range(2)]

    def step(t, carry):
        new = []
        for c in range(n_chain):
            d = c // bsz
            tt = t if d == 0 else tc - 1 - t
            r0 = pl.multiple_of(tt * SUBLANES, SUBLANES)
            h_re, h_im = carry[2 * c], carry[2 * c + 1]
            n_re = a_re[d] * h_re - a_im[d] * h_im + bu_scr[c, pl.ds(r0, SUBLANES), 0:LANES]
            n_im = a_re[d] * h_im + a_im[d] * h_re + bu_scr[c, pl.ds(r0, SUBLANES), LANES:2 * LANES]
            hs_scr[c, 0, pl.ds(r0, SUBLANES), :] = n_re
            hs_scr[c, 1, pl.ds(r0, SUBLANES), :] = n_im
            new += [n_re, n_im]
        return tuple(new)

    init = []
    for c in range(n_chain):
        init += [h_scr[c, :, 0:LANES], h_scr[c, :, LANES:2 * LANES]]
    fin = lax.fori_loop(0, tc, step, tuple(init), unroll=8)
    for c in range(n_chain):
        h_scr[c, :, 0:LANES] = fin[2 * c]
        h_scr[c, :, LANES:2 * LANES] = fin[2 * c + 1]
    hfin_ref[...] = h_scr[...]

    for d, y_ref in enumerate((yf_ref, yb_ref)):
        for b in range(bsz):
            c = d * bsz + b
            parts = [hs_scr[c, ri, pl.ds(s, tc, stride=SUBLANES), :].astype(BF16)
                     for s in range(SUBLANES) for ri in range(2)]
            y_ref[b] = _dot(jnp.concatenate(parts, axis=-1), cw_ref[0, d])


def _s5_scan(u, h0, mask, bst, a_tiles, cwide, *, layer, tc):
    bsz, seq, _ = u.shape
    nch = seq // tc
    full = lambda shape: pl.BlockSpec(shape, lambda i: (0,) * len(shape))
    fwd = pl.BlockSpec((bsz, tc, S5_WIDTH), lambda i: (0, i, 0))
    bwd = pl.BlockSpec((bsz, tc, S5_WIDTH), lambda i: (0, nch - 1 - i, 0))
    state = (2 * bsz, SUBLANES, 2 * LANES)
    rows = SUBLANES * tc
    return pl.pallas_call(
        functools.partial(_s5_kernel, bsz=bsz, tc=tc),
        grid=(nch,),
        in_specs=[fwd, bwd, full(state), full(mask.shape), _layer_spec(layer, *bst.shape[1:]),
                  _layer_spec(layer, *a_tiles.shape[1:]), _layer_spec(layer, *cwide.shape[1:])],
        out_specs=[fwd, bwd, full(state)],
        out_shape=[jax.ShapeDtypeStruct(u.shape, F32), jax.ShapeDtypeStruct(u.shape, F32),
                   jax.ShapeDtypeStruct(state, F32)],
        scratch_shapes=[pltpu.VMEM((2 * bsz, rows, S5_WIDTH), BF16),
                        pltpu.VMEM((2 * bsz, rows, 2 * LANES), F32),
                        pltpu.VMEM((2 * bsz, 2, rows, LANES), F32),
                        pltpu.VMEM(state, F32)],
        compiler_params=_cparams("arbitrary"),
        name="s5_scan",
    )(u, u, h0, mask, bst, a_tiles, cwide)


def _s5_mask():
    m = np.arange(S5_WIDTH)[None, :] // CH_PER_CHUNK == np.arange(SUBLANES)[:, None]
    return jnp.asarray(m.astype(np.float32), F32)


def _s5_out_mask():
    row_group = (np.arange(SUBLANES)[:, None, None, None] * (S5_GROUPS // SUBLANES)
                 + np.arange(S5_GROUPS // SUBLANES)[None, None, :, None]
                 + np.zeros((1, 2, 1, S5_STATE), np.int64)).reshape(-1)
    col_group = np.arange(S5_WIDTH) // S5_GROUP
    return (row_group[:, None] == col_group[None, :]).astype(np.float32)


_S5_OUT_MASK = _s5_out_mask()


def _s5_params(lam_re, lam_im, log_dt, b_re, b_im, c_re, c_im):
    dt = jnp.exp(log_dt)[..., None]
    mag = jnp.exp(lam_re * dt)
    l_re = mag * jnp.cos(lam_im * dt)
    l_im = mag * jnp.sin(lam_im * dt)
    den = lam_re * lam_re + lam_im * lam_im
    f_re = ((l_re - 1.0) * lam_re + l_im * lam_im) / den
    f_im = (l_im * lam_re - (l_re - 1.0) * lam_im) / den
    bb_re = f_re[..., None] * b_re - f_im[..., None] * b_im
    bb_im = f_re[..., None] * b_im + f_im[..., None] * b_re
    a_tiles = jnp.stack([l_re.reshape(2, SUBLANES, LANES), l_im.reshape(2, SUBLANES, LANES)], axis=1)

    half = S5_GROUPS // SUBLANES
    eye = jnp.eye(half, dtype=F32)

    def in_mat(bb):
        t = bb.reshape(2, SUBLANES, half, S5_STATE, S5_GROUP)
        m = jnp.einsum('dsgpc,gh->dsgchp', t, eye)
        return m.reshape(2, S5_WIDTH, half * S5_STATE)

    bst = jnp.concatenate([in_mat(bb_re), in_mat(bb_im)], axis=-1).astype(BF16)
    t = jnp.stack([c_re, -c_im], axis=1).astype(BF16)
    t = t.reshape(2, 2, SUBLANES, half, S5_GROUP, S5_STATE)
    rows = jnp.transpose(t, (0, 2, 1, 3, 5, 4)).reshape(2, 2 * SUBLANES * LANES, S5_GROUP)
    cwide = jnp.tile(rows, (1, 1, S5_GROUPS)) * jnp.asarray(_S5_OUT_MASK, BF16)
    return bst, a_tiles, cwide


def _conv_kernel(prev_ref, cur_ref, next_ref, wdw_ref, bdw_ref, g_ref, b_ref, wpw_ref, bpw_ref,
                 o_ref, win_ref, *, tm):
    i = pl.program_id(1)
    nt = pl.num_programs(1)
    zero = jnp.zeros((CONV_HALO, CONV_WIDTH), F32)
    win_ref[0:CONV_HALO] = jnp.where(i > 0, prev_ref[0], zero)
    win_ref[CONV_HALO:CONV_HALO + tm] = cur_ref[0]
    win_ref[CONV_HALO + tm:] = jnp.where(i < nt - 1, next_ref[0], zero)
    acc = jnp.zeros((tm, CONV_WIDTH), F32) + bdw_ref[0]
    for r in range(SUBLANES):
        z = win_ref[0:tm + SUBLANES, :] * wdw_ref[0, r:r + 1, :]
        for a in range(1, (CONV_K + 1) // SUBLANES):
            j = a * SUBLANES
            z = z + win_ref[j:j + tm + SUBLANES, :] * wdw_ref[0, j + r:j + r + 1, :]
        acc = acc + z[r:r + tm]
    h = _silu(_layer_norm(acc) * g_ref[0] + b_ref[0])
    o_ref[0] = (_dot(h.astype(BF16), wpw_ref[0]) + bpw_ref[0]).astype(BF16)


def _conformer_conv(cg, w_dw, b_dw, ln_g, ln_b, w_pw_bf, b_pw, *, layer, tm):
    bsz, seq, _ = cg.shape
    hb = tm // CONV_HALO
    last = seq // CONV_HALO - 1
    vec = _layer_spec(layer, 1, CONV_WIDTH)
    return pl.pallas_call(
        functools.partial(_conv_kernel, tm=tm),
        grid=(bsz, seq // tm),
        in_specs=[
            pl.BlockSpec((1, CONV_HALO, CONV_WIDTH), lambda b, i: (b, jnp.maximum(i * hb - 1, 0), 0)),
            pl.BlockSpec((1, tm, CONV_WIDTH), lambda b, i: (b, i, 0)),
            pl.BlockSpec((1, CONV_HALO, CONV_WIDTH), lambda b, i: (b, jnp.minimum((i + 1) * hb, last), 0)),
            _layer_spec(layer, CONV_K + 1, CONV_WIDTH),
            vec, vec, vec,
            _layer_spec(layer, CONV_WIDTH, CONV_WIDTH),
            vec,
        ],
        out_specs=pl.BlockSpec((1, tm, CONV_WIDTH), lambda b, i: (b, i, 0)),
        out_shape=jax.ShapeDtypeStruct((bsz, seq, CONV_WIDTH), BF16),
        scratch_shapes=[pltpu.VMEM((tm + 2 * CONV_HALO, CONV_WIDTH), F32)],
        compiler_params=_cparams("arbitrary", "arbitrary"),
        name="conformer_conv",
    )(cg, cg, cg, w_dw, b_dw, ln_g, ln_b, w_pw_bf, b_pw)


def _mixout_kernel(attn_ref, yf_ref, yb_ref, u_ref, conv_ref, x_ref,
                   dskip_ref, wglu_ref, bglu_ref, wout_ref, bout_ref,
                   gate_ref, g1_ref, b1_ref, shift_ref, scale_ref, wr_hi_ref, wr_lo_ref,
                   xmid_ref, h2_ref, logit_ref, *, mod_row):
    gate = _mod_vec(gate_ref, mod_row)
    shift = _mod_vec(shift_ref, mod_row)
    scale1 = 1.0 + _mod_vec(scale_ref, mod_row)
    tm = x_ref.shape[1]
    chunk = min(tm, MIX_CHUNK)
    for r0 in range(0, tm, chunk):
        rows = slice(r0, r0 + chunk)
        y = dskip_ref[0] * u_ref[0, rows, :] + yf_ref[0, rows, :] + yb_ref[0, rows, :]
        z = _gelu_tanh(y)
        s5 = z * _sigmoid(_dot(z.astype(BF16), wglu_ref[0]) + bglu_ref[0])
        y_mix = (_dot(attn_ref[0, rows, :], wout_ref[0, 0:ATTN_WIDTH, :])
                 + _dot(s5.astype(BF16), wout_ref[0, ATTN_WIDTH:ATTN_WIDTH + S5_WIDTH, :])
                 + _dot(conv_ref[0, rows, :], wout_ref[0, ATTN_WIDTH + S5_WIDTH:, :])
                 + bout_ref[0])
        r = DEEPNORM_ALPHA * x_ref[0, rows, :] + gate * y_mix
        x_mid = _layer_norm(r) * g1_ref[0] + b1_ref[0]
        xmid_ref[0, rows, :] = x_mid
        h2 = _layer_norm(x_mid) * scale1 + shift
        h2_ref[0, rows, :] = h2.astype(BF16)
        h_hi, h_lo = _split_bf16(h2)
        logit_ref[0, :, rows] = (_dot_nt(wr_hi_ref[0], h_hi)
                                 + (_dot_nt(wr_hi_ref[0], h_lo) + _dot_nt(wr_lo_ref[0], h_hi)))


def _mixer_output(attn, yf, yb, u, conv, x, d_skip, w_glu_bf, b_glu, w_out_bf, b_out,
                  mods, ln_g, ln_b, wr_hi, wr_lo, *, layer, mod_row, tm):
    bsz, seq, _ = x.shape
    tok = lambda w: pl.BlockSpec((1, tm, w), lambda b, i: (b, i, 0))
    lay = lambda r, c: _layer_spec(layer, r, c)
    return pl.pallas_call(
        functools.partial(_mixout_kernel, mod_row=mod_row),
        grid=(bsz, seq // tm),
        in_specs=[
            tok(ATTN_WIDTH), tok(S5_WIDTH), tok(S5_WIDTH), tok(S5_WIDTH), tok(CONV_WIDTH), tok(D_MODEL),
            lay(1, S5_WIDTH), lay(S5_WIDTH, S5_WIDTH), lay(1, S5_WIDTH),
            lay(D_MODEL, D_MODEL), lay(1, D_MODEL),
            _mod_spec(layer, 2), lay(1, D_MODEL), lay(1, D_MODEL), _mod_spec(layer, 3), _mod_spec(layer, 4),
            lay(LANES, D_MODEL), lay(LANES, D_MODEL),
        ],
        out_specs=[tok(D_MODEL), tok(D_MODEL), pl.BlockSpec((1, LANES, tm), lambda b, i: (b, 0, i))],
        out_shape=[
            jax.ShapeDtypeStruct((bsz, seq, D_MODEL), F32),
            jax.ShapeDtypeStruct((bsz, seq, D_MODEL), BF16),
            jax.ShapeDtypeStruct((bsz, LANES, seq), F32),
        ],
        compiler_params=_cparams("arbitrary", "arbitrary"),
        name="mixer_output",
    )(attn, yf, yb, u, conv, x, d_skip, w_glu_bf, b_glu, w_out_bf, b_out,
      mods, ln_g, ln_b, mods, mods, wr_hi, wr_lo)


def _token_cumsum(m, tri, ones):
    nblk = m.shape[0]
    m2 = m.reshape(nblk * N_EXPERTS, LANES).astype(BF16)
    within = _dot(m2, tri).reshape(nblk, N_EXPERTS, LANES)
    tot = _dot(m2, ones).reshape(nblk, N_EXPERTS, LANES)
    offs = []
    run = jnp.zeros((N_EXPERTS, LANES), F32)
    for j in range(nblk):
        offs.append(run)
        run = run + tot[j]
    off = jnp.stack(offs, axis=0)
    return within + off, off, tot


def _router_kernel(logit_ref, tri_ref, ones_ref, aff_ref, sel_ref, pos_ref, off_ref, cnt_ref, *, cap, slot0):
    b = pl.program_id(0)
    nblk = aff_ref.shape[1]

    def soft(j, carry):
        r0 = pl.multiple_of(j * LANES, LANES)
        t = logit_ref[0, 0:N_EXPERTS, pl.ds(r0, LANES)]
        ex = jnp.exp(t - jnp.max(t, axis=0, keepdims=True))
        aff_ref[0, j] = ex / jnp.sum(ex, axis=0, keepdims=True)
        return carry

    lax.fori_loop(0, nblk, soft, 0, unroll=min(nblk, 4))
    aff = aff_ref[0]

    def enough(cand):
        cnt = jnp.sum(jnp.where(aff >= cand[None], 1.0, 0.0), axis=0)
        return jnp.sum(cnt, axis=-1, keepdims=True) >= cap

    p = jnp.full((N_EXPERTS, LANES), 2.0, F32)
    for k in range(6, -1, -1):
        cand = p * (2.0 ** -(2 ** k))
        p = jnp.where(enough(cand), p, cand)
    thr = 0.5 * p
    thr = jnp.where(enough(thr), thr, 0.0)

    def refine(_, carry):
        lo, step = carry
        cand = lo + step
        return jnp.where(enough(cand), cand, lo), 0.5 * step

    thr, _ = lax.fori_loop(0, 23, refine, (thr, 0.5 * thr))
    gt = aff > thr[None]
    eq = aff == thr[None]
    n_gt = jnp.sum(jnp.sum(jnp.where(gt, 1.0, 0.0), axis=0), axis=-1, keepdims=True)
    need = cap - n_gt
    tri = tri_ref[...]
    ones = ones_ref[...]
    cum_eq, _, _ = _token_cumsum(jnp.where(eq, 1.0, 0.0), tri, ones)
    sel = jnp.where(gt | (eq & (cum_eq <= need[None])), 1.0, 0.0)
    cum_sel, off, cnt = _token_cumsum(sel, tri, ones)
    base = (slot0 + b * cap).astype(F32)
    sel_ref[0] = sel
    pos_ref[0] = cum_sel - sel + base
    off_ref[0] = off + base
    cnt_ref[0] = cnt


def _router(logits, *, slot0):
    bsz, _, n = logits.shape
    nblk = n // LANES
    cap = CAPACITY_FACTOR * n // N_EXPERTS
    idx = np.arange(LANES)
    tri = jnp.asarray((idx[:, None] <= idx[None, :]).astype(np.float32), BF16)
    ones = jnp.ones((LANES, LANES), BF16)
    shape = (bsz, nblk, N_EXPERTS, LANES)
    out = pl.BlockSpec((1, nblk, N_EXPERTS, LANES), lambda b: (b, 0, 0, 0))
    sq = pl.BlockSpec((LANES, LANES), lambda b: (0, 0))
    return pl.pallas_call(
        functools.partial(_router_kernel, cap=cap, slot0=slot0),
        grid=(bsz,),
        in_specs=[pl.BlockSpec((1, LANES, n), lambda b: (b, 0, 0)), sq, sq],
        out_specs=[out] * 5,
        out_shape=[jax.ShapeDtypeStruct(shape, F32)] * 5,
        compiler_params=_cparams("arbitrary"),
        name="router",
    )(logits, tri, ones)


COMBINE_KW = LANES + 16
SMALL_COUNT = 32
COMBINE_SMALL_KW = SMALL_COUNT + 16
FF_TILE = 256


def _dispatch_most(tb_ref, n_entries, e, blocks):
    most = tb_ref[n_entries + blocks[0][0] * N_EXPERTS + e]
    for tile, _ in blocks[1:]:
        most = jnp.maximum(most, tb_ref[n_entries + tile * N_EXPERTS + e])
    return most


def _dispatch_run(tb_ref, e, blocks, xs_buf, g_buf, kw):
    k = lax.broadcasted_iota(jnp.int32, (kw, LANES), 0).astype(F32)
    for tile, load in blocks:
        h2_blk, pos_row, sel_row, aff_row = load()
        base = tb_ref[tile * N_EXPERTS + e]
        base16 = pl.multiple_of(lax.shift_left(lax.shift_right_logical(base, 4), 4), 16)
        hit = (k == (pos_row - base16.astype(F32))) & (sel_row > 0.5)
        sel_t = jnp.where(hit, 1.0, 0.0).astype(BF16)
        xs_buf[pl.ds(base16, kw), :] += _dot(sel_t, h2_blk).astype(BF16)
        g = jnp.sum(jnp.where(hit, aff_row, 0.0), axis=-1, keepdims=True)
        g_buf[pl.ds(base16, kw), :] += jnp.broadcast_to(g, (kw, LANES))


def _moe_kernel(*refs, n_ctx, rows, n_entries):
    tb_ref = refs[0]
    if n_ctx:
        (posx_ref, selx_ref, affx_ref, h2x_ref, posc_ref, selc_ref, affc_ref, h2c_ref,
         wg_ref, wu_ref, wd_ref, y_ref, acc_scr, g_scr, xs_scr) = refs[1:]
    else:
        (posx_ref, selx_ref, affx_ref, h2x_ref, wg_ref, wu_ref, wd_ref, y_ref,
         acc_scr, g_scr, xs_scr) = refs[1:]
    p = pl.program_id(0)
    s = pl.program_id(1)
    last = pl.num_programs(1) - 1
    fill = lax.rem(p, 2)
    use = 1 - fill
    blk_per_step = posx_ref.shape[2]
    e = jnp.minimum(p, N_EXPERTS - 1)
    compacting = p < N_EXPERTS
    computing = p >= 1
    xs_buf = xs_scr.at[fill]
    g_buf = g_scr.at[fill]

    @pl.when(compacting & (s == 0))
    def _():
        xs_buf[...] = jnp.zeros(xs_buf.shape, BF16)
        g_buf[...] = jnp.zeros(g_buf.shape, F32)
        if n_ctx:
            bsz_c, blk_c = posc_ref.shape[0], posc_ref.shape[2]

            def load_c(b, i):
                return lambda: (h2c_ref[b, i * LANES:(i + 1) * LANES, :], posc_ref[b, 0, i:i + 1, :],
                                selc_ref[b, 0, i:i + 1, :], affc_ref[b, 0, i:i + 1, :])

            blocks_c = [((last + 1) * blk_per_step + b * blk_c + i, load_c(b, i))
                        for b in range(bsz_c) for i in range(blk_c)]
            _dispatch_run(tb_ref, e, blocks_c, xs_buf, g_buf, COMBINE_KW)

    def load_x(i):
        return lambda: (h2x_ref[0, i * LANES:(i + 1) * LANES, :], posx_ref[0, 0, i:i + 1, :],
                        selx_ref[0, 0, i:i + 1, :], affx_ref[0, 0, i:i + 1, :])

    blocks = [(s * blk_per_step + i, load_x(i)) for i in range(blk_per_step)]
    short = _dispatch_most(tb_ref, n_entries, e, blocks) <= SMALL_COUNT

    def ffn_tile():
        wg = wg_ref[0, 0].astype(BF16)
        wu = wu_ref[0, 0].astype(BF16)
        wd = wd_ref[0, 0].astype(BF16)
        half = rows // 2
        for r0 in (0, half):
            xs = xs_scr[use, r0:r0 + half, :]
            hid = _silu(_dot(xs, wg)) * _dot(xs, wu)
            part = _dot(hid.astype(BF16), wd)
            acc_scr[r0:r0 + half, :] = jnp.where(s == 0, part, acc_scr[r0:r0 + half, :] + part)

    for window, fits in ((COMBINE_SMALL_KW, short), (COMBINE_KW, jnp.logical_not(short))):
        @pl.when(compacting & computing & fits)
        def _(window=window):
            ffn_tile()
            _dispatch_run(tb_ref, e, blocks, xs_buf, g_buf, window)

        @pl.when(compacting & jnp.logical_not(computing) & fits)
        def _(window=window):
            _dispatch_run(tb_ref, e, blocks, xs_buf, g_buf, window)

    @pl.when(jnp.logical_not(compacting))
    def _():
        ffn_tile()

    @pl.when(computing & (s == last))
    def _():
        g = g_scr[use, 0:rows, :]
        gated = acc_scr[...] * jnp.concatenate([g] * (D_MODEL // LANES), axis=-1)
        y_ref[0, 0:rows, :] = gated.astype(BF16)
        y_ref[0, rows:, :] = jnp.zeros((y_ref.shape[1] - rows, D_MODEL), BF16)


def _moe_experts(tb, route_x, h2x, route_c, h2c, w_gate, w_up, w_down, *, layer):
    bsz, n, d = h2x.shape
    n_ff = EXPERT_FF // FF_TILE
    x_step = bsz * n // n_ff
    steps_per_b = n // x_step
    blk_per_step = x_step // LANES
    n_ctx = 0 if h2c is None else h2c.shape[0] * h2c.shape[1]
    rows = bsz * CAPACITY_FACTOR * n // N_EXPERTS + CAPACITY_FACTOR * n_ctx // N_EXPERTS
    rows_pad = rows + COMBINE_KW
    last_e = N_EXPERTS - 1

    def xs_idx(p, s):
        sc = jnp.where(p > last_e, n_ff - 1, s)
        return sc // steps_per_b, sc % steps_per_b

    def route_x_spec():
        return pl.BlockSpec((1, 1, blk_per_step, LANES),
                            lambda p, s, tb: (xs_idx(p, s)[0], jnp.minimum(p, last_e), xs_idx(p, s)[1], 0))

    in_specs = [route_x_spec(), route_x_spec(), route_x_spec(),
                pl.BlockSpec((1, x_step, d), lambda p, s, tb: (xs_idx(p, s)[0], xs_idx(p, s)[1], 0))]
    args = list(route_x) + [h2x]
    if n_ctx:
        bc, nc, _ = h2c.shape
        rc = pl.BlockSpec((bc, 1, nc // LANES, LANES), lambda p, s, tb: (0, jnp.minimum(p, last_e), 0, 0))
        in_specs += [rc, rc, rc, pl.BlockSpec((bc, nc, d), lambda p, s, tb: (0, 0, 0))]
        args += list(route_c) + [h2c]
    ffn_e = lambda p: jnp.maximum(p - 1, 0)
    in_specs += [pl.BlockSpec((1, 1, d, FF_TILE), lambda p, s, tb: (layer, ffn_e(p), 0, s)),
                 pl.BlockSpec((1, 1, d, FF_TILE), lambda p, s, tb: (layer, ffn_e(p), 0, s)),
                 pl.BlockSpec((1, 1, FF_TILE, d), lambda p, s, tb: (layer, ffn_e(p), s, 0))]
    args += [w_gate, w_up, w_down]
    return pl.pallas_call(
        functools.partial(_moe_kernel, n_ctx=n_ctx, rows=rows, n_entries=tb.shape[0] // 2),
        grid_spec=pltpu.PrefetchScalarGridSpec(
            num_scalar_prefetch=1,
            grid=(N_EXPERTS + 1, n_ff),
            in_specs=in_specs,
            out_specs=pl.BlockSpec((1, rows_pad, d), lambda p, s, tb: (ffn_e(p), 0, 0)),
            scratch_shapes=[pltpu.VMEM((rows, d), F32), pltpu.VMEM((2, rows_pad, LANES), F32),
                            pltpu.VMEM((2, rows_pad, d), BF16)],
        ),
        out_shape=jax.ShapeDtypeStruct((N_EXPERTS, rows_pad, d), BF16),
        compiler_params=_cparams("arbitrary", "arbitrary"),
        name="moe_experts",
    )(tb, *args)


def _combine_kernel(tb_ref, pos_ref, sel_ref, xmid_ref, gate_ref, g_ref, b_ref, y_hbm, o_ref,
                    y_scr, stack_scr, sem, *, cap, slot0, blk_per_step, nblk, n_entries, mod_row):
    b = pl.program_id(0)
    j = pl.program_id(1)
    win = y_scr.shape[1]
    gate = _mod_vec(gate_ref, mod_row)

    @pl.when(j == 0)
    def _():
        row0 = pl.multiple_of(slot0 + b * cap, 16)
        cp = pltpu.make_async_copy(y_hbm.at[:, pl.ds(row0, win), :], y_scr, sem)
        cp.start()
        cp.wait()

    set_base = (slot0 + b * cap).astype(F32)
    tn_dims = (((0,), (0,)), ((), ()))
    for i in range(blk_per_step):
        entry = (b * nblk + j * blk_per_step + i) * N_EXPERTS
        rows = slice(i * LANES, (i + 1) * LANES)

        def hits(e, kw, entry=entry, i=i):
            base = tb_ref[entry + e] - (slot0 + b * cap)
            base16 = pl.multiple_of(lax.shift_left(lax.shift_right_logical(base, 4), 4), 16)
            k = lax.broadcasted_iota(jnp.int32, (kw, LANES), 0).astype(F32)
            rel = pos_ref[0, i, e:e + 1, :] - (set_base + base16.astype(F32))
            hit = (k == rel) & (sel_ref[0, i, e:e + 1, :] > 0.5)
            return base16, jnp.where(hit, 1.0, 0.0).astype(BF16)

        def finish(moe, rows=rows):
            r = DEEPNORM_ALPHA * xmid_ref[0, rows, :] + gate * moe
            o_ref[0, rows, :] = _layer_norm(r) * g_ref[0] + b_ref[0]

        most = tb_ref[n_entries + entry]
        for e in range(1, N_EXPERTS):
            most = jnp.maximum(most, tb_ref[n_entries + entry + e])

        @pl.when(most <= SMALL_COUNT)
        def _():
            sel_all = []
            for e in range(N_EXPERTS):
                base16, sel_t = hits(e, COMBINE_SMALL_KW)
                sel_all.append(sel_t)
                stack_scr[e * COMBINE_SMALL_KW:(e + 1) * COMBINE_SMALL_KW, :] = (
                    y_scr[e, pl.ds(base16, COMBINE_SMALL_KW), :])
            finish(lax.dot_general(jnp.concatenate(sel_all, axis=0), stack_scr[...], tn_dims,
                                   preferred_element_type=F32))

        @pl.when(most > SMALL_COUNT)
        def _():
            acc = jnp.zeros((LANES, D_MODEL), F32)
            for e in range(N_EXPERTS):
                base16, sel_t = hits(e, COMBINE_KW)
                acc = acc + lax.dot_general(sel_t, y_scr[e, pl.ds(base16, COMBINE_KW), :], tn_dims,
                                            preferred_element_type=F32)
            finish(acc)


def _combine_post(tb, pos, sel, x_mid, mods, ln_g, ln_b, y, *, layer, mod_row, slot0, tm):
    bsz, n, d = x_mid.shape
    nblk = n // LANES
    cap = CAPACITY_FACTOR * n // N_EXPERTS
    blk_per_step = tm // LANES
    tok = pl.BlockSpec((1, tm, d), lambda b, j, tb: (b, j, 0))
    route = pl.BlockSpec((1, blk_per_step, N_EXPERTS, LANES), lambda b, j, tb: (b, j, 0, 0))
    vec = _layer_spec(layer, 1, d)
    return pl.pallas_call(
        functools.partial(_combine_kernel, cap=cap, slot0=slot0, blk_per_step=blk_per_step, nblk=nblk,
                          n_entries=bsz * nblk * N_EXPERTS, mod_row=mod_row),
        grid_spec=pltpu.PrefetchScalarGridSpec(
            num_scalar_prefetch=1,
            grid=(bsz, n // tm),
            in_specs=[route, route, tok, _mod_spec(layer, 5), vec, vec,
                      pl.BlockSpec(memory_space=pl.ANY)],
            out_specs=tok,
            scratch_shapes=[pltpu.VMEM((N_EXPERTS, cap + COMBINE_KW, d), BF16),
                            pltpu.VMEM((N_EXPERTS * COMBINE_SMALL_KW, d), BF16),
                            pltpu.SemaphoreType.DMA(())],
        ),
        out_shape=jax.ShapeDtypeStruct(x_mid.shape, F32),
        compiler_params=_cparams("arbitrary", "arbitrary"),
        name="combine_post",
    )(tb, pos, sel, x_mid, mods, ln_g, ln_b, y)


def _tile_table(*lane_replicated):
    return jnp.concatenate([a[..., 0].astype(jnp.int32).reshape(-1) for a in lane_replicated])


def _expert_major(a):
    return jnp.swapaxes(a, 1, 2)


def kernel(x, c, ctx, c_ctx, w_mod, b_mod, w_in, b_in, attn_sink, s5_lam_re, s5_lam_im, s5_log_dt,
           s5_b_re, s5_b_im, s5_c_re, s5_c_im, s5_d, s5_w_glu, s5_b_glu, conv_w_dw, conv_b_dw,
           conv_ln_g, conv_ln_b, conv_w_pw, conv_b_pw, w_out, b_out, ln1_g, ln1_b, w_router,
           exp_w_gate, exp_w_up, exp_w_down, ln2_g, ln2_b):
    bsz, seq, d = x.shape
    lc = ctx.shape[1]
    tm_x, tm_c = min(1024, seq), lc

    cond = jnp.zeros((SUBLANES, d), F32).at[:bsz].set(c).at[bsz].set(c_ctx)
    mods = _modulation(cond, w_mod, b_mod)
    cos_t, sin_t = _rope_tables(seq)
    s5_mask = _s5_mask()
    zero_state = jnp.zeros((2 * bsz, SUBLANES, 2 * LANES), F32)

    vec3 = lambda a: a.reshape(DEPTH, 1, -1)
    w_in_bf, w_out_bf = w_in.astype(BF16), w_out.astype(BF16)
    w_glu_bf, w_pw_bf = s5_w_glu.astype(BF16), conv_w_pw.astype(BF16)
    wr = jnp.pad(jnp.swapaxes(w_router, 1, 2), ((0, 0), (0, LANES - N_EXPERTS), (0, 0)))
    wr_hi, wr_lo = _split_bf16(wr)
    sink_rep = jnp.broadcast_to(attn_sink[:, :, None], (DEPTH, N_Q_HEADS, LANES))
    bst, a_tiles, cwide = jax.vmap(_s5_params)(s5_lam_re, s5_lam_im, s5_log_dt, s5_b_re, s5_b_im,
                                               s5_c_re, s5_c_im)
    w_dw = jnp.pad(conv_w_dw.reshape(DEPTH, CONV_K, CONV_WIDTH), ((0, 0), (1, 0), (0, 0)))
    conv_args = (w_dw, vec3(conv_b_dw), vec3(conv_ln_g), vec3(conv_ln_b), w_pw_bf, vec3(conv_b_pw))
    mix_args = (vec3(s5_d), w_glu_bf, vec3(s5_b_glu), w_out_bf, vec3(b_out), mods, vec3(ln1_g), vec3(ln1_b),
                wr_hi, wr_lo)
    b_in3, ln2_g3, ln2_b3 = vec3(b_in), vec3(ln2_g), vec3(ln2_b)

    xc = ctx
    ctx_row = bsz
    for l in range(DEPTH):
        last = l == DEPTH - 1
        q, k, v, u, cg = _in_projection(x, mods, w_in_bf, b_in3, cos_t, sin_t,
                                        layer=l, mod_row=None, rope=True, tm=tm_x)
        q_c, k_c, v_c, u_c, cg_c = _in_projection(xc, mods, w_in_bf, b_in3, cos_t, sin_t,
                                                  layer=l, mod_row=ctx_row, rope=False, tm=tm_c)

        attn_x = _attention(q, k, v, k_c, v_c, sink_rep, layer=l, window=True)
        yf_c, yb_c, h_ctx = _s5_scan(u_c, zero_state, s5_mask, bst, a_tiles, cwide, layer=l, tc=lc)
        yf, yb, _ = _s5_scan(u, h_ctx, s5_mask, bst, a_tiles, cwide, layer=l, tc=256)
        conv_x = _conformer_conv(cg, *conv_args, layer=l, tm=256)
        x_mid, h2, logits = _mixer_output(attn_x, yf, yb, u, conv_x, x, *mix_args,
                                          layer=l, mod_row=None, tm=tm_x)
        aff_x, sel_x, pos_x, off_x, cnt_x = _router(logits, slot0=0)
        route_x = tuple(_expert_major(a) for a in (pos_x, sel_x, aff_x))
        tb_x = _tile_table(off_x, cnt_x)
        if not last:
            attn_c = _attention(q_c, None, None, k_c, v_c, sink_rep, layer=l, window=False)
            conv_c = _conformer_conv(cg_c, *conv_args, layer=l, tm=lc)
            xc_mid, hc2, logits_c = _mixer_output(attn_c, yf_c, yb_c, u_c, conv_c, xc, *mix_args,
                                                  layer=l, mod_row=ctx_row, tm=tm_c)
            slot0_c = bsz * CAPACITY_FACTOR * seq // N_EXPERTS
            aff_c, sel_c, pos_c, off_c, cnt_c = _router(logits_c, slot0=slot0_c)
            route_c = tuple(_expert_major(a) for a in (pos_c, sel_c, aff_c))
            tb_c = _tile_table(off_c, cnt_c)
            y = _moe_experts(_tile_table(off_x, off_c, cnt_x, cnt_c), route_x, h2, route_c, hc2,
                             exp_w_gate, exp_w_up, exp_w_down, layer=l)
            xc = _combine_post(tb_c, pos_c, sel_c, xc_mid, mods, ln2_g3, ln2_b3, y,
                               layer=l, mod_row=ctx_row, slot0=slot0_c, tm=lc)
        else:
            y = _moe_experts(tb_x, route_x, h2, None, None, exp_w_gate, exp_w_up, exp_w_down, layer=l)
        x = _combine_post(tb_x, pos_x, sel_x, x_mid, mods, ln2_g3, ln2_b3, y,
                          layer=l, mod_row=None, slot0=0, tm=min(512, seq))
    return x
```

```python
import functools
import math

import jax
import jax.numpy as jnp
import numpy as np
from jax import lax
from jax.experimental import pallas as pl
from jax.experimental.pallas import tpu as pltpu

D_MODEL = 1024
DEPTH = 2
GRID_W = 64
HEAD_DIM = 64
ATTN_WIDTH = D_MODEL // 2
N_Q_HEADS = ATTN_WIDTH // HEAD_DIM
N_KV_HEADS = N_Q_HEADS // 4
Q_PER_KV = N_Q_HEADS // N_KV_HEADS
KV_WIDTH = N_KV_HEADS * HEAD_DIM
BLOCK = 128
ROPE_BASE = 10000.0
NEG_INF = -1e30
S5_WIDTH = D_MODEL // 4
S5_GROUP = 16
S5_GROUPS = S5_WIDTH // S5_GROUP
S5_STATE = 64
CONV_WIDTH = D_MODEL - ATTN_WIDTH - S5_WIDTH
CONV_K = 31
CONV_HALO = 16
Q_END = ATTN_WIDTH
K_END = Q_END + KV_WIDTH
V_END = K_END + KV_WIDTH
U_END = V_END + S5_WIDTH
IN_WIDTH = U_END + 2 * CONV_WIDTH
N_EXPERTS = 16
EXPERT_FF = 2 * D_MODEL
CAPACITY_FACTOR = 2
DEEPNORM_ALPHA = (2.0 * DEPTH) ** 0.25
LN_EPS = 1e-5

LANES = 128
SUBLANES = 8
S5_CHUNKS = S5_WIDTH * S5_STATE // S5_GROUP // LANES
CH_PER_CHUNK = S5_WIDTH // S5_CHUNKS
VMEM_LIMIT = 56 * 1024 * 1024
MIX_CHUNK = 256

F32 = jnp.float32
BF16 = jnp.bfloat16


def _cparams(*sem):
    return pltpu.CompilerParams(dimension_semantics=sem, vmem_limit_bytes=VMEM_LIMIT)


def _dot(a, b):
    return jnp.dot(a, b, preferred_element_type=F32)


def _dot_nt(a, b):
    return lax.dot_general(a, b, (((1,), (1,)), ((), ())), preferred_element_type=F32)


def _split_bf16(x):
    hi = x.astype(BF16)
    lo = (x - hi.astype(F32)).astype(BF16)
    return hi, lo


def _dot3(a, b_hi, b_lo):
    a_hi, a_lo = _split_bf16(a)
    return _dot(a_hi, b_hi) + (_dot(a_lo, b_hi) + _dot(a_hi, b_lo))


def _sigmoid(x):
    return 1.0 / (1.0 + jnp.exp(-x))


def _silu(x):
    return x * _sigmoid(x)


def _gelu_tanh(x):
    c = math.sqrt(2.0 / math.pi)
    return 0.5 * x * (1.0 + jnp.tanh(c * (x + 0.044715 * (x * x * x))))


def _layer_norm(x):
    mu = jnp.mean(x, axis=-1, keepdims=True)
    xc = x - mu
    var = jnp.mean(xc * xc, axis=-1, keepdims=True)
    return xc * lax.rsqrt(var + LN_EPS)


def _mod_kernel(c_ref, w_ref, b_ref, o_ref):
    s = _silu(c_ref[...])
    w = w_ref[0]
    w_hi, w_lo = _split_bf16(w)
    o_ref[0] = _dot3(s, w_hi, w_lo) + b_ref[0]


def _modulation(cond, w_mod, b_mod):
    tn = 1536
    n = w_mod.shape[-1]
    return pl.pallas_call(
        _mod_kernel,
        grid=(DEPTH, n // tn),
        in_specs=[
            pl.BlockSpec((SUBLANES, D_MODEL), lambda l, j: (0, 0)),
            pl.BlockSpec((1, D_MODEL, tn), lambda l, j: (l, 0, j)),
            pl.BlockSpec((1, 1, tn), lambda l, j: (l, 0, j)),
        ],
        out_specs=pl.BlockSpec((1, SUBLANES, tn), lambda l, j: (l, 0, j)),
        out_shape=jax.ShapeDtypeStruct((DEPTH, SUBLANES, n), F32),
        compiler_params=_cparams("arbitrary", "arbitrary"),
        name="modulation",
    )(cond, w_mod, b_mod.reshape(DEPTH, 1, n))


def _rope_chunk(x, cos, sin_signed):
    lane = lax.broadcasted_iota(jnp.int32, x.shape, 1)
    first = (lane % 32) < 16
    partner = jnp.where(first, pltpu.roll(x, LANES - 16, 1), pltpu.roll(x, 16, 1))
    return x * cos + partner * sin_signed


def _mod_spec(layer, k):
    return pl.BlockSpec((1, SUBLANES, D_MODEL), lambda *_: (layer, 0, k))


def _layer_spec(layer, *shape):
    return pl.BlockSpec((1,) + shape, lambda *_: (layer,) + (0,) * len(shape))


def _mod_vec(ref, mod_row):
    row = pl.program_id(0) if mod_row is None else mod_row
    return ref[0, pl.ds(row, 1), :]


def _inproj_kernel(x_ref, shift_ref, scale_ref, w_ref, b_ref, cos_ref, sin_ref,
                   q_ref, k_ref, v_ref, u_ref, cg_ref, *, rope, mod_row):
    shift = _mod_vec(shift_ref, mod_row)
    scale1 = 1.0 + _mod_vec(scale_ref, mod_row)
    tm = x_ref.shape[1]
    chunk = min(tm, MIX_CHUNK)
    scale = HEAD_DIM ** -0.5
    for r0 in range(0, tm, chunk):
        rows = slice(r0, r0 + chunk)
        h = _layer_norm(x_ref[0, rows, :]) * scale1 + shift
        p = _dot(h.astype(BF16), w_ref[0]) + b_ref[0]
        if rope:
            cos = cos_ref[rows, :]
            sin = sin_ref[rows, :]
        for j in range(ATTN_WIDTH // LANES):
            qc = p[:, j * LANES:(j + 1) * LANES]
            if rope:
                qc = _rope_chunk(qc, cos, sin)
            q_ref[0, rows, j * LANES:(j + 1) * LANES] = (qc * scale).astype(BF16)
        kc = p[:, Q_END:K_END]
        if rope:
            kc = _rope_chunk(kc, cos, sin)
        k_ref[0, rows, :] = kc.astype(BF16)
        v_ref[0, rows, :] = p[:, K_END:V_END].astype(BF16)
        u_ref[0, rows, :] = p[:, V_END:U_END]
        a = p[:, U_END:U_END + CONV_WIDTH]
        g = p[:, U_END + CONV_WIDTH:]
        cg_ref[0, rows, :] = a * _sigmoid(g)


def _in_projection(x, mods, w_in_bf, b_in, cos_t, sin_t, *, layer, mod_row, rope, tm):
    bsz, seq, _ = x.shape
    tok = lambda w: pl.BlockSpec((1, tm, w), lambda b, i: (b, i, 0))
    tab = pl.BlockSpec((tm, LANES), lambda b, i: (i, 0))
    return pl.pallas_call(
        functools.partial(_inproj_kernel, rope=rope, mod_row=mod_row),
        grid=(bsz, seq // tm),
        in_specs=[
            tok(D_MODEL), _mod_spec(layer, 0), _mod_spec(layer, 1),
            _layer_spec(layer, D_MODEL, IN_WIDTH), _layer_spec(layer, 1, IN_WIDTH),
            tab, tab,
        ],
        out_specs=[tok(ATTN_WIDTH), tok(KV_WIDTH), tok(KV_WIDTH), tok(S5_WIDTH), tok(CONV_WIDTH)],
        out_shape=[
            jax.ShapeDtypeStruct((bsz, seq, ATTN_WIDTH), BF16),
            jax.ShapeDtypeStruct((bsz, seq, KV_WIDTH), BF16),
            jax.ShapeDtypeStruct((bsz, seq, KV_WIDTH), BF16),
            jax.ShapeDtypeStruct((bsz, seq, S5_WIDTH), F32),
            jax.ShapeDtypeStruct((bsz, seq, CONV_WIDTH), F32),
        ],
        compiler_params=_cparams("arbitrary", "arbitrary"),
        name="in_projection",
    )(x, mods, mods, w_in_bf, b_in.reshape(DEPTH, 1, IN_WIDTH), cos_t, sin_t)


def _rope_tables(seq):
    f = HEAD_DIM // 4
    inv_freq = jnp.asarray(ROPE_BASE, F32) ** (-jnp.arange(f, dtype=F32) / f)

    def axis_tables(n_pos):
        ang = jnp.arange(n_pos, dtype=jnp.int32).astype(F32)[:, None] * inv_freq[None, :]
        cos, sin = jnp.cos(ang), jnp.sin(ang)
        return jnp.concatenate([cos, cos], axis=-1), jnp.concatenate([-sin, sin], axis=-1)

    n_rows = seq // GRID_W
    cos_r, sin_r = axis_tables(n_rows)
    cos_c, sin_c = axis_tables(GRID_W)
    by_row = lambda a: jnp.repeat(a, GRID_W, axis=0)
    by_col = lambda a: jnp.tile(a, (n_rows, 1))
    cos = jnp.concatenate([by_row(cos_r), by_col(cos_c)], axis=-1)
    sin = jnp.concatenate([by_row(sin_r), by_col(sin_c)], axis=-1)
    return jnp.tile(cos, (1, LANES // HEAD_DIM)), jnp.tile(sin, (1, LANES // HEAD_DIM))


ATTN_SUBBLOCKS = 8


def _attn_kernel(*refs, window):
    if window:
        n_kv = ATTN_SUBBLOCKS + 2
        q_ref = refs[0]
        k_refs = refs[1:1 + n_kv]
        v_refs = refs[1 + n_kv:1 + 2 * n_kv]
        kx_ref, vx_ref, sink_ref, o_ref = refs[1 + 2 * n_kv:]
        subs = ATTN_SUBBLOCKS
    else:
        q_ref, kx_ref, vx_ref, sink_ref, o_ref = refs
        subs = 1
    step = pl.program_id(1)
    n_steps = pl.num_programs(1)
    tq = q_ref.shape[1] // subs
    rows = Q_PER_KV * tq
    if window:
        row = lax.broadcasted_iota(jnp.int32, (rows, BLOCK), 0) % tq
        col = lax.broadcasted_iota(jnp.int32, (rows, BLOCK), 1)

    def with_ones(v, ks):
        return jnp.concatenate([v[:, ks], jnp.ones((v.shape[0], HEAD_DIM), BF16)], axis=-1)

    for sub in range(subs):
        q = q_ref[0, sub * tq:(sub + 1) * tq, :]
        if window:
            kp_ref, kc_ref, kn_ref = k_refs[sub:sub + 3]
            vp_ref, vc_ref, vn_ref = v_refs[sub:sub + 3]
            ok_prev = (col >= row) if sub > 0 else (col >= row) & (step > 0)
            ok_next = (col <= row) if sub < subs - 1 else (col <= row) & (step < n_steps - 1)
        outs = []
        for g in range(N_KV_HEADS):
            ks = slice(g * HEAD_DIM, (g + 1) * HEAD_DIM)
            heads = range(g * Q_PER_KV, (g + 1) * Q_PER_KV)
            qs = jnp.concatenate([q[:, h * HEAD_DIM:(h + 1) * HEAD_DIM] for h in heads], axis=0)
            sink = jnp.concatenate([jnp.broadcast_to(sink_ref[0, h:h + 1, 0:1], (tq, 1)) for h in heads], axis=0)
            lx = kx_ref.shape[1]
            s_x = _dot_nt(qs, kx_ref[0][:, ks])
            m_el = s_x[:, 0:LANES]
            for c in range(1, lx // LANES):
                m_el = jnp.maximum(m_el, s_x[:, c * LANES:(c + 1) * LANES])
            if window:
                s_p = jnp.where(ok_prev, _dot_nt(qs, kp_ref[0][:, ks]), NEG_INF)
                s_c = _dot_nt(qs, kc_ref[0][:, ks])
                s_n = jnp.where(ok_next, _dot_nt(qs, kn_ref[0][:, ks]), NEG_INF)
                m_el = jnp.maximum(jnp.maximum(m_el, s_c), jnp.maximum(s_p, s_n))
            m = jnp.maximum(jnp.max(m_el, axis=-1, keepdims=True), sink)
            acc = _dot(jnp.exp(s_x - m).astype(BF16), with_ones(vx_ref[0], ks))
            if window:
                for s_w, v_ref in ((s_p, vp_ref), (s_c, vc_ref), (s_n, vn_ref)):
                    acc = acc + _dot(jnp.exp(s_w - m).astype(BF16), with_ones(v_ref[0], ks))
            o = acc[:, 0:HEAD_DIM] / (acc[:, HEAD_DIM:] + jnp.exp(sink - m))
            outs += [o[i * tq:(i + 1) * tq] for i in range(Q_PER_KV)]
        o_ref[0, sub * tq:(sub + 1) * tq, :] = jnp.concatenate(outs, axis=-1).astype(BF16)


def _attention(q, k, v, k_ctx, v_ctx, sink_rep, *, layer, window):
    bsz, seq, _ = q.shape
    lc = k_ctx.shape[1]
    ctx_spec = pl.BlockSpec((1, lc, KV_WIDTH), lambda b, i: (b, 0, 0))
    sink_spec = _layer_spec(layer, N_Q_HEADS, LANES)
    if window:
        tq = BLOCK * ATTN_SUBBLOCKS
        nb = seq // BLOCK

        def kv_spec(j):
            return pl.BlockSpec((1, BLOCK, KV_WIDTH),
                                lambda b, i: (b, jnp.clip(ATTN_SUBBLOCKS * i - 1 + j, 0, nb - 1), 0))

        kv_specs = [kv_spec(j) for j in range(ATTN_SUBBLOCKS + 2)]
        in_specs = ([pl.BlockSpec((1, tq, ATTN_WIDTH), lambda b, i: (b, i, 0))] + kv_specs + kv_specs
                    + [ctx_spec, ctx_spec, sink_spec])
        args = (q,) + (k,) * len(kv_specs) + (v,) * len(kv_specs) + (k_ctx, v_ctx, sink_rep)
    else:
        tq = seq
        in_specs = [pl.BlockSpec((1, tq, ATTN_WIDTH), lambda b, i: (b, i, 0)), ctx_spec, ctx_spec, sink_spec]
        args = (q, k_ctx, v_ctx, sink_rep)
    return pl.pallas_call(
        functools.partial(_attn_kernel, window=window),
        grid=(bsz, seq // tq),
        in_specs=in_specs,
        out_specs=pl.BlockSpec((1, tq, ATTN_WIDTH), lambda b, i: (b, i, 0)),
        out_shape=jax.ShapeDtypeStruct((bsz, seq, ATTN_WIDTH), BF16),
        compiler_params=_cparams("arbitrary", "arbitrary"),
        name="window_attention" if window else "context_attention",
    )(*args)


def _s5_kernel(uf_ref, ub_ref, h0_ref, mask_ref, bst_ref, a_ref, cw_ref,
               yf_ref, yb_ref, hfin_ref, lhs_scr, bu_scr, hs_scr, h_scr, *, bsz, tc):
    i = pl.program_id(0)

    @pl.when(i == 0)
    def _():
        h_scr[...] = h0_ref[...]

    mask = mask_ref[...]
    n_chain = 2 * bsz

    for d, u_ref in enumerate((uf_ref, ub_ref)):
        for b in range(bsz):
            for j in range(tc // 2):
                pair = [jnp.broadcast_to(u_ref[b, 2 * j + k:2 * j + k + 1, :], (SUBLANES, S5_WIDTH)) * mask
                        for k in range(2)]
                lhs_scr[d * bsz + b, 2 * SUBLANES * j:2 * SUBLANES * (j + 1), :] = (
                    jnp.concatenate(pair, axis=0).astype(BF16))
    for c in range(n_chain):
        bu_scr[c] = _dot(lhs_scr[c], bst_ref[0, c // bsz])

    a_re = [a_ref[0, d, 0] for d in range(2)]
    a_im = [a_ref[0, d, 1] for d in range(2)]

    def step(t, carry):
        new = []
        for c in range(n_chain):
            d = c // bsz
            tt = t if d == 0 else tc - 1 - t
            r0 = pl.multiple_of(tt * SUBLANES, SUBLANES)
            h_re, h_im = carry[2 * c], carry[2 * c + 1]
            n_re = a_re[d] * h_re - a_im[d] * h_im + bu_scr[c, pl.ds(r0, SUBLANES), 0:LANES]
            n_im = a_re[d] * h_im + a_im[d] * h_re + bu_scr[c, pl.ds(r0, SUBLANES), LANES:2 * LANES]
            hs_scr[c, 0, pl.ds(r0, SUBLANES), :] = n_re
            hs_scr[c, 1, pl.ds(r0, SUBLANES), :] = n_im
            new += [n_re, n_im]
        return tuple(new)

    init = []
    for c in range(n_chain):
        init += [h_scr[c, :, 0:LANES], h_scr[c, :, LANES:2 * LANES]]
    fin = lax.fori_loop(0, tc, step, tuple(init), unroll=8)
    for c in range(n_chain):
        h_scr[c, :, 0:LANES] = fin[2 * c]
        h_scr[c, :, LANES:2 * LANES] = fin[2 * c + 1]
    hfin_ref[...] = h_scr[...]

    for d, y_ref in enumerate((yf_ref, yb_ref)):
        for b in range(bsz):
            c = d * bsz + b
            parts = [hs_scr[c, ri, pl.ds(s, tc, stride=SUBLANES), :].astype(BF16)
                     for s in range(SUBLANES) for ri in range(2)]
            y_ref[b] = _dot(jnp.concatenate(parts, axis=-1), cw_ref[0, d])


def _s5_scan(u, h0, mask, bst, a_tiles, cwide, *, layer, tc):
    bsz, seq, _ = u.shape
    nch = seq // tc
    full = lambda shape: pl.BlockSpec(shape, lambda i: (0,) * len(shape))
    fwd = pl.BlockSpec((bsz, tc, S5_WIDTH), lambda i: (0, i, 0))
    bwd = pl.BlockSpec((bsz, tc, S5_WIDTH), lambda i: (0, nch - 1 - i, 0))
    state = (2 * bsz, SUBLANES, 2 * LANES)
    rows = SUBLANES * tc
    return pl.pallas_call(
        functools.partial(_s5_kernel, bsz=bsz, tc=tc),
        grid=(nch,),
        in_specs=[fwd, bwd, full(state), full(mask.shape), _layer_spec(layer, *bst.shape[1:]),
                  _layer_spec(layer, *a_tiles.shape[1:]), _layer_spec(layer, *cwide.shape[1:])],
        out_specs=[fwd, bwd, full(state)],
        out_shape=[jax.ShapeDtypeStruct(u.shape, F32), jax.ShapeDtypeStruct(u.shape, F32),
                   jax.ShapeDtypeStruct(state, F32)],
        scratch_shapes=[pltpu.VMEM((2 * bsz, rows, S5_WIDTH), BF16),
                        pltpu.VMEM((2 * bsz, rows, 2 * LANES), F32),
                        pltpu.VMEM((2 * bsz, 2, rows, LANES), F32),
                        pltpu.VMEM(state, F32)],
        compiler_params=_cparams("arbitrary"),
        name="s5_scan",
    )(u, u, h0, mask, bst, a_tiles, cwide)


def _s5_mask():
    m = np.arange(S5_WIDTH)[None, :] // CH_PER_CHUNK == np.arange(SUBLANES)[:, None]
    return jnp.asarray(m.astype(np.float32), F32)


def _s5_out_mask():
    row_group = (np.arange(SUBLANES)[:, None, None, None] * (S5_GROUPS // SUBLANES)
                 + np.arange(S5_GROUPS // SUBLANES)[None, None, :, None]
                 + np.zeros((1, 2, 1, S5_STATE), np.int64)).reshape(-1)
    col_group = np.arange(S5_WIDTH) // S5_GROUP
    return (row_group[:, None] == col_group[None, :]).astype(np.float32)


_S5_OUT_MASK = _s5_out_mask()


def _s5_params(lam_re, lam_im, log_dt, b_re, b_im, c_re, c_im):
    dt = jnp.exp(log_dt)[..., None]
    mag = jnp.exp(lam_re * dt)
    l_re = mag * jnp.cos(lam_im * dt)
    l_im = mag * jnp.sin(lam_im * dt)
    den = lam_re * lam_re + lam_im * lam_im
    f_re = ((l_re - 1.0) * lam_re + l_im * lam_im) / den
    f_im = (l_im * lam_re - (l_re - 1.0) * lam_im) / den
    bb_re = f_re[..., None] * b_re - f_im[..., None] * b_im
    bb_im = f_re[..., None] * b_im + f_im[..., None] * b_re
    a_tiles = jnp.stack([l_re.reshape(2, SUBLANES, LANES), l_im.reshape(2, SUBLANES, LANES)], axis=1)

    half = S5_GROUPS // SUBLANES
    eye = jnp.eye(half, dtype=F32)

    def in_mat(bb):
        t = bb.reshape(2, SUBLANES, half, S5_STATE, S5_GROUP)
        m = jnp.einsum('dsgpc,gh->dsgchp', t, eye)
        return m.reshape(2, S5_WIDTH, half * S5_STATE)

    bst = jnp.concatenate([in_mat(bb_re), in_mat(bb_im)], axis=-1).astype(BF16)
    t = jnp.stack([c_re, -c_im], axis=1).astype(BF16)
    t = t.reshape(2, 2, SUBLANES, half, S5_GROUP, S5_STATE)
    rows = jnp.transpose(t, (0, 2, 1, 3, 5, 4)).reshape(2, 2 * SUBLANES * LANES, S5_GROUP)
    cwide = jnp.tile(rows, (1, 1, S5_GROUPS)) * jnp.asarray(_S5_OUT_MASK, BF16)
    return bst, a_tiles, cwide


def _conv_kernel(prev_ref, cur_ref, next_ref, wdw_ref, bdw_ref, g_ref, b_ref, wpw_ref, bpw_ref,
                 o_ref, win_ref, *, tm):
    i = pl.program_id(1)
    nt = pl.num_programs(1)
    zero = jnp.zeros((CONV_HALO, CONV_WIDTH), F32)
    win_ref[0:CONV_HALO] = jnp.where(i > 0, prev_ref[0], zero)
    win_ref[CONV_HALO:CONV_HALO + tm] = cur_ref[0]
    win_ref[CONV_HALO + tm:] = jnp.where(i < nt - 1, next_ref[0], zero)
    acc = jnp.zeros((tm, CONV_WIDTH), F32) + bdw_ref[0]
    for r in range(SUBLANES):
        z = win_ref[0:tm + SUBLANES, :] * wdw_ref[0, r:r + 1, :]
        for a in range(1, (CONV_K + 1) // SUBLANES):
            j = a * SUBLANES
            z = z + win_ref[j:j + tm + SUBLANES, :] * wdw_ref[0, j + r:j + r + 1, :]
        acc = acc + z[r:r + tm]
    h = _silu(_layer_norm(acc) * g_ref[0] + b_ref[0])
    o_ref[0] = (_dot(h.astype(BF16), wpw_ref[0]) + bpw_ref[0]).astype(BF16)


def _conformer_conv(cg, w_dw, b_dw, ln_g, ln_b, w_pw_bf, b_pw, *, layer, tm):
    bsz, seq, _ = cg.shape
    hb = tm // CONV_HALO
    last = seq // CONV_HALO - 1
    vec = _layer_spec(layer, 1, CONV_WIDTH)
    return pl.pallas_call(
        functools.partial(_conv_kernel, tm=tm),
        grid=(bsz, seq // tm),
        in_specs=[
            pl.BlockSpec((1, CONV_HALO, CONV_WIDTH), lambda b, i: (b, jnp.maximum(i * hb - 1, 0), 0)),
            pl.BlockSpec((1, tm, CONV_WIDTH), lambda b, i: (b, i, 0)),
            pl.BlockSpec((1, CONV_HALO, CONV_WIDTH), lambda b, i: (b, jnp.minimum((i + 1) * hb, last), 0)),
            _layer_spec(layer, CONV_K + 1, CONV_WIDTH),
            vec, vec, vec,
            _layer_spec(layer, CONV_WIDTH, CONV_WIDTH),
            vec,
        ],
        out_specs=pl.BlockSpec((1, tm, CONV_WIDTH), lambda b, i: (b, i, 0)),
        out_shape=jax.ShapeDtypeStruct((bsz, seq, CONV_WIDTH), BF16),
        scratch_shapes=[pltpu.VMEM((tm + 2 * CONV_HALO, CONV_WIDTH), F32)],
        compiler_params=_cparams("arbitrary", "arbitrary"),
        name="conformer_conv",
    )(cg, cg, cg, w_dw, b_dw, ln_g, ln_b, w_pw_bf, b_pw)


def _mixout_kernel(attn_ref, yf_ref, yb_ref, u_ref, conv_ref, x_ref,
                   dskip_ref, wglu_ref, bglu_ref, wout_ref, bout_ref,
                   gate_ref, g1_ref, b1_ref, shift_ref, scale_ref, wr_hi_ref, wr_lo_ref,
                   xmid_ref, h2_ref, logit_ref, *, mod_row):
    gate = _mod_vec(gate_ref, mod_row)
    shift = _mod_vec(shift_ref, mod_row)
    scale1 = 1.0 + _mod_vec(scale_ref, mod_row)
    tm = x_ref.shape[1]
    chunk = min(tm, MIX_CHUNK)
    for r0 in range(0, tm, chunk):
        rows = slice(r0, r0 + chunk)
        y = dskip_ref[0] * u_ref[0, rows, :] + yf_ref[0, rows, :] + yb_ref[0, rows, :]
        z = _gelu_tanh(y)
        s5 = z * _sigmoid(_dot(z.astype(BF16), wglu_ref[0]) + bglu_ref[0])
        y_mix = (_dot(attn_ref[0, rows, :], wout_ref[0, 0:ATTN_WIDTH, :])
                 + _dot(s5.astype(BF16), wout_ref[0, ATTN_WIDTH:ATTN_WIDTH + S5_WIDTH, :])
                 + _dot(conv_ref[0, rows, :], wout_ref[0, ATTN_WIDTH + S5_WIDTH:, :])
                 + bout_ref[0])
        r = DEEPNORM_ALPHA * x_ref[0, rows, :] + gate * y_mix
        x_mid = _layer_norm(r) * g1_ref[0] + b1_ref[0]
        xmid_ref[0, rows, :] = x_mid
        h2 = _layer_norm(x_mid) * scale1 + shift
        h2_ref[0, rows, :] = h2.astype(BF16)
        h_hi, h_lo = _split_bf16(h2)
        logit_ref[0, :, rows] = (_dot_nt(wr_hi_ref[0], h_hi)
                                 + (_dot_nt(wr_hi_ref[0], h_lo) + _dot_nt(wr_lo_ref[0], h_hi)))


def _mixer_output(attn, yf, yb, u, conv, x, d_skip, w_glu_bf, b_glu, w_out_bf, b_out,
                  mods, ln_g, ln_b, wr_hi, wr_lo, *, layer, mod_row, tm):
    bsz, seq, _ = x.shape
    tok = lambda w: pl.BlockSpec((1, tm, w), lambda b, i: (b, i, 0))
    lay = lambda r, c: _layer_spec(layer, r, c)
    return pl.pallas_call(
        functools.partial(_mixout_kernel, mod_row=mod_row),
        grid=(bsz, seq // tm),
        in_specs=[
            tok(ATTN_WIDTH), tok(S5_WIDTH), tok(S5_WIDTH), tok(S5_WIDTH), tok(CONV_WIDTH), tok(D_MODEL),
            lay(1, S5_WIDTH), lay(S5_WIDTH, S5_WIDTH), lay(1, S5_WIDTH),
            lay(D_MODEL, D_MODEL), lay(1, D_MODEL),
            _mod_spec(layer, 2), lay(1, D_MODEL), lay(1, D_MODEL), _mod_spec(layer, 3), _mod_spec(layer, 4),
            lay(N_EXPERTS, D_MODEL), lay(N_EXPERTS, D_MODEL),
        ],
        out_specs=[tok(D_MODEL), tok(D_MODEL), pl.BlockSpec((1, N_EXPERTS, tm), lambda b, i: (b, 0, i))],
        out_shape=[
            jax.ShapeDtypeStruct((bsz, seq, D_MODEL), F32),
            jax.ShapeDtypeStruct((bsz, seq, D_MODEL), BF16),
            jax.ShapeDtypeStruct((bsz, N_EXPERTS, seq), F32),
        ],
        compiler_params=_cparams("arbitrary", "arbitrary"),
        name="mixer_output",
    )(attn, yf, yb, u, conv, x, d_skip, w_glu_bf, b_glu, w_out_bf, b_out,
      mods, ln_g, ln_b, mods, mods, wr_hi, wr_lo)


def _token_cumsum(m, tri, ones):
    nblk = m.shape[0]
    m2 = m.reshape(nblk * N_EXPERTS, LANES).astype(BF16)
    within = _dot(m2, tri).reshape(nblk, N_EXPERTS, LANES)
    tot = _dot(m2, ones).reshape(nblk, N_EXPERTS, LANES)
    offs = []
    run = jnp.zeros((N_EXPERTS, LANES), F32)
    for j in range(nblk):
        offs.append(run)
        run = run + tot[j]
    off = jnp.stack(offs, axis=0)
    return within + off, off, tot


def _router_kernel(logit_ref, tri_ref, ones_ref, aff_ref, sel_ref, pos_ref, off_ref, cnt_ref, *, cap, slot0):
    b = pl.program_id(0)
    nblk = aff_ref.shape[1]

    def soft(j, carry):
        r0 = pl.multiple_of(j * LANES, LANES)
        t = logit_ref[0, 0:N_EXPERTS, pl.ds(r0, LANES)]
        ex = jnp.exp(t - jnp.max(t, axis=0, keepdims=True))
        aff_ref[0, j] = ex / jnp.sum(ex, axis=0, keepdims=True)
        return carry

    lax.fori_loop(0, nblk, soft, 0, unroll=min(nblk, 4))
    aff = aff_ref[0]

    def enough(cand):
        cnt = jnp.sum(jnp.where(aff >= cand[None], 1.0, 0.0), axis=0)
        return jnp.sum(cnt, axis=-1, keepdims=True) >= cap

    p = jnp.full((N_EXPERTS, LANES), 2.0, F32)
    for k in range(6, -1, -1):
        cand = p * (2.0 ** -(2 ** k))
        p = jnp.where(enough(cand), p, cand)
    thr = 0.5 * p
    thr = jnp.where(enough(thr), thr, 0.0)

    def refine(_, carry):
        lo, step = carry
        cand = lo + step
        return jnp.where(enough(cand), cand, lo), 0.5 * step

    thr, _ = lax.fori_loop(0, 23, refine, (thr, 0.5 * thr))
    gt = aff > thr[None]
    eq = aff == thr[None]
    n_gt = jnp.sum(jnp.sum(jnp.where(gt, 1.0, 0.0), axis=0), axis=-1, keepdims=True)
    need = cap - n_gt
    tri = tri_ref[...]
    ones = ones_ref[...]
    cum_eq, _, _ = _token_cumsum(jnp.where(eq, 1.0, 0.0), tri, ones)
    sel = jnp.where(gt | (eq & (cum_eq <= need[None])), 1.0, 0.0)
    cum_sel, off, cnt = _token_cumsum(sel, tri, ones)
    base = (slot0 + b * cap).astype(F32)
    sel_ref[0] = sel
    pos_ref[0] = cum_sel - sel + base
    off_ref[0] = off + base
    cnt_ref[0] = cnt


def _router(logits, *, slot0):
    bsz, _, n = logits.shape
    nblk = n // LANES
    cap = CAPACITY_FACTOR * n // N_EXPERTS
    idx = np.arange(LANES)
    tri = jnp.asarray((idx[:, None] <= idx[None, :]).astype(np.float32), BF16)
    ones = jnp.ones((LANES, LANES), BF16)
    shape = (bsz, nblk, N_EXPERTS, LANES)
    out = pl.BlockSpec((1, nblk, N_EXPERTS, LANES), lambda b: (b, 0, 0, 0))
    sq = pl.BlockSpec((LANES, LANES), lambda b: (0, 0))
    return pl.pallas_call(
        functools.partial(_router_kernel, cap=cap, slot0=slot0),
        grid=(bsz,),
        in_specs=[pl.BlockSpec((1, N_EXPERTS, n), lambda b: (b, 0, 0)), sq, sq],
        out_specs=[out] * 5,
        out_shape=[jax.ShapeDtypeStruct(shape, F32)] * 5,
        compiler_params=_cparams("arbitrary"),
        name="router",
    )(logits, tri, ones)


COMBINE_KW = LANES + 16
SMALL_COUNT = 32
COMBINE_SMALL_KW = SMALL_COUNT + 16
FF_TILE = 256


def _dispatch_most(tb_ref, n_entries, e, blocks):
    most = tb_ref[n_entries + blocks[0][0] * N_EXPERTS + e]
    for tile, _ in blocks[1:]:
        most = jnp.maximum(most, tb_ref[n_entries + tile * N_EXPERTS + e])
    return most


def _dispatch_run(tb_ref, e, blocks, xs_buf, g_buf, kw):
    k = lax.broadcasted_iota(jnp.int32, (kw, LANES), 0).astype(F32)
    for tile, load in blocks:
        h2_blk, pos_row, sel_row, aff_row = load()
        base = tb_ref[tile * N_EXPERTS + e]
        base16 = pl.multiple_of(lax.shift_left(lax.shift_right_logical(base, 4), 4), 16)
        hit = (k == (pos_row - base16.astype(F32))) & (sel_row > 0.5)
        sel_t = jnp.where(hit, 1.0, 0.0).astype(BF16)
        xs_buf[pl.ds(base16, kw), :] += _dot(sel_t, h2_blk).astype(BF16)
        g = jnp.sum(jnp.where(hit, aff_row, 0.0), axis=-1, keepdims=True)
        g_buf[pl.ds(base16, kw), :] += jnp.broadcast_to(g, (kw, LANES))


def _moe_kernel(*refs, n_ctx, rows, n_entries):
    tb_ref = refs[0]
    if n_ctx:
        (posx_ref, selx_ref, affx_ref, h2x_ref, posc_ref, selc_ref, affc_ref, h2c_ref,
         wg_ref, wu_ref, wd_ref, y_ref, acc_scr, g_scr, xs_scr) = refs[1:]
    else:
        (posx_ref, selx_ref, affx_ref, h2x_ref, wg_ref, wu_ref, wd_ref, y_ref,
         acc_scr, g_scr, xs_scr) = refs[1:]
    p = pl.program_id(0)
    s = pl.program_id(1)
    last = pl.num_programs(1) - 1
    fill = lax.rem(p, 2)
    use = 1 - fill
    blk_per_step = posx_ref.shape[2]
    e = jnp.minimum(p, N_EXPERTS - 1)
    compacting = p < N_EXPERTS
    computing = p >= 1
    xs_buf = xs_scr.at[fill]
    g_buf = g_scr.at[fill]

    @pl.when(compacting & (s == 0))
    def _():
        xs_buf[...] = jnp.zeros(xs_buf.shape, BF16)
        g_buf[...] = jnp.zeros(g_buf.shape, F32)
        if n_ctx:
            bsz_c, blk_c = posc_ref.shape[0], posc_ref.shape[2]

            def load_c(b, i):
                return lambda: (h2c_ref[b, i * LANES:(i + 1) * LANES, :], posc_ref[b, 0, i:i + 1, :],
                                selc_ref[b, 0, i:i + 1, :], affc_ref[b, 0, i:i + 1, :])

            blocks_c = [((last + 1) * blk_per_step + b * blk_c + i, load_c(b, i))
                        for b in range(bsz_c) for i in range(blk_c)]
            _dispatch_run(tb_ref, e, blocks_c, xs_buf, g_buf, COMBINE_KW)

    def load_x(i):
        return lambda: (h2x_ref[0, i * LANES:(i + 1) * LANES, :], posx_ref[0, 0, i:i + 1, :],
                        selx_ref[0, 0, i:i + 1, :], affx_ref[0, 0, i:i + 1, :])

    blocks = [(s * blk_per_step + i, load_x(i)) for i in range(blk_per_step)]
    short = _dispatch_most(tb_ref, n_entries, e, blocks) <= SMALL_COUNT

    def ffn_tile():
        wg = wg_ref[0, 0].astype(BF16)
        wu = wu_ref[0, 0].astype(BF16)
        wd = wd_ref[0, 0].astype(BF16)
        half = rows // 2
        for r0 in (0, half):
            xs = xs_scr[use, r0:r0 + half, :]
            hid = _silu(_dot(xs, wg)) * _dot(xs, wu)
            part = _dot(hid.astype(BF16), wd)
            acc_scr[r0:r0 + half, :] = jnp.where(s == 0, part, acc_scr[r0:r0 + half, :] + part)

    for window, fits in ((COMBINE_SMALL_KW, short), (COMBINE_KW, jnp.logical_not(short))):
        @pl.when(compacting & computing & fits)
        def _(window=window):
            ffn_tile()
            _dispatch_run(tb_ref, e, blocks, xs_buf, g_buf, window)

        @pl.when(compacting & jnp.logical_not(computing) & fits)
        def _(window=window):
            _dispatch_run(tb_ref, e, blocks, xs_buf, g_buf, window)

    @pl.when(jnp.logical_not(compacting))
    def _():
        ffn_tile()

    @pl.when(computing & (s == last))
    def _():
        g = g_scr[use, 0:rows, :]
        gated = acc_scr[...] * jnp.concatenate([g] * (D_MODEL // LANES), axis=-1)
        y_ref[0, 0:rows, :] = gated.astype(BF16)
        y_ref[0, rows:, :] = jnp.zeros((y_ref.shape[1] - rows, D_MODEL), BF16)


def _moe_experts(tb, route_x, h2x, route_c, h2c, w_gate, w_up, w_down, *, layer):
    bsz, n, d = h2x.shape
    n_ff = EXPERT_FF // FF_TILE
    x_step = bsz * n // n_ff
    steps_per_b = n // x_step
    blk_per_step = x_step // LANES
    n_ctx = 0 if h2c is None else h2c.shape[0] * h2c.shape[1]
    rows = bsz * CAPACITY_FACTOR * n // N_EXPERTS + CAPACITY_FACTOR * n_ctx // N_EXPERTS
    rows_pad = rows + COMBINE_KW
    last_e = N_EXPERTS - 1

    def xs_idx(p, s):
        sc = jnp.where(p > last_e, n_ff - 1, s)
        return sc // steps_per_b, sc % steps_per_b

    def route_x_spec():
        return pl.BlockSpec((1, 1, blk_per_step, LANES),
                            lambda p, s, tb: (xs_idx(p, s)[0], jnp.minimum(p, last_e), xs_idx(p, s)[1], 0))

    in_specs = [route_x_spec(), route_x_spec(), route_x_spec(),
                pl.BlockSpec((1, x_step, d), lambda p, s, tb: (xs_idx(p, s)[0], xs_idx(p, s)[1], 0))]
    args = list(route_x) + [h2x]
    if n_ctx:
        bc, nc, _ = h2c.shape
        rc = pl.BlockSpec((bc, 1, nc // LANES, LANES), lambda p, s, tb: (0, jnp.minimum(p, last_e), 0, 0))
        in_specs += [rc, rc, rc, pl.BlockSpec((bc, nc, d), lambda p, s, tb: (0, 0, 0))]
        args += list(route_c) + [h2c]
    ffn_e = lambda p: jnp.maximum(p - 1, 0)
    in_specs += [pl.BlockSpec((1, 1, d, FF_TILE), lambda p, s, tb: (layer, ffn_e(p), 0, s)),
                 pl.BlockSpec((1, 1, d, FF_TILE), lambda p, s, tb: (layer, ffn_e(p), 0, s)),
                 pl.BlockSpec((1, 1, FF_TILE, d), lambda p, s, tb: (layer, ffn_e(p), s, 0))]
    args += [w_gate, w_up, w_down]
    return pl.pallas_call(
        functools.partial(_moe_kernel, n_ctx=n_ctx, rows=rows, n_entries=tb.shape[0] // 2),
        grid_spec=pltpu.PrefetchScalarGridSpec(
            num_scalar_prefetch=1,
            grid=(N_EXPERTS + 1, n_ff),
            in_specs=in_specs,
            out_specs=pl.BlockSpec((1, rows_pad, d), lambda p, s, tb: (ffn_e(p), 0, 0)),
            scratch_shapes=[pltpu.VMEM((rows, d), F32), pltpu.VMEM((2, rows_pad, LANES), F32),
                            pltpu.VMEM((2, rows_pad, d), BF16)],
        ),
        out_shape=jax.ShapeDtypeStruct((N_EXPERTS, rows_pad, d), BF16),
        compiler_params=_cparams("arbitrary", "arbitrary"),
        name="moe_experts",
    )(tb, *args)


def _combine_kernel(tb_ref, pos_ref, sel_ref, xmid_ref, gate_ref, g_ref, b_ref, y_hbm, o_ref,
                    y_scr, stack_scr, sem, *, cap, slot0, blk_per_step, nblk, n_entries, mod_row):
    b = pl.program_id(0)
    j = pl.program_id(1)
    win = y_scr.shape[1]
    gate = _mod_vec(gate_ref, mod_row)

    @pl.when(j == 0)
    def _():
        row0 = pl.multiple_of(slot0 + b * cap, 16)
        cp = pltpu.make_async_copy(y_hbm.at[:, pl.ds(row0, win), :], y_scr, sem)
        cp.start()
        cp.wait()

    set_base = (slot0 + b * cap).astype(F32)
    tn_dims = (((0,), (0,)), ((), ()))
    for i in range(blk_per_step):
        entry = (b * nblk + j * blk_per_step + i) * N_EXPERTS
        rows = slice(i * LANES, (i + 1) * LANES)

        def hits(e, kw, entry=entry, i=i):
            base = tb_ref[entry + e] - (slot0 + b * cap)
            base16 = pl.multiple_of(lax.shift_left(lax.shift_right_logical(base, 4), 4), 16)
            k = lax.broadcasted_iota(jnp.int32, (kw, LANES), 0).astype(F32)
            rel = pos_ref[0, i, e:e + 1, :] - (set_base + base16.astype(F32))
            hit = (k == rel) & (sel_ref[0, i, e:e + 1, :] > 0.5)
            return base16, jnp.where(hit, 1.0, 0.0).astype(BF16)

        def finish(moe, rows=rows):
            r = DEEPNORM_ALPHA * xmid_ref[0, rows, :] + gate * moe
            o_ref[0, rows, :] = _layer_norm(r) * g_ref[0] + b_ref[0]

        most = tb_ref[n_entries + entry]
        for e in range(1, N_EXPERTS):
            most = jnp.maximum(most, tb_ref[n_entries + entry + e])

        @pl.when(most <= SMALL_COUNT)
        def _():
            sel_all = []
            for e in range(N_EXPERTS):
                base16, sel_t = hits(e, COMBINE_SMALL_KW)
                sel_all.append(sel_t)
                stack_scr[e * COMBINE_SMALL_KW:(e + 1) * COMBINE_SMALL_KW, :] = (
                    y_scr[e, pl.ds(base16, COMBINE_SMALL_KW), :])
            finish(lax.dot_general(jnp.concatenate(sel_all, axis=0), stack_scr[...], tn_dims,
                                   preferred_element_type=F32))

        @pl.when(most > SMALL_COUNT)
        def _():
            acc = jnp.zeros((LANES, D_MODEL), F32)
            for e in range(N_EXPERTS):
                base16, sel_t = hits(e, COMBINE_KW)
                acc = acc + lax.dot_general(sel_t, y_scr[e, pl.ds(base16, COMBINE_KW), :], tn_dims,
                                            preferred_element_type=F32)
            finish(acc)


def _combine_post(tb, pos, sel, x_mid, mods, ln_g, ln_b, y, *, layer, mod_row, slot0, tm):
    bsz, n, d = x_mid.shape
    nblk = n // LANES
    cap = CAPACITY_FACTOR * n // N_EXPERTS
    blk_per_step = tm // LANES
    tok = pl.BlockSpec((1, tm, d), lambda b, j, tb: (b, j, 0))
    route = pl.BlockSpec((1, blk_per_step, N_EXPERTS, LANES), lambda b, j, tb: (b, j, 0, 0))
    vec = _layer_spec(layer, 1, d)
    return pl.pallas_call(
        functools.partial(_combine_kernel, cap=cap, slot0=slot0, blk_per_step=blk_per_step, nblk=nblk,
                          n_entries=bsz * nblk * N_EXPERTS, mod_row=mod_row),
        grid_spec=pltpu.PrefetchScalarGridSpec(
            num_scalar_prefetch=1,
            grid=(bsz, n // tm),
            in_specs=[route, route, tok, _mod_spec(layer, 5), vec, vec,
                      pl.BlockSpec(memory_space=pl.ANY)],
            out_specs=tok,
            scratch_shapes=[pltpu.VMEM((N_EXPERTS, cap + COMBINE_KW, d), BF16),
                            pltpu.VMEM((N_EXPERTS * COMBINE_SMALL_KW, d), BF16),
                            pltpu.SemaphoreType.DMA(())],
        ),
        out_shape=jax.ShapeDtypeStruct(x_mid.shape, F32),
        compiler_params=_cparams("arbitrary", "arbitrary"),
        name="combine_post",
    )(tb, pos, sel, x_mid, mods, ln_g, ln_b, y)


def _tile_table(*lane_replicated):
    return jnp.concatenate([a[..., 0].astype(jnp.int32).reshape(-1) for a in lane_replicated])


def _expert_major(a):
    return jnp.swapaxes(a, 1, 2)


def kernel(x, c, ctx, c_ctx, w_mod, b_mod, w_in, b_in, attn_sink, s5_lam_re, s5_lam_im, s5_log_dt,
           s5_b_re, s5_b_im, s5_c_re, s5_c_im, s5_d, s5_w_glu, s5_b_glu, conv_w_dw, conv_b_dw,
           conv_ln_g, conv_ln_b, conv_w_pw, conv_b_pw, w_out, b_out, ln1_g, ln1_b, w_router,
           exp_w_gate, exp_w_up, exp_w_down, ln2_g, ln2_b):
    bsz, seq, d = x.shape
    lc = ctx.shape[1]
    tm_x, tm_c = min(1024, seq), lc

    cond = jnp.zeros((SUBLANES, d), F32).at[:bsz].set(c).at[bsz].set(c_ctx)
    mods = _modulation(cond, w_mod, b_mod)
    cos_t, sin_t = _rope_tables(seq)
    s5_mask = _s5_mask()
    zero_state = jnp.zeros((2 * bsz, SUBLANES, 2 * LANES), F32)

    vec3 = lambda a: a.reshape(DEPTH, 1, -1)
    w_in_bf, w_out_bf = w_in.astype(BF16), w_out.astype(BF16)
    w_glu_bf, w_pw_bf = s5_w_glu.astype(BF16), conv_w_pw.astype(BF16)
    wr = jnp.swapaxes(w_router, 1, 2)
    wr_hi, wr_lo = _split_bf16(wr)
    sink_rep = jnp.broadcast_to(attn_sink[:, :, None], (DEPTH, N_Q_HEADS, LANES))
    bst, a_tiles, cwide = jax.vmap(_s5_params)(s5_lam_re, s5_lam_im, s5_log_dt, s5_b_re, s5_b_im,
                                               s5_c_re, s5_c_im)
    w_dw = jnp.pad(conv_w_dw.reshape(DEPTH, CONV_K, CONV_WIDTH), ((0, 0), (1, 0), (0, 0)))
    conv_args = (w_dw, vec3(conv_b_dw), vec3(conv_ln_g), vec3(conv_ln_b), w_pw_bf, vec3(conv_b_pw))
    mix_args = (vec3(s5_d), w_glu_bf, vec3(s5_b_glu), w_out_bf, vec3(b_out), mods, vec3(ln1_g), vec3(ln1_b),
                wr_hi, wr_lo)
    b_in3, ln2_g3, ln2_b3 = vec3(b_in), vec3(ln2_g), vec3(ln2_b)

    xc = ctx
    ctx_row = bsz
    for l in range(DEPTH):
        last = l == DEPTH - 1
        q, k, v, u, cg = _in_projection(x, mods, w_in_bf, b_in3, cos_t, sin_t,
                                        layer=l, mod_row=None, rope=True, tm=tm_x)
        q_c, k_c, v_c, u_c, cg_c = _in_projection(xc, mods, w_in_bf, b_in3, cos_t, sin_t,
                                                  layer=l, mod_row=ctx_row, rope=False, tm=tm_c)

        attn_x = _attention(q, k, v, k_c, v_c, sink_rep, layer=l, window=True)
        yf_c, yb_c, h_ctx = _s5_scan(u_c, zero_state, s5_mask, bst, a_tiles, cwide, layer=l, tc=lc)
        yf, yb, _ = _s5_scan(u, h_ctx, s5_mask, bst, a_tiles, cwide, layer=l, tc=256)
        conv_x = _conformer_conv(cg, *conv_args, layer=l, tm=256)
        x_mid, h2, logits = _mixer_output(attn_x, yf, yb, u, conv_x, x, *mix_args,
                                          layer=l, mod_row=None, tm=tm_x)
        aff_x, sel_x, pos_x, off_x, cnt_x = _router(logits, slot0=0)
        route_x = tuple(_expert_major(a) for a in (pos_x, sel_x, aff_x))
        tb_x = _tile_table(off_x, cnt_x)
        if not last:
            attn_c = _attention(q_c, None, None, k_c, v_c, sink_rep, layer=l, window=False)
            conv_c = _conformer_conv(cg_c, *conv_args, layer=l, tm=lc)
            xc_mid, hc2, logits_c = _mixer_output(attn_c, yf_c, yb_c, u_c, conv_c, xc, *mix_args,
                                                  layer=l, mod_row=ctx_row, tm=tm_c)
            slot0_c = bsz * CAPACITY_FACTOR * seq // N_EXPERTS
            aff_c, sel_c, pos_c, off_c, cnt_c = _router(logits_c, slot0=slot0_c)
            route_c = tuple(_expert_major(a) for a in (pos_c, sel_c, aff_c))
            tb_c = _tile_table(off_c, cnt_c)
            y = _moe_experts(_tile_table(off_x, off_c, cnt_x, cnt_c), route_x, h2, route_c, hc2,
                             exp_w_gate, exp_w_up, exp_w_down, layer=l)
            xc = _combine_post(tb_c, pos_c, sel_c, xc_mid, mods, ln2_g3, ln2_b3, y,
                               layer=l, mod_row=ctx_row, slot0=slot0_c, tm=lc)
        else:
            y = _moe_experts(tb_x, route_x, h2, None, None, exp_w_gate, exp_w_up, exp_w_down, layer=l)
        x = _combine_post(tb_x, pos_x, sel_x, x_mid, mods, ln2_g3, ln2_b3, y,
                          layer=l, mod_row=None, slot0=0, tm=min(512, seq))
    return x
```

```python
import functools
import math

import jax
import jax.numpy as jnp
import numpy as np
from jax import lax
from jax.experimental import pallas as pl
from jax.experimental.pallas import tpu as pltpu

D_MODEL = 1024
DEPTH = 2
GRID_W = 64
HEAD_DIM = 64
ATTN_WIDTH = D_MODEL // 2
N_Q_HEADS = ATTN_WIDTH // HEAD_DIM
N_KV_HEADS = N_Q_HEADS // 4
Q_PER_KV = N_Q_HEADS // N_KV_HEADS
KV_WIDTH = N_KV_HEADS * HEAD_DIM
BLOCK = 128
ROPE_BASE = 10000.0
NEG_INF = -1e30
S5_WIDTH = D_MODEL // 4
S5_GROUP = 16
S5_GROUPS = S5_WIDTH // S5_GROUP
S5_STATE = 64
CONV_WIDTH = D_MODEL - ATTN_WIDTH - S5_WIDTH
CONV_K = 31
CONV_HALO = 16
Q_END = ATTN_WIDTH
K_END = Q_END + KV_WIDTH
V_END = K_END + KV_WIDTH
U_END = V_END + S5_WIDTH
IN_WIDTH = U_END + 2 * CONV_WIDTH
N_EXPERTS = 16
EXPERT_FF = 2 * D_MODEL
CAPACITY_FACTOR = 2
DEEPNORM_ALPHA = (2.0 * DEPTH) ** 0.25
LN_EPS = 1e-5

LANES = 128
SUBLANES = 8
S5_CHUNKS = S5_WIDTH * S5_STATE // S5_GROUP // LANES
CH_PER_CHUNK = S5_WIDTH // S5_CHUNKS
VMEM_LIMIT = 56 * 1024 * 1024
MIX_CHUNK = 256

F32 = jnp.float32
BF16 = jnp.bfloat16


def _cparams(*sem):
    return pltpu.CompilerParams(dimension_semantics=sem, vmem_limit_bytes=VMEM_LIMIT)


def _dot(a, b):
    return jnp.dot(a, b, preferred_element_type=F32)


def _dot_nt(a, b):
    return lax.dot_general(a, b, (((1,), (1,)), ((), ())), preferred_element_type=F32)


def _split_bf16(x):
    hi = x.astype(BF16)
    lo = (x - hi.astype(F32)).astype(BF16)
    return hi, lo


def _dot3(a, b_hi, b_lo):
    a_hi, a_lo = _split_bf16(a)
    return _dot(a_hi, b_hi) + (_dot(a_lo, b_hi) + _dot(a_hi, b_lo))


def _sigmoid(x):
    return 1.0 / (1.0 + jnp.exp(-x))


def _silu(x):
    return x * _sigmoid(x)


def _gelu_tanh(x):
    c = math.sqrt(2.0 / math.pi)
    return 0.5 * x * (1.0 + jnp.tanh(c * (x + 0.044715 * (x * x * x))))


def _layer_norm(x):
    mu = jnp.mean(x, axis=-1, keepdims=True)
    xc = x - mu
    var = jnp.mean(xc * xc, axis=-1, keepdims=True)
    return xc * lax.rsqrt(var + LN_EPS)


def _mod_kernel(c_ref, w_ref, b_ref, o_ref):
    s = _silu(c_ref[...])
    w = w_ref[0]
    w_hi, w_lo = _split_bf16(w)
    o_ref[0] = _dot3(s, w_hi, w_lo) + b_ref[0]


def _modulation(cond, w_mod, b_mod):
    tn = 1536
    n = w_mod.shape[-1]
    return pl.pallas_call(
        _mod_kernel,
        grid=(DEPTH, n // tn),
        in_specs=[
            pl.BlockSpec((SUBLANES, D_MODEL), lambda l, j: (0, 0)),
            pl.BlockSpec((1, D_MODEL, tn), lambda l, j: (l, 0, j)),
            pl.BlockSpec((1, 1, tn), lambda l, j: (l, 0, j)),
        ],
        out_specs=pl.BlockSpec((1, SUBLANES, tn), lambda l, j: (l, 0, j)),
        out_shape=jax.ShapeDtypeStruct((DEPTH, SUBLANES, n), F32),
        compiler_params=_cparams("arbitrary", "arbitrary"),
        name="modulation",
    )(cond, w_mod, b_mod.reshape(DEPTH, 1, n))


def _rope_chunk(x, cos, sin_signed):
    lane = lax.broadcasted_iota(jnp.int32, x.shape, 1)
    first = (lane % 32) < 16
    partner = jnp.where(first, pltpu.roll(x, LANES - 16, 1), pltpu.roll(x, 16, 1))
    return x * cos + partner * sin_signed


def _mod_spec(layer, k):
    return pl.BlockSpec((1, SUBLANES, D_MODEL), lambda *_: (layer, 0, k))


def _layer_spec(layer, *shape):
    return pl.BlockSpec((1,) + shape, lambda *_: (layer,) + (0,) * len(shape))


def _mod_vec(ref, mod_row):
    row = pl.program_id(0) if mod_row is None else mod_row
    return ref[0, pl.ds(row, 1), :]


def _inproj_kernel(x_ref, shift_ref, scale_ref, w_ref, b_ref, cos_ref, sin_ref,
                   q_ref, k_ref, v_ref, u_ref, cg_ref, *, rope, mod_row):
    shift = _mod_vec(shift_ref, mod_row)
    scale1 = 1.0 + _mod_vec(scale_ref, mod_row)
    tm = x_ref.shape[1]
    chunk = min(tm, MIX_CHUNK)
    scale = HEAD_DIM ** -0.5
    for r0 in range(0, tm, chunk):
        rows = slice(r0, r0 + chunk)
        h = _layer_norm(x_ref[0, rows, :]) * scale1 + shift
        p = _dot(h.astype(BF16), w_ref[0]) + b_ref[0]
        if rope:
            cos = cos_ref[rows, :]
            sin = sin_ref[rows, :]
        for j in range(ATTN_WIDTH // LANES):
            qc = p[:, j * LANES:(j + 1) * LANES]
            if rope:
                qc = _rope_chunk(qc, cos, sin)
            q_ref[0, rows, j * LANES:(j + 1) * LANES] = (qc * scale).astype(BF16)
        kc = p[:, Q_END:K_END]
        if rope:
            kc = _rope_chunk(kc, cos, sin)
        k_ref[0, rows, :] = kc.astype(BF16)
        v_ref[0, rows, :] = p[:, K_END:V_END].astype(BF16)
        u_ref[0, rows, :] = p[:, V_END:U_END]
        a = p[:, U_END:U_END + CONV_WIDTH]
        g = p[:, U_END + CONV_WIDTH:]
        cg_ref[0, rows, :] = a * _sigmoid(g)


def _in_projection(x, mods, w_in_bf, b_in, cos_t, sin_t, *, layer, mod_row, rope, tm):
    bsz, seq, _ = x.shape
    tok = lambda w: pl.BlockSpec((1, tm, w), lambda b, i: (b, i, 0))
    tab = pl.BlockSpec((tm, LANES), lambda b, i: (i, 0))
    return pl.pallas_call(
        functools.partial(_inproj_kernel, rope=rope, mod_row=mod_row),
        grid=(bsz, seq // tm),
        in_specs=[
            tok(D_MODEL), _mod_spec(layer, 0), _mod_spec(layer, 1),
            _layer_spec(layer, D_MODEL, IN_WIDTH), _layer_spec(layer, 1, IN_WIDTH),
            tab, tab,
        ],
        out_specs=[tok(ATTN_WIDTH), tok(KV_WIDTH), tok(KV_WIDTH), tok(S5_WIDTH), tok(CONV_WIDTH)],
        out_shape=[
            jax.ShapeDtypeStruct((bsz, seq, ATTN_WIDTH), BF16),
            jax.ShapeDtypeStruct((bsz, seq, KV_WIDTH), BF16),
            jax.ShapeDtypeStruct((bsz, seq, KV_WIDTH), BF16),
            jax.ShapeDtypeStruct((bsz, seq, S5_WIDTH), F32),
            jax.ShapeDtypeStruct((bsz, seq, CONV_WIDTH), F32),
        ],
        compiler_params=_cparams("arbitrary", "arbitrary"),
        name="in_projection",
    )(x, mods, mods, w_in_bf, b_in.reshape(DEPTH, 1, IN_WIDTH), cos_t, sin_t)


def _rope_tables(seq):
    f = HEAD_DIM // 4
    inv_freq = jnp.asarray(ROPE_BASE, F32) ** (-jnp.arange(f, dtype=F32) / f)

    def axis_tables(n_pos):
        ang = jnp.arange(n_pos, dtype=jnp.int32).astype(F32)[:, None] * inv_freq[None, :]
        cos, sin = jnp.cos(ang), jnp.sin(ang)
        return jnp.concatenate([cos, cos], axis=-1), jnp.concatenate([-sin, sin], axis=-1)

    n_rows = seq // GRID_W
    cos_r, sin_r = axis_tables(n_rows)
    cos_c, sin_c = axis_tables(GRID_W)
    by_row = lambda a: jnp.repeat(a, GRID_W, axis=0)
    by_col = lambda a: jnp.tile(a, (n_rows, 1))
    cos = jnp.concatenate([by_row(cos_r), by_col(cos_c)], axis=-1)
    sin = jnp.concatenate([by_row(sin_r), by_col(sin_c)], axis=-1)
    return jnp.tile(cos, (1, LANES // HEAD_DIM)), jnp.tile(sin, (1, LANES // HEAD_DIM))


ATTN_SUBBLOCKS = 8


def _attn_kernel(*refs, window):
    if window:
        n_kv = ATTN_SUBBLOCKS + 2
        q_ref = refs[0]
        k_refs = refs[1:1 + n_kv]
        v_refs = refs[1 + n_kv:1 + 2 * n_kv]
        kx_ref, vx_ref, sink_ref, o_ref = refs[1 + 2 * n_kv:]
        subs = ATTN_SUBBLOCKS
    else:
        q_ref, kx_ref, vx_ref, sink_ref, o_ref = refs
        subs = 1
    step = pl.program_id(1)
    n_steps = pl.num_programs(1)
    tq = q_ref.shape[1] // subs
    rows = Q_PER_KV * tq
    if window:
        row = lax.broadcasted_iota(jnp.int32, (rows, BLOCK), 0) % tq
        col = lax.broadcasted_iota(jnp.int32, (rows, BLOCK), 1)

    def with_ones(v, ks):
        return jnp.concatenate([v[:, ks], jnp.ones((v.shape[0], HEAD_DIM), BF16)], axis=-1)

    for sub in range(subs):
        q = q_ref[0, sub * tq:(sub + 1) * tq, :]
        if window:
            kp_ref, kc_ref, kn_ref = k_refs[sub:sub + 3]
            vp_ref, vc_ref, vn_ref = v_refs[sub:sub + 3]
            ok_prev = (col >= row) if sub > 0 else (col >= row) & (step > 0)
            ok_next = (col <= row) if sub < subs - 1 else (col <= row) & (step < n_steps - 1)
        outs = []
        for g in range(N_KV_HEADS):
            ks = slice(g * HEAD_DIM, (g + 1) * HEAD_DIM)
            heads = range(g * Q_PER_KV, (g + 1) * Q_PER_KV)
            qs = jnp.concatenate([q[:, h * HEAD_DIM:(h + 1) * HEAD_DIM] for h in heads], axis=0)
            sink = jnp.concatenate([jnp.broadcast_to(sink_ref[0, h:h + 1, 0:1], (tq, 1)) for h in heads], axis=0)
            lx = kx_ref.shape[1]
            s_x = _dot_nt(qs, kx_ref[0][:, ks])
            m_el = s_x[:, 0:LANES]
            for c in range(1, lx // LANES):
                m_el = jnp.maximum(m_el, s_x[:, c * LANES:(c + 1) * LANES])
            if window:
                s_p = jnp.where(ok_prev, _dot_nt(qs, kp_ref[0][:, ks]), NEG_INF)
                s_c = _dot_nt(qs, kc_ref[0][:, ks])
                s_n = jnp.where(ok_next, _dot_nt(qs, kn_ref[0][:, ks]), NEG_INF)
                m_el = jnp.maximum(jnp.maximum(m_el, s_c), jnp.maximum(s_p, s_n))
            m = jnp.maximum(jnp.max(m_el, axis=-1, keepdims=True), sink)
            acc = _dot(jnp.exp(s_x - m).astype(BF16), with_ones(vx_ref[0], ks))
            if window:
                for s_w, v_ref in ((s_p, vp_ref), (s_c, vc_ref), (s_n, vn_ref)):
                    acc = acc + _dot(jnp.exp(s_w - m).astype(BF16), with_ones(v_ref[0], ks))
            o = acc[:, 0:HEAD_DIM] / (acc[:, HEAD_DIM:] + jnp.exp(sink - m))
            outs += [o[i * tq:(i + 1) * tq] for i in range(Q_PER_KV)]
        o_ref[0, sub * tq:(sub + 1) * tq, :] = jnp.concatenate(outs, axis=-1).astype(BF16)


def _attention(q, k, v, k_ctx, v_ctx, sink_rep, *, layer, window):
    bsz, seq, _ = q.shape
    lc = k_ctx.shape[1]
    ctx_spec = pl.BlockSpec((1, lc, KV_WIDTH), lambda b, i: (b, 0, 0))
    sink_spec = _layer_spec(layer, N_Q_HEADS, LANES)
    if window:
        tq = BLOCK * ATTN_SUBBLOCKS
        nb = seq // BLOCK

        def kv_spec(j):
            return pl.BlockSpec((1, BLOCK, KV_WIDTH),
                                lambda b, i: (b, jnp.clip(ATTN_SUBBLOCKS * i - 1 + j, 0, nb - 1), 0))

        kv_specs = [kv_spec(j) for j in range(ATTN_SUBBLOCKS + 2)]
        in_specs = ([pl.BlockSpec((1, tq, ATTN_WIDTH), lambda b, i: (b, i, 0))] + kv_specs + kv_specs
                    + [ctx_spec, ctx_spec, sink_spec])
        args = (q,) + (k,) * len(kv_specs) + (v,) * len(kv_specs) + (k_ctx, v_ctx, sink_rep)
    else:
        tq = seq
        in_specs = [pl.BlockSpec((1, tq, ATTN_WIDTH), lambda b, i: (b, i, 0)), ctx_spec, ctx_spec, sink_spec]
        args = (q, k_ctx, v_ctx, sink_rep)
    return pl.pallas_call(
        functools.partial(_attn_kernel, window=window),
        grid=(bsz, seq // tq),
        in_specs=in_specs,
        out_specs=pl.BlockSpec((1, tq, ATTN_WIDTH), lambda b, i: (b, i, 0)),
        out_shape=jax.ShapeDtypeStruct((bsz, seq, ATTN_WIDTH), BF16),
        compiler_params=_cparams("arbitrary", "arbitrary"),
        name="window_attention" if window else "context_attention",
    )(*args)


def _s5_kernel(uf_ref, ub_ref, h0_ref, mask_ref, bst_ref, a_ref, cw_ref,
               yf_ref, yb_ref, hfin_ref, lhs_scr, bu_scr, hs_scr, h_scr, *, bsz, tc):
    i = pl.program_id(0)

    @pl.when(i == 0)
    def _():
        h_scr[...] = h0_ref[...]

    mask = mask_ref[...]
    n_chain = 2 * bsz

    for d, u_ref in enumerate((uf_ref, ub_ref)):
        for b in range(bsz):
            for j in range(tc // 2):
                pair = [jnp.broadcast_to(u_ref[b, 2 * j + k:2 * j + k + 1, :], (SUBLANES, S5_WIDTH)) * mask
                        for k in range(2)]
                lhs_scr[d * bsz + b, 2 * SUBLANES * j:2 * SUBLANES * (j + 1), :] = (
                    jnp.concatenate(pair, axis=0).astype(BF16))
    for c in range(n_chain):
        bu_scr[c] = _dot(lhs_scr[c], bst_ref[0, c // bsz])

    a_re = [a_ref[0, d, 0] for d in range(2)]
    a_im = [a_ref[0, d, 1] for d in range(2)]

    def step(t, carry):
        new = []
        for c in range(n_chain):
            d = c // bsz
            tt = t if d == 0 else tc - 1 - t
            r0 = pl.multiple_of(tt * SUBLANES, SUBLANES)
            h_re, h_im = carry[2 * c], carry[2 * c + 1]
            n_re = a_re[d] * h_re - a_im[d] * h_im + bu_scr[c, pl.ds(r0, SUBLANES), 0:LANES]
            n_im = a_re[d] * h_im + a_im[d] * h_re + bu_scr[c, pl.ds(r0, SUBLANES), LANES:2 * LANES]
            hs_scr[c, 0, pl.ds(r0, SUBLANES), :] = n_re
            hs_scr[c, 1, pl.ds(r0, SUBLANES), :] = n_im
            new += [n_re, n_im]
        return tuple(new)

    init = []
    for c in range(n_chain):
        init += [h_scr[c, :, 0:LANES], h_scr[c, :, LANES:2 * LANES]]
    fin = lax.fori_loop(0, tc, step, tuple(init), unroll=8)
    for c in range(n_chain):
        h_scr[c, :, 0:LANES] = fin[2 * c]
        h_scr[c, :, LANES:2 * LANES] = fin[2 * c + 1]
    hfin_ref[...] = h_scr[...]

    for d, y_ref in enumerate((yf_ref, yb_ref)):
        for b in range(bsz):
            c = d * bsz + b
            parts = [hs_scr[c, ri, pl.ds(s, tc, stride=SUBLANES), :].astype(BF16)
                     for s in range(SUBLANES) for ri in range(2)]
            y_ref[b] = _dot(jnp.concatenate(parts, axis=-1), cw_ref[0, d])


def _s5_scan(u, h0, mask, bst, a_tiles, cwide, *, layer, tc):
    bsz, seq, _ = u.shape
    nch = seq // tc
    full = lambda shape: pl.BlockSpec(shape, lambda i: (0,) * len(shape))
    fwd = pl.BlockSpec((bsz, tc, S5_WIDTH), lambda i: (0, i, 0))
    bwd = pl.BlockSpec((bsz, tc, S5_WIDTH), lambda i: (0, nch - 1 - i, 0))
    state = (2 * bsz, SUBLANES, 2 * LANES)
    rows = SUBLANES * tc
    return pl.pallas_call(
        functools.partial(_s5_kernel, bsz=bsz, tc=tc),
        grid=(nch,),
        in_specs=[fwd, bwd, full(state), full(mask.shape), _layer_spec(layer, *bst.shape[1:]),
                  _layer_spec(layer, *a_tiles.shape[1:]), _layer_spec(layer, *cwide.shape[1:])],
        out_specs=[fwd, bwd, full(state)],
        out_shape=[jax.ShapeDtypeStruct(u.shape, F32), jax.ShapeDtypeStruct(u.shape, F32),
                   jax.ShapeDtypeStruct(state, F32)],
        scratch_shapes=[pltpu.VMEM((2 * bsz, rows, S5_WIDTH), BF16),
                        pltpu.VMEM((2 * bsz, rows, 2 * LANES), F32),
                        pltpu.VMEM((2 * bsz, 2, rows, LANES), F32),
                        pltpu.VMEM(state, F32)],
        compiler_params=_cparams("arbitrary"),
        name="s5_scan",
    )(u, u, h0, mask, bst, a_tiles, cwide)


def _s5_mask():
    m = np.arange(S5_WIDTH)[None, :] // CH_PER_CHUNK == np.arange(SUBLANES)[:, None]
    return jnp.asarray(m.astype(np.float32), F32)


def _s5_out_mask():
    row_group = (np.arange(SUBLANES)[:, None, None, None] * (S5_GROUPS // SUBLANES)
                 + np.arange(S5_GROUPS // SUBLANES)[None, None, :, None]
                 + np.zeros((1, 2, 1, S5_STATE), np.int64)).reshape(-1)
    col_group = np.arange(S5_WIDTH) // S5_GROUP
    return (row_group[:, None] == col_group[None, :]).astype(np.float32)


_S5_OUT_MASK = _s5_out_mask()


def _s5_params(lam_re, lam_im, log_dt, b_re, b_im, c_re, c_im):
    dt = jnp.exp(log_dt)[..., None]
    mag = jnp.exp(lam_re * dt)
    l_re = mag * jnp.cos(lam_im * dt)
    l_im = mag * jnp.sin(lam_im * dt)
    den = lam_re * lam_re + lam_im * lam_im
    f_re = ((l_re - 1.0) * lam_re + l_im * lam_im) / den
    f_im = (l_im * lam_re - (l_re - 1.0) * lam_im) / den
    bb_re = f_re[..., None] * b_re - f_im[..., None] * b_im
    bb_im = f_re[..., None] * b_im + f_im[..., None] * b_re
    a_tiles = jnp.stack([l_re.reshape(2, SUBLANES, LANES), l_im.reshape(2, SUBLANES, LANES)], axis=1)

    half = S5_GROUPS // SUBLANES
    eye = jnp.eye(half, dtype=F32)

    def in_mat(bb):
        t = bb.reshape(2, SUBLANES, half, S5_STATE, S5_GROUP)
        m = jnp.einsum('dsgpc,gh->dsgchp', t, eye)
        return m.reshape(2, S5_WIDTH, half * S5_STATE)

    bst = jnp.concatenate([in_mat(bb_re), in_mat(bb_im)], axis=-1).astype(BF16)
    t = jnp.stack([c_re, -c_im], axis=1).astype(BF16)
    t = t.reshape(2, 2, SUBLANES, half, S5_GROUP, S5_STATE)
    rows = jnp.transpose(t, (0, 2, 1, 3, 5, 4)).reshape(2, 2 * SUBLANES * LANES, S5_GROUP)
    cwide = jnp.tile(rows, (1, 1, S5_GROUPS)) * jnp.asarray(_S5_OUT_MASK, BF16)
    return bst, a_tiles, cwide


def _conv_kernel(prev_ref, cur_ref, next_ref, wdw_ref, bdw_ref, g_ref, b_ref, wpw_ref, bpw_ref,
                 o_ref, win_ref, *, tm):
    i = pl.program_id(1)
    nt = pl.num_programs(1)
    zero = jnp.zeros((CONV_HALO, CONV_WIDTH), F32)
    win_ref[0:CONV_HALO] = jnp.where(i > 0, prev_ref[0], zero)
    win_ref[CONV_HALO:CONV_HALO + tm] = cur_ref[0]
    win_ref[CONV_HALO + tm:] = jnp.where(i < nt - 1, next_ref[0], zero)
    acc = jnp.zeros((tm, CONV_WIDTH), F32) + bdw_ref[0]
    for r in range(SUBLANES):
        z = win_ref[0:tm + SUBLANES, :] * wdw_ref[0, r:r + 1, :]
        for a in range(1, (CONV_K + 1) // SUBLANES):
            j = a * SUBLANES
            z = z + win_ref[j:j + tm + SUBLANES, :] * wdw_ref[0, j + r:j + r + 1, :]
        acc = acc + z[r:r + tm]
    h = _silu(_layer_norm(acc) * g_ref[0] + b_ref[0])
    o_ref[0] = (_dot(h.astype(BF16), wpw_ref[0]) + bpw_ref[0]).astype(BF16)


def _conformer_conv(cg, w_dw, b_dw, ln_g, ln_b, w_pw_bf, b_pw, *, layer, tm):
    bsz, seq, _ = cg.shape
    hb = tm // CONV_HALO
    last = seq // CONV_HALO - 1
    vec = _layer_spec(layer, 1, CONV_WIDTH)
    return pl.pallas_call(
        functools.partial(_conv_kernel, tm=tm),
        grid=(bsz, seq // tm),
        in_specs=[
            pl.BlockSpec((1, CONV_HALO, CONV_WIDTH), lambda b, i: (b, jnp.maximum(i * hb - 1, 0), 0)),
            pl.BlockSpec((1, tm, CONV_WIDTH), lambda b, i: (b, i, 0)),
            pl.BlockSpec((1, CONV_HALO, CONV_WIDTH), lambda b, i: (b, jnp.minimum((i + 1) * hb, last), 0)),
            _layer_spec(layer, CONV_K + 1, CONV_WIDTH),
            vec, vec, vec,
            _layer_spec(layer, CONV_WIDTH, CONV_WIDTH),
            vec,
        ],
        out_specs=pl.BlockSpec((1, tm, CONV_WIDTH), lambda b, i: (b, i, 0)),
        out_shape=jax.ShapeDtypeStruct((bsz, seq, CONV_WIDTH), BF16),
        scratch_shapes=[pltpu.VMEM((tm + 2 * CONV_HALO, CONV_WIDTH), F32)],
        compiler_params=_cparams("arbitrary", "arbitrary"),
        name="conformer_conv",
    )(cg, cg, cg, w_dw, b_dw, ln_g, ln_b, w_pw_bf, b_pw)


def _mixout_kernel(attn_ref, yf_ref, yb_ref, u_ref, conv_ref, x_ref,
                   dskip_ref, wglu_ref, bglu_ref, wout_ref, bout_ref,
                   gate_ref, g1_ref, b1_ref, shift_ref, scale_ref, wr_hi_ref, wr_lo_ref,
                   xmid_ref, h2_ref, logit_ref, *, mod_row):
    gate = _mod_vec(gate_ref, mod_row)
    shift = _mod_vec(shift_ref, mod_row)
    scale1 = 1.0 + _mod_vec(scale_ref, mod_row)
    tm = x_ref.shape[1]
    chunk = min(tm, MIX_CHUNK)
    for r0 in range(0, tm, chunk):
        rows = slice(r0, r0 + chunk)
        y = dskip_ref[0] * u_ref[0, rows, :] + yf_ref[0, rows, :] + yb_ref[0, rows, :]
        z = _gelu_tanh(y)
        s5 = z * _sigmoid(_dot(z.astype(BF16), wglu_ref[0]) + bglu_ref[0])
        y_mix = (_dot(attn_ref[0, rows, :], wout_ref[0, 0:ATTN_WIDTH, :])
                 + _dot(s5.astype(BF16), wout_ref[0, ATTN_WIDTH:ATTN_WIDTH + S5_WIDTH, :])
                 + _dot(conv_ref[0, rows, :], wout_ref[0, ATTN_WIDTH + S5_WIDTH:, :])
                 + bout_ref[0])
        r = DEEPNORM_ALPHA * x_ref[0, rows, :] + gate * y_mix
        x_mid = _layer_norm(r) * g1_ref[0] + b1_ref[0]
        xmid_ref[0, rows, :] = x_mid
        h2 = _layer_norm(x_mid) * scale1 + shift
        h2_ref[0, rows, :] = h2.astype(BF16)
        h_hi, h_lo = _split_bf16(h2)
        logit_ref[0, :, rows] = (_dot_nt(wr_hi_ref[0], h_hi)
                                 + (_dot_nt(wr_hi_ref[0], h_lo) + _dot_nt(wr_lo_ref[0], h_hi)))


def _mixer_output(attn, yf, yb, u, conv, x, d_skip, w_glu_bf, b_glu, w_out_bf, b_out,
                  mods, ln_g, ln_b, wr_hi, wr_lo, *, layer, mod_row, tm):
    bsz, seq, _ = x.shape
    tok = lambda w: pl.BlockSpec((1, tm, w), lambda b, i: (b, i, 0))
    lay = lambda r, c: _layer_spec(layer, r, c)
    return pl.pallas_call(
        functools.partial(_mixout_kernel, mod_row=mod_row),
        grid=(bsz, seq // tm),
        in_specs=[
            tok(ATTN_WIDTH), tok(S5_WIDTH), tok(S5_WIDTH), tok(S5_WIDTH), tok(CONV_WIDTH), tok(D_MODEL),
            lay(1, S5_WIDTH), lay(S5_WIDTH, S5_WIDTH), lay(1, S5_WIDTH),
            lay(D_MODEL, D_MODEL), lay(1, D_MODEL),
            _mod_spec(layer, 2), lay(1, D_MODEL), lay(1, D_MODEL), _mod_spec(layer, 3), _mod_spec(layer, 4),
            lay(N_EXPERTS, D_MODEL), lay(N_EXPERTS, D_MODEL),
        ],
        out_specs=[tok(D_MODEL), tok(D_MODEL), pl.BlockSpec((1, N_EXPERTS, tm), lambda b, i: (b, 0, i))],
        out_shape=[
            jax.ShapeDtypeStruct((bsz, seq, D_MODEL), F32),
            jax.ShapeDtypeStruct((bsz, seq, D_MODEL), BF16),
            jax.ShapeDtypeStruct((bsz, N_EXPERTS, seq), F32),
        ],
        compiler_params=_cparams("arbitrary", "arbitrary"),
        name="mixer_output",
    )(attn, yf, yb, u, conv, x, d_skip, w_glu_bf, b_glu, w_out_bf, b_out,
      mods, ln_g, ln_b, mods, mods, wr_hi, wr_lo)


def _token_cumsum(m, tri, ones):
    nblk = m.shape[0]
    m2 = m.reshape(nblk * N_EXPERTS, LANES).astype(BF16)
    within = _dot(m2, tri).reshape(nblk, N_EXPERTS, LANES)
    tot = _dot(m2, ones).reshape(nblk, N_EXPERTS, LANES)
    offs = []
    run = jnp.zeros((N_EXPERTS, LANES), F32)
    for j in range(nblk):
        offs.append(run)
        run = run + tot[j]
    off = jnp.stack(offs, axis=0)
    return within + off, off, tot


def _router_kernel(logit_ref, tri_ref, ones_ref, aff_ref, sel_ref, pos_ref, off_ref, cnt_ref, *, cap, slot0):
    b = pl.program_id(0)
    nblk = aff_ref.shape[1]

    def soft(j, carry):
        r0 = pl.multiple_of(j * LANES, LANES)
        t = logit_ref[0, 0:N_EXPERTS, pl.ds(r0, LANES)]
        ex = jnp.exp(t - jnp.max(t, axis=0, keepdims=True))
        aff_ref[0, j] = ex / jnp.sum(ex, axis=0, keepdims=True)
        return carry

    lax.fori_loop(0, nblk, soft, 0, unroll=min(nblk, 4))
    aff = aff_ref[0]

    def enough(cand):
        cnt = jnp.sum(jnp.where(aff >= cand[None], 1.0, 0.0), axis=0)
        return jnp.sum(cnt, axis=-1, keepdims=True) >= cap

    p = jnp.full((N_EXPERTS, LANES), 2.0, F32)
    for k in range(6, -1, -1):
        cand = p * (2.0 ** -(2 ** k))
        p = jnp.where(enough(cand), p, cand)
    thr = 0.5 * p
    thr = jnp.where(enough(thr), thr, 0.0)

    def refine(_, carry):
        lo, step = carry
        cand = lo + step
        return jnp.where(enough(cand), cand, lo), 0.5 * step

    thr, _ = lax.fori_loop(0, 23, refine, (thr, 0.5 * thr))
    gt = aff > thr[None]
    eq = aff == thr[None]
    n_gt = jnp.sum(jnp.sum(jnp.where(gt, 1.0, 0.0), axis=0), axis=-1, keepdims=True)
    need = cap - n_gt
    tri = tri_ref[...]
    ones = ones_ref[...]
    cum_eq, _, _ = _token_cumsum(jnp.where(eq, 1.0, 0.0), tri, ones)
    sel = jnp.where(gt | (eq & (cum_eq <= need[None])), 1.0, 0.0)
    cum_sel, off, cnt = _token_cumsum(sel, tri, ones)
    base = (slot0 + b * cap).astype(F32)
    sel_ref[0] = sel
    pos_ref[0] = cum_sel - sel + base
    off_ref[0] = off + base
    cnt_ref[0] = cnt


def _router(logits, *, slot0):
    bsz, _, n = logits.shape
    nblk = n // LANES
    cap = CAPACITY_FACTOR * n // N_EXPERTS
    idx = np.arange(LANES)
    tri = jnp.asarray((idx[:, None] <= idx[None, :]).astype(np.float32), BF16)
    ones = jnp.ones((LANES, LANES), BF16)
    shape = (bsz, nblk, N_EXPERTS, LANES)
    out = pl.BlockSpec((1, nblk, N_EXPERTS, LANES), lambda b: (b, 0, 0, 0))
    sq = pl.BlockSpec((LANES, LANES), lambda b: (0, 0))
    return pl.pallas_call(
        functools.partial(_router_kernel, cap=cap, slot0=slot0),
        grid=(bsz,),
        in_specs=[pl.BlockSpec((1, N_EXPERTS, n), lambda b: (b, 0, 0)), sq, sq],
        out_specs=[out] * 5,
        out_shape=[jax.ShapeDtypeStruct(shape, F32)] * 5,
        compiler_params=_cparams("arbitrary"),
        name="router",
    )(logits, tri, ones)


COMBINE_KW = LANES + 16
SMALL_COUNT = 32
COMBINE_SMALL_KW = SMALL_COUNT + 16
FF_TILE = 256


def _dispatch_most(tb_ref, n_entries, e, blocks):
    most = tb_ref[n_entries + blocks[0][0] * N_EXPERTS + e]
    for tile, _ in blocks[1:]:
        most = jnp.maximum(most, tb_ref[n_entries + tile * N_EXPERTS + e])
    return most


def _dispatch_run(tb_ref, e, blocks, xs_buf, g_buf, kw):
    k = lax.broadcasted_iota(jnp.int32, (kw, LANES), 0).astype(F32)
    for tile, load in blocks:
        h2_blk, pos_row, sel_row, aff_row = load()
        base = tb_ref[tile * N_EXPERTS + e]
        base16 = pl.multiple_of(lax.shift_left(lax.shift_right_logical(base, 4), 4), 16)
        hit = (k == (pos_row - base16.astype(F32))) & (sel_row > 0.5)
        sel_t = jnp.where(hit, 1.0, 0.0).astype(BF16)
        xs_buf[pl.ds(base16, kw), :] += _dot(sel_t, h2_blk).astype(BF16)
        g = jnp.sum(jnp.where(hit, aff_row, 0.0), axis=-1, keepdims=True)
        g_buf[pl.ds(base16, kw), :] += jnp.broadcast_to(g, (kw, LANES))


def _moe_kernel(*refs, n_ctx, rows, n_entries):
    tb_ref = refs[0]
    if n_ctx:
        (posx_ref, selx_ref, affx_ref, h2x_ref, posc_ref, selc_ref, affc_ref, h2c_ref,
         wg_ref, wu_ref, wd_ref, y_ref, acc_scr, g_scr, xs_scr) = refs[1:]
    else:
        (posx_ref, selx_ref, affx_ref, h2x_ref, wg_ref, wu_ref, wd_ref, y_ref,
         acc_scr, g_scr, xs_scr) = refs[1:]
    p = pl.program_id(0)
    s = pl.program_id(1)
    last = pl.num_programs(1) - 1
    fill = lax.rem(p, 2)
    use = 1 - fill
    blk_per_step = posx_ref.shape[2]
    e = jnp.minimum(p, N_EXPERTS - 1)
    compacting = p < N_EXPERTS
    computing = p >= 1
    xs_buf = xs_scr.at[fill]
    g_buf = g_scr.at[fill]

    @pl.when(compacting & (s == 0))
    def _():
        xs_buf[...] = jnp.zeros(xs_buf.shape, BF16)
        g_buf[...] = jnp.zeros(g_buf.shape, F32)
        if n_ctx:
            bsz_c, blk_c = posc_ref.shape[0], posc_ref.shape[2]

            def load_c(b, i):
                return lambda: (h2c_ref[b, i * LANES:(i + 1) * LANES, :], posc_ref[b, 0, i:i + 1, :],
                                selc_ref[b, 0, i:i + 1, :], affc_ref[b, 0, i:i + 1, :])

            blocks_c = [((last + 1) * blk_per_step + b * blk_c + i, load_c(b, i))
                        for b in range(bsz_c) for i in range(blk_c)]
            _dispatch_run(tb_ref, e, blocks_c, xs_buf, g_buf, COMBINE_KW)

    def load_x(i):
        return lambda: (h2x_ref[0, i * LANES:(i + 1) * LANES, :], posx_ref[0, 0, i:i + 1, :],
                        selx_ref[0, 0, i:i + 1, :], affx_ref[0, 0, i:i + 1, :])

    blocks = [(s * blk_per_step + i, load_x(i)) for i in range(blk_per_step)]
    short = _dispatch_most(tb_ref, n_entries, e, blocks) <= SMALL_COUNT

    def ffn_tile():
        wg = wg_ref[0, 0].astype(BF16)
        wu = wu_ref[0, 0].astype(BF16)
        wd = wd_ref[0, 0].astype(BF16)
        half = rows // 2
        for r0 in (0, half):
            xs = xs_scr[use, r0:r0 + half, :]
            hid = _silu(_dot(xs, wg)) * _dot(xs, wu)
            part = _dot(hid.astype(BF16), wd)
            acc_scr[r0:r0 + half, :] = jnp.where(s == 0, part, acc_scr[r0:r0 + half, :] + part)

    for window, fits in ((COMBINE_SMALL_KW, short), (COMBINE_KW, jnp.logical_not(short))):
        @pl.when(compacting & computing & fits)
        def _(window=window):
            ffn_tile()
            _dispatch_run(tb_ref, e, blocks, xs_buf, g_buf, window)

        @pl.when(compacting & jnp.logical_not(computing) & fits)
        def _(window=window):
            _dispatch_run(tb_ref, e, blocks, xs_buf, g_buf, window)

    @pl.when(jnp.logical_not(compacting))
    def _():
        ffn_tile()

    @pl.when(computing & (s == last))
    def _():
        g = g_scr[use, 0:rows, :]
        gated = acc_scr[...] * jnp.concatenate([g] * (D_MODEL // LANES), axis=-1)
        y_ref[0, 0:rows, :] = gated.astype(BF16)
        y_ref[0, rows:, :] = jnp.zeros((y_ref.shape[1] - rows, D_MODEL), BF16)


def _moe_experts(tb, route_x, h2x, route_c, h2c, w_gate, w_up, w_down, *, layer):
    bsz, n, d = h2x.shape
    n_ff = EXPERT_FF // FF_TILE
    x_step = bsz * n // n_ff
    steps_per_b = n // x_step
    blk_per_step = x_step // LANES
    n_ctx = 0 if h2c is None else h2c.shape[0] * h2c.shape[1]
    rows = bsz * CAPACITY_FACTOR * n // N_EXPERTS + CAPACITY_FACTOR * n_ctx // N_EXPERTS
    rows_pad = rows + COMBINE_KW
    last_e = N_EXPERTS - 1

    def xs_idx(p, s):
        sc = jnp.where(p > last_e, n_ff - 1, s)
        return sc // steps_per_b, sc % steps_per_b

    def route_x_spec():
        return pl.BlockSpec((1, 1, blk_per_step, LANES),
                            lambda p, s, tb: (xs_idx(p, s)[0], jnp.minimum(p, last_e), xs_idx(p, s)[1], 0))

    in_specs = [route_x_spec(), route_x_spec(), route_x_spec(),
                pl.BlockSpec((1, x_step, d), lambda p, s, tb: (xs_idx(p, s)[0], xs_idx(p, s)[1], 0))]
    args = list(route_x) + [h2x]
    if n_ctx:
        bc, nc, _ = h2c.shape
        rc = pl.BlockSpec((bc, 1, nc // LANES, LANES), lambda p, s, tb: (0, jnp.minimum(p, last_e), 0, 0))
        in_specs += [rc, rc, rc, pl.BlockSpec((bc, nc, d), lambda p, s, tb: (0, 0, 0))]
        args += list(route_c) + [h2c]
    ffn_e = lambda p: jnp.maximum(p - 1, 0)
    in_specs += [pl.BlockSpec((1, 1, d, FF_TILE), lambda p, s, tb: (layer, ffn_e(p), 0, s)),
                 pl.BlockSpec((1, 1, d, FF_TILE), lambda p, s, tb: (layer, ffn_e(p), 0, s)),
                 pl.BlockSpec((1, 1, FF_TILE, d), lambda p, s, tb: (layer, ffn_e(p), s, 0))]
    args += [w_gate, w_up, w_down]
    return pl.pallas_call(
        functools.partial(_moe_kernel, n_ctx=n_ctx, rows=rows, n_entries=tb.shape[0] // 2),
        grid_spec=pltpu.PrefetchScalarGridSpec(
            num_scalar_prefetch=1,
            grid=(N_EXPERTS + 1, n_ff),
            in_specs=in_specs,
            out_specs=pl.BlockSpec((1, rows_pad, d), lambda p, s, tb: (ffn_e(p), 0, 0)),
            scratch_shapes=[pltpu.VMEM((rows, d), F32), pltpu.VMEM((2, rows_pad, LANES), F32),
                            pltpu.VMEM((2, rows_pad, d), BF16)],
        ),
        out_shape=jax.ShapeDtypeStruct((N_EXPERTS, rows_pad, d), BF16),
        compiler_params=_cparams("arbitrary", "arbitrary"),
        name="moe_experts",
    )(tb, *args)


def _combine_kernel(tb_ref, pos_ref, sel_ref, xmid_ref, gate_ref, g_ref, b_ref, y_hbm, o_ref,
                    y_scr, stack_scr, sem, *, cap, slot0, blk_per_step, nblk, n_entries, mod_row):
    b = pl.program_id(0)
    j = pl.program_id(1)
    win = y_scr.shape[1]
    gate = _mod_vec(gate_ref, mod_row)

    @pl.when(j == 0)
    def _():
        row0 = pl.multiple_of(slot0 + b * cap, 16)
        cp = pltpu.make_async_copy(y_hbm.at[:, pl.ds(row0, win), :], y_scr, sem)
        cp.start()
        cp.wait()

    set_base = (slot0 + b * cap).astype(F32)
    tn_dims = (((0,), (0,)), ((), ()))
    for i in range(blk_per_step):
        entry = (b * nblk + j * blk_per_step + i) * N_EXPERTS
        rows = slice(i * LANES, (i + 1) * LANES)

        def hits(e, kw, entry=entry, i=i):
            base = tb_ref[entry + e] - (slot0 + b * cap)
            base16 = pl.multiple_of(lax.shift_left(lax.shift_right_logical(base, 4), 4), 16)
            k = lax.broadcasted_iota(jnp.int32, (kw, LANES), 0).astype(F32)
            rel = pos_ref[0, i, e:e + 1, :] - (set_base + base16.astype(F32))
            hit = (k == rel) & (sel_ref[0, i, e:e + 1, :] > 0.5)
            return base16, jnp.where(hit, 1.0, 0.0).astype(BF16)

        def finish(moe, rows=rows):
            r = DEEPNORM_ALPHA * xmid_ref[0, rows, :] + gate * moe
            o_ref[0, rows, :] = _layer_norm(r) * g_ref[0] + b_ref[0]

        most = tb_ref[n_entries + entry]
        for e in range(1, N_EXPERTS):
            most = jnp.maximum(most, tb_ref[n_entries + entry + e])

        @pl.when(most <= SMALL_COUNT)
        def _():
            sel_all = []
            for e in range(N_EXPERTS):
                base16, sel_t = hits(e, COMBINE_SMALL_KW)
                sel_all.append(sel_t)
                stack_scr[e * COMBINE_SMALL_KW:(e + 1) * COMBINE_SMALL_KW, :] = (
                    y_scr[e, pl.ds(base16, COMBINE_SMALL_KW), :])
            finish(lax.dot_general(jnp.concatenate(sel_all, axis=0), stack_scr[...], tn_dims,
                                   preferred_element_type=F32))

        @pl.when(most > SMALL_COUNT)
        def _():
            acc = jnp.zeros((LANES, D_MODEL), F32)
            for e in range(N_EXPERTS):
                base16, sel_t = hits(e, COMBINE_KW)
                acc = acc + lax.dot_general(sel_t, y_scr[e, pl.ds(base16, COMBINE_KW), :], tn_dims,
                                            preferred_element_type=F32)
            finish(acc)


def _combine_post(tb, pos, sel, x_mid, mods, ln_g, ln_b, y, *, layer, mod_row, slot0, tm):
    bsz, n, d = x_mid.shape
    nblk = n // LANES
    cap = CAPACITY_FACTOR * n // N_EXPERTS
    blk_per_step = tm // LANES
    tok = pl.BlockSpec((1, tm, d), lambda b, j, tb: (b, j, 0))
    route = pl.BlockSpec((1, blk_per_step, N_EXPERTS, LANES), lambda b, j, tb: (b, j, 0, 0))
    vec = _layer_spec(layer, 1, d)
    return pl.pallas_call(
        functools.partial(_combine_kernel, cap=cap, slot0=slot0, blk_per_step=blk_per_step, nblk=nblk,
                          n_entries=bsz * nblk * N_EXPERTS, mod_row=mod_row),
        grid_spec=pltpu.PrefetchScalarGridSpec(
            num_scalar_prefetch=1,
            grid=(bsz, n // tm),
            in_specs=[route, route, tok, _mod_spec(layer, 5), vec, vec,
                      pl.BlockSpec(memory_space=pl.ANY)],
            out_specs=tok,
            scratch_shapes=[pltpu.VMEM((N_EXPERTS, cap + COMBINE_KW, d), BF16),
                            pltpu.VMEM((N_EXPERTS * COMBINE_SMALL_KW, d), BF16),
                            pltpu.SemaphoreType.DMA(())],
        ),
        out_shape=jax.ShapeDtypeStruct(x_mid.shape, F32),
        compiler_params=_cparams("arbitrary", "arbitrary"),
        name="combine_post",
    )(tb, pos, sel, x_mid, mods, ln_g, ln_b, y)


def _tile_table(*lane_replicated):
    return jnp.concatenate([a[..., 0].astype(jnp.int32).reshape(-1) for a in lane_replicated])


def _expert_major(a):
    return jnp.swapaxes(a, 1, 2)


def kernel(x, c, ctx, c_ctx, w_mod, b_mod, w_in, b_in, attn_sink, s5_lam_re, s5_lam_im, s5_log_dt,
           s5_b_re, s5_b_im, s5_c_re, s5_c_im, s5_d, s5_w_glu, s5_b_glu, conv_w_dw, conv_b_dw,
           conv_ln_g, conv_ln_b, conv_w_pw, conv_b_pw, w_out, b_out, ln1_g, ln1_b, w_router,
           exp_w_gate, exp_w_up, exp_w_down, ln2_g, ln2_b):
    bsz, seq, d = x.shape
    lc = ctx.shape[1]
    tm_x, tm_c = min(1024, seq), lc

    cond = jnp.zeros((SUBLANES, d), F32).at[:bsz].set(c).at[bsz].set(c_ctx)
    mods = _modulation(cond, w_mod, b_mod)
    cos_t, sin_t = _rope_tables(seq)
    s5_mask = _s5_mask()
    zero_state = jnp.zeros((2 * bsz, SUBLANES, 2 * LANES), F32)

    vec3 = lambda a: a.reshape(DEPTH, 1, -1)
    w_in_bf, w_out_bf = w_in.astype(BF16), w_out.astype(BF16)
    w_glu_bf, w_pw_bf = s5_w_glu.astype(BF16), conv_w_pw.astype(BF16)
    wr = jnp.swapaxes(w_router, 1, 2)
    wr_hi, wr_lo = _split_bf16(wr)
    sink_rep = jnp.broadcast_to(attn_sink[:, :, None], (DEPTH, N_Q_HEADS, LANES))
    bst, a_tiles, cwide = jax.vmap(_s5_params)(s5_lam_re, s5_lam_im, s5_log_dt, s5_b_re, s5_b_im,
                                               s5_c_re, s5_c_im)
    w_dw = jnp.pad(conv_w_dw.reshape(DEPTH, CONV_K, CONV_WIDTH), ((0, 0), (1, 0), (0, 0)))
    conv_args = (w_dw, vec3(conv_b_dw), vec3(conv_ln_g), vec3(conv_ln_b), w_pw_bf, vec3(conv_b_pw))
    mix_args = (vec3(s5_d), w_glu_bf, vec3(s5_b_glu), w_out_bf, vec3(b_out), mods, vec3(ln1_g), vec3(ln1_b),
                wr_hi, wr_lo)
    b_in3, ln2_g3, ln2_b3 = vec3(b_in), vec3(ln2_g), vec3(ln2_b)

    xc = ctx
    ctx_row = bsz
    for l in range(DEPTH):
        last = l == DEPTH - 1
        q, k, v, u, cg = _in_projection(x, mods, w_in_bf, b_in3, cos_t, sin_t,
                                        layer=l, mod_row=None, rope=True, tm=tm_x)
        q_c, k_c, v_c, u_c, cg_c = _in_projection(xc, mods, w_in_bf, b_in3, cos_t, sin_t,
                                                  layer=l, mod_row=ctx_row, rope=False, tm=tm_c)

        attn_x = _attention(q, k, v, k_c, v_c, sink_rep, layer=l, window=True)
        yf_c, yb_c, h_ctx = _s5_scan(u_c, zero_state, s5_mask, bst, a_tiles, cwide, layer=l, tc=lc)
        yf, yb, _ = _s5_scan(u, h_ctx, s5_mask, bst, a_tiles, cwide, layer=l, tc=256)
        conv_x = _conformer_conv(cg, *conv_args, layer=l, tm=min(512, seq))
        x_mid, h2, logits = _mixer_output(attn_x, yf, yb, u, conv_x, x, *mix_args,
                                          layer=l, mod_row=None, tm=tm_x)
        aff_x, sel_x, pos_x, off_x, cnt_x = _router(logits, slot0=0)
        route_x = tuple(_expert_major(a) for a in (pos_x, sel_x, aff_x))
        tb_x = _tile_table(off_x, cnt_x)
        if not last:
            attn_c = _attention(q_c, None, None, k_c, v_c, sink_rep, layer=l, window=False)
            conv_c = _conformer_conv(cg_c, *conv_args, layer=l, tm=lc)
            xc_mid, hc2, logits_c = _mixer_output(attn_c, yf_c, yb_c, u_c, conv_c, xc, *mix_args,
                                                  layer=l, mod_row=ctx_row, tm=tm_c)
            slot0_c = bsz * CAPACITY_FACTOR * seq // N_EXPERTS
            aff_c, sel_c, pos_c, off_c, cnt_c = _router(logits_c, slot0=slot0_c)
            route_c = tuple(_expert_major(a) for a in (pos_c, sel_c, aff_c))
            tb_c = _tile_table(off_c, cnt_c)
            y = _moe_experts(_tile_table(off_x, off_c, cnt_x, cnt_c), route_x, h2, route_c, hc2,
                             exp_w_gate, exp_w_up, exp_w_down, layer=l)
            xc = _combine_post(tb_c, pos_c, sel_c, xc_mid, mods, ln2_g3, ln2_b3, y,
                               layer=l, mod_row=ctx_row, slot0=slot0_c, tm=lc)
        else:
            y = _moe_experts(tb_x, route_x, h2, None, None, exp_w_gate, exp_w_up, exp_w_down, layer=l)
        x = _combine_post(tb_x, pos_x, sel_x, x_mid, mods, ln2_g3, ln2_b3, y,
                          layer=l, mod_row=None, slot0=0, tm=min(512, seq))
    return x
```
